```python
import jax, jax.numpy as jnp
from jax import lax
import numpy as np

D_MODEL = 1024
BATCH = 2
SEQ = 8192
DEPTH = 2
DEC_BATCH = 128
DEC_SEQ = 4
PAST_LEN = 2048
PAGE_SIZE = 128

HEAD_DIM = 64
N_HEADS = D_MODEL // HEAD_DIM
NSA_KV_HEADS = 4
NSA_GROUP = N_HEADS // NSA_KV_HEADS
CMP_BLOCK = 64
SLC_BLOCK = 64
N_SELECT = 16
WINDOW = 512
FOX_HEADS = N_HEADS
ROT_DIM = HEAD_DIM // 4
ROPE_THETA = 500000.0
D_FF = ((8 * D_MODEL // 3) + 127) // 128 * 128
CONV_W = 3
Q_BLOCK = 128
N_MIXERS = 2
N_NSA = (DEPTH + 1) // 2
N_FOX = DEPTH // 2
QD = N_HEADS * HEAD_DIM
KVD = NSA_KV_HEADS * HEAD_DIM
NSA_IN = QD + 6 * KVD + 3 * N_HEADS
FD = FOX_HEADS * HEAD_DIM
FOX_IN = 4 * FD + FOX_HEADS
EPS = 1e-6
NEG = -1e30

kernel_name = 'nsa_fox_hybrid_decoder_step'


def _rmsnorm(x, g):
    x32 = x.astype(jnp.float32)
    y = x32 * lax.rsqrt(jnp.mean(x32 * x32, axis=-1, keepdims=True) + EPS)
    return (y * g.astype(jnp.float32)).astype(x.dtype)


def _rope(x, pos):
    half = ROT_DIM // 2
    freqs = ROPE_THETA ** (-jnp.arange(0, ROT_DIM, 2, dtype=jnp.float32) / ROT_DIM)
    ang = pos.astype(jnp.float32)[:, None] * freqs[None, :]
    cos = jnp.cos(ang)[:, None, :]
    sin = jnp.sin(ang)[:, None, :]
    x1 = x[..., :half].astype(jnp.float32)
    x2 = x[..., half:ROT_DIM].astype(jnp.float32)
    rot = jnp.concatenate([x1 * cos - x2 * sin, x2 * cos + x1 * sin], axis=-1).astype(x.dtype)
    return jnp.concatenate([rot, x[..., ROT_DIM:]], axis=-1)


def _adaln(c, w, b):
    mod = jax.nn.silu(c) @ w + b
    return jnp.split(mod[:, None, :], 6, axis=-1)


def _query_blocks(a, nqb):
    b = a.shape[0]
    return jnp.moveaxis(a.reshape((b, nqb, Q_BLOCK) + a.shape[2:]), 1, 0)


def _conv_ffn(h, buf, w_up, cw, cb, w_down):
    t = h.shape[1]
    up = h @ w_up
    ext = jnp.concatenate([buf.astype(up.dtype), up], axis=1)
    mixed = cb + cw[0] * ext[:, 0:t]
    for j in range(1, CONV_W):
        mixed = mixed + cw[j] * ext[:, j:j + t]
    a, g = jnp.split(mixed, 2, axis=-1)
    return (jax.nn.silu(g) * a) @ w_down, ext[:, t:]


def _nsa_project(h, w_in, pos):
    b, t, _ = h.shape
    z = h @ w_in
    q = z[..., :QD].reshape(b, t, N_HEADS, HEAD_DIM)
    kv = z[..., QD:QD + 6 * KVD].reshape(b, t, 6, NSA_KV_HEADS, HEAD_DIM)
    gates = jax.nn.sigmoid(z[..., QD + 6 * KVD:]).reshape(b, t, N_HEADS, 3)
    q_rot = _rope(q, pos)
    rows = jnp.stack([kv[:, :, 0], kv[:, :, 1], _rope(kv[:, :, 2], pos), kv[:, :, 3]], axis=2)
    win = jnp.stack([_rope(kv[:, :, 4], pos), kv[:, :, 5]], axis=2)
    return q, q_rot, gates, rows, win


def _compress(raw, pe, w1, w2):
    b, n = raw.shape[0], raw.shape[1] // CMP_BLOCK
    blk = raw[:, :n * CMP_BLOCK].reshape(b, n, CMP_BLOCK, NSA_KV_HEADS, HEAD_DIM) + pe[None, None, :, None, :]
    hid = jax.nn.silu(jnp.einsum('bnlgd,lde->bnge', blk, w1))
    return jnp.einsum('bnge,ef->bngf', hid, w2)


def _to_blocks(x, nb):
    b, t = x.shape[:2]
    x = jnp.pad(x, ((0, 0), (0, nb * SLC_BLOCK - t), (0, 0), (0, 0)))
    return x.reshape(b, nb, SLC_BLOCK, NSA_KV_HEADS, HEAD_DIM).transpose(0, 3, 1, 2, 4)


def _nsa_memory(rows, pe_c, w_c1, w_c2):
    n_len = rows.shape[1]
    kc = _compress(rows[:, :, 0], pe_c[0], w_c1[0], w_c2[0])
    vc = _compress(rows[:, :, 1], pe_c[1], w_c1[1], w_c2[1])
    nbs = -(-n_len // SLC_BLOCK)
    return kc, vc, _to_blocks(rows[:, :, 2], nbs), _to_blocks(rows[:, :, 3], nbs)


def _nsa_branches(q, q_rot, gates, t_pos, kc, vc, ks_blk, vs_blk, kw, vw, kw_pos):
    b, nq = q.shape[:2]
    g_, r_ = NSA_KV_HEADS, NSA_GROUP
    scale = HEAD_DIM ** -0.5
    qg = q.reshape(b, nq, g_, r_, HEAD_DIM)
    qrg = q_rot.reshape(b, nq, g_, r_, HEAD_DIM)
    nbc = kc.shape[1]
    sc = jnp.einsum('bqgrd,bngd->bgrqn', qg, kc).astype(jnp.float32) * scale
    cmask = ((jnp.arange(nbc) + 1) * CMP_BLOCK - 1)[None, :] <= t_pos[:, None]
    p_c = jax.nn.softmax(jnp.where(cmask, sc, NEG), axis=-1) * cmask
    o_c = jnp.einsum('bgrqn,bngd->bqgrd', p_c.astype(vc.dtype), vc)
    nbs = ks_blk.shape[2]
    imp = jnp.pad(p_c.sum(axis=2), ((0, 0), (0, 0), (0, 0), (0, nbs - nbc)))
    blk = jnp.arange(nbs)[None, :]
    cur = (t_pos // SLC_BLOCK)[:, None]
    forced = (blk == 0) | (blk == cur) | (blk == cur - 1)
    future = blk * SLC_BLOCK > t_pos[:, None]
    imp = jnp.where(forced, jnp.inf, jnp.where(future, -jnp.inf, imp))
    n_sel = min(N_SELECT, nbs)
    _, idx = lax.top_k(imp, n_sel)
    gather = jax.vmap(jax.vmap(lambda blocks, ids: blocks[ids]))
    ksel = gather(ks_blk, idx).reshape(b, g_, nq, n_sel * SLC_BLOCK, HEAD_DIM)
    vsel = gather(vs_blk, idx).reshape(b, g_, nq, n_sel * SLC_BLOCK, HEAD_DIM)
    kpos = (idx[..., None] * SLC_BLOCK + jnp.arange(SLC_BLOCK)).reshape(b, g_, nq, n_sel * SLC_BLOCK)
    ss = jnp.einsum('bqgrd,bgqkd->bgrqk', qrg, ksel).astype(jnp.float32) * scale
    smask = (kpos <= t_pos[None, None, :, None])[:, :, None]
    p_s = jax.nn.softmax(jnp.where(smask, ss, NEG), axis=-1)
    o_s = jnp.einsum('bgrqk,bgqkd->bqgrd', p_s.astype(vsel.dtype), vsel)
    sw = jnp.einsum('bqgrd,bkgd->bgrqk', qrg, kw).astype(jnp.float32) * scale
    dpos = t_pos[:, None] - kw_pos[None, :]
    wmask = (dpos >= 0) & (dpos < WINDOW) & (kw_pos[None, :] >= 0)
    p_w = jax.nn.softmax(jnp.where(wmask, sw, NEG), axis=-1)
    o_w = jnp.einsum('bgrqk,bkgd->bqgrd', p_w.astype(vw.dtype), vw)
    gg = gates.reshape(b, nq, g_, r_, 3)
    o = gg[..., 0:1] * o_c + gg[..., 1:2] * o_s + gg[..., 2:3] * o_w
    return o.reshape(b, nq, QD)


def _nsa_prompt(h, w_in, pe_c, w_c1, w_c2, w_out):
    b, s, _ = h.shape
    pos = jnp.arange(s, dtype=jnp.int32)
    q, q_rot, gates, rows, win = _nsa_project(h, w_in, pos)
    kc, vc, ks_blk, vs_blk = _nsa_memory(rows, pe_c, w_c1, w_c2)
    win_pad = jnp.pad(win, ((0, 0), (WINDOW, 0), (0, 0), (0, 0), (0, 0)))
    nqb = s // Q_BLOCK

    def block(args):
        qb, qrb, gb, s0 = args
        t_pos = s0 + jnp.arange(Q_BLOCK, dtype=jnp.int32)
        wb = lax.dynamic_slice_in_dim(win_pad, s0, WINDOW + Q_BLOCK, axis=1)
        kw_pos = s0 - WINDOW + jnp.arange(WINDOW + Q_BLOCK, dtype=jnp.int32)
        return _nsa_branches(qb, qrb, gb, t_pos, kc, vc, ks_blk, vs_blk, wb[:, :, 0], wb[:, :, 1], kw_pos)

    o = lax.map(block, (_query_blocks(q, nqb), _query_blocks(q_rot, nqb), _query_blocks(gates, nqb),
                        jnp.arange(nqb, dtype=jnp.int32) * Q_BLOCK))
    o = jnp.moveaxis(o, 0, 1).reshape(b, s, QD)
    return o @ w_out, rows, win[:, s - min(WINDOW, s):]


def _nsa_sample(h, pool, win_buf, page_table, w_in, pe_c, w_c1, w_c2, w_out):
    b, t, _ = h.shape
    past = page_table.shape[1] * PAGE_SIZE
    pos = past + jnp.arange(t, dtype=jnp.int32)
    q, q_rot, gates, new_rows, new_win = _nsa_project(h, w_in, pos)
    past_rows = pool[page_table].reshape((b, past) + pool.shape[2:])
    rows = jnp.concatenate([past_rows.astype(new_rows.dtype), new_rows], axis=1)
    kc, vc, ks_blk, vs_blk = _nsa_memory(rows, pe_c, w_c1, w_c2)
    wrows = jnp.concatenate([win_buf.astype(new_win.dtype), new_win], axis=1)
    kw_pos = past - win_buf.shape[1] + jnp.arange(wrows.shape[1], dtype=jnp.int32)
    o = _nsa_branches(q, q_rot, gates, pos, kc, vc, ks_blk, vs_blk, wrows[:, :, 0], wrows[:, :, 1], kw_pos)
    return o @ w_out, new_rows, new_win


def _fox_project(h, w_in, b_f, g_q, g_k):
    b, t, _ = h.shape
    z = h @ w_in
    hs = (b, t, FOX_HEADS, HEAD_DIM)
    q = _rmsnorm(z[..., :FD].reshape(hs), g_q)
    k = _rmsnorm(z[..., FD:2 * FD].reshape(hs), g_k)
    v = z[..., 2 * FD:3 * FD].reshape(hs)
    og = jax.nn.sigmoid(z[..., 3 * FD:4 * FD])
    logf = jax.nn.log_sigmoid(z[..., 4 * FD:].astype(jnp.float32) + b_f.astype(jnp.float32))
    return q, k, v, og, logf


def _fox_attend(q, t_pos, k, v, cq, ck_t):
    s = jnp.einsum('bqhd,bkhd->bhqk', q, k).astype(jnp.float32) * HEAD_DIM ** -0.5
    s = s + jnp.swapaxes(cq, 1, 2)[..., None] - ck_t[:, :, None, :]
    kpos = jnp.arange(k.shape[1], dtype=jnp.int32)
    p = jax.nn.softmax(jnp.where(kpos[None, :] <= t_pos[:, None], s, NEG), axis=-1)
    return jnp.einsum('bhqk,bkhd->bqhd', p.astype(v.dtype), v)


def _fox_prompt(h, w_in, b_f, g_q, g_k, w_out):
    b, s, _ = h.shape
    q, k, v, og, logf = _fox_project(h, w_in, b_f, g_q, g_k)
    c = jnp.cumsum(logf, axis=1)
    ck_t = jnp.swapaxes(c, 1, 2)
    nqb = s // Q_BLOCK

    def block(args):
        qb, cqb, s0 = args
        return _fox_attend(qb, s0 + jnp.arange(Q_BLOCK, dtype=jnp.int32), k, v, cqb, ck_t)

    o = lax.map(block, (_query_blocks(q, nqb), _query_blocks(c, nqb), jnp.arange(nqb, dtype=jnp.int32) * Q_BLOCK))
    o = jnp.moveaxis(o, 0, 1).reshape(b, s, FD)
    return (o * og) @ w_out, jnp.stack([k, v], axis=2), logf


def _fox_sample(h, kv_pool, lf_pool, page_table, w_in, b_f, g_q, g_k, w_out):
    b, t, _ = h.shape
    past = page_table.shape[1] * PAGE_SIZE
    q, k, v, og, logf = _fox_project(h, w_in, b_f, g_q, g_k)
    past_kv = kv_pool[page_table].reshape((b, past) + kv_pool.shape[2:]).astype(k.dtype)
    past_lf = lf_pool[page_table].reshape(b, past, FOX_HEADS).astype(jnp.float32)
    k_all = jnp.concatenate([past_kv[:, :, 0], k], axis=1)
    v_all = jnp.concatenate([past_kv[:, :, 1], v], axis=1)
    c = jnp.cumsum(jnp.concatenate([past_lf, logf], axis=1), axis=1)
    o = _fox_attend(q, past + jnp.arange(t, dtype=jnp.int32), k_all, v_all, c[:, past:], jnp.swapaxes(c, 1, 2))
    return (o.reshape(b, t, FD) * og) @ w_out, jnp.stack([k, v], axis=2), logf


def setup_inputs(seed: int = 0) -> dict:
    key = jax.random.key(seed)
    ks = jax.random.split(key, 40)
    f32 = jnp.float32
    n_pages = PAST_LEN // PAGE_SIZE
    n_pool = (DEC_BATCH * n_pages * 5) // 4
    win_buf = min(WINDOW, PAST_LEN)

    def nrm(k, shape, s=1.0):
        return s * jax.random.normal(k, shape, f32)

    page_table = jax.random.permutation(ks[0], n_pool)[:DEC_BATCH * n_pages].reshape(DEC_BATCH, n_pages).astype(jnp.int32)
    return {
        'x_prompt': nrm(ks[1], (BATCH, SEQ, D_MODEL)),
        'x_sample': nrm(ks[2], (DEC_BATCH, DEC_SEQ, D_MODEL)),
        'cache_nsa_kv': nrm(ks[3], (N_NSA, n_pool, PAGE_SIZE, 4, NSA_KV_HEADS, HEAD_DIM)),
        'cache_nsa_win': nrm(ks[4], (N_NSA, DEC_BATCH, win_buf, 2, NSA_KV_HEADS, HEAD_DIM)),
        'cache_fox_kv': nrm(ks[5], (N_FOX, n_pool, PAGE_SIZE, 2, FOX_HEADS, HEAD_DIM)),
        'cache_fox_logf': jax.nn.log_sigmoid(4.0 + nrm(ks[6], (N_FOX, n_pool, PAGE_SIZE, FOX_HEADS), 0.5)),
        'state_ffn_conv': nrm(ks[7], (DEPTH, DEC_BATCH, CONV_W - 1, 2 * D_FF)),
        'page_table': page_table,
        'c_prompt': nrm(ks[8], (BATCH, D_MODEL)),
        'c_sample': nrm(ks[9], (DEC_BATCH, D_MODEL)),
        'w_ada': nrm(ks[10], (DEPTH, D_MODEL, 6 * D_MODEL), 0.5 * D_MODEL ** -0.5),
        'b_ada': nrm(ks[11], (DEPTH, 6 * D_MODEL), 0.02),
        'norm_mix_g': 1.0 + nrm(ks[12], (DEPTH, D_MODEL), 0.1),
        'norm_ffn_g': 1.0 + nrm(ks[13], (DEPTH, D_MODEL), 0.1),
        'w_nsa_in': nrm(ks[14], (N_NSA, D_MODEL, NSA_IN), D_MODEL ** -0.5),
        'pe_cmp': nrm(ks[15], (N_NSA, 2, CMP_BLOCK, HEAD_DIM), 0.1),
        'w_cmp1': nrm(ks[16], (N_NSA, 2, CMP_BLOCK, HEAD_DIM, HEAD_DIM), (CMP_BLOCK * HEAD_DIM) ** -0.5),
        'w_cmp2': nrm(ks[17], (N_NSA, 2, HEAD_DIM, HEAD_DIM), HEAD_DIM ** -0.5),
        'w_nsa_out': nrm(ks[18], (N_NSA, QD, D_MODEL), QD ** -0.5),
        'w_fox_in': nrm(ks[19], (N_FOX, D_MODEL, FOX_IN), D_MODEL ** -0.5),
        'b_fox_f': 4.0 + nrm(ks[20], (N_FOX, FOX_HEADS), 0.5),
        'fox_q_norm_g': 1.0 + nrm(ks[21], (N_FOX, HEAD_DIM), 0.1),
        'fox_k_norm_g': 1.0 + nrm(ks[22], (N_FOX, HEAD_DIM), 0.1),
        'w_fox_out': nrm(ks[23], (N_FOX, FD, D_MODEL), FD ** -0.5),
        'w_ffn_up': nrm(ks[24], (DEPTH, D_MODEL, 2 * D_FF), D_MODEL ** -0.5),
        'ffn_conv_w': nrm(ks[25], (DEPTH, CONV_W, 2 * D_FF), CONV_W ** -0.5),
        'ffn_conv_b': nrm(ks[26], (DEPTH, 2 * D_FF), 0.02),
        'w_ffn_down': nrm(ks[27], (DEPTH, D_FF, D_MODEL), D_FF ** -0.5),
        'final_norm_g': 1.0 + nrm(ks[28], (D_MODEL,), 0.1),
    }


def reference(x_prompt, x_sample, cache_nsa_kv, cache_nsa_win, cache_fox_kv, cache_fox_logf, state_ffn_conv,
              page_table, c_prompt, c_sample, w_ada, b_ada, norm_mix_g, norm_ffn_g, w_nsa_in, pe_cmp, w_cmp1,
              w_cmp2, w_nsa_out, w_fox_in, b_fox_f, fox_q_norm_g, fox_k_norm_g, w_fox_out, w_ffn_up, ffn_conv_w,
              ffn_conv_b, w_ffn_down, final_norm_g):
    xp, xs = x_prompt, x_sample
    nsa_kv_p, nsa_kv_s, nsa_win_p, nsa_win_s = [], [], [], []
    fox_kv_p, fox_kv_s, fox_lf_p, fox_lf_s = [], [], [], []
    conv_p, conv_s = [], []
    for i in range(DEPTH):
        j = i // N_MIXERS
        mp = _adaln(c_prompt, w_ada[i], b_ada[i])
        ms = _adaln(c_sample, w_ada[i], b_ada[i])
        hp = _rmsnorm(xp, norm_mix_g[i]) * (1.0 + mp[1]) + mp[0]
        hs = _rmsnorm(xs, norm_mix_g[i]) * (1.0 + ms[1]) + ms[0]
        if i % N_MIXERS == 0:
            yp, rp, wp = _nsa_prompt(hp, w_nsa_in[j], pe_cmp[j], w_cmp1[j], w_cmp2[j], w_nsa_out[j])
            ys, rs, ws = _nsa_sample(hs, cache_nsa_kv[j], cache_nsa_win[j], page_table, w_nsa_in[j], pe_cmp[j],
                                     w_cmp1[j], w_cmp2[j], w_nsa_out[j])
            nsa_kv_p.append(rp)
            nsa_kv_s.append(rs)
            nsa_win_p.append(wp)
            nsa_win_s.append(ws)
        else:
            yp, kvp, lfp = _fox_prompt(hp, w_fox_in[j], b_fox_f[j], fox_q_norm_g[j], fox_k_norm_g[j], w_fox_out[j])
            ys, kvs, lfs = _fox_sample(hs, cache_fox_kv[j], cache_fox_logf[j], page_table, w_fox_in[j], b_fox_f[j],
                                       fox_q_norm_g[j], fox_k_norm_g[j], w_fox_out[j])
            fox_kv_p.append(kvp)
            fox_kv_s.append(kvs)
            fox_lf_p.append(lfp)
            fox_lf_s.append(lfs)
        xp = xp + mp[2] * yp
        xs = xs + ms[2] * ys
        hp = _rmsnorm(xp, norm_ffn_g[i]) * (1.0 + mp[4]) + mp[3]
        hs = _rmsnorm(xs, norm_ffn_g[i]) * (1.0 + ms[4]) + ms[3]
        buf0 = jnp.zeros((xp.shape[0], CONV_W - 1, 2 * D_FF), xp.dtype)
        fp, bp = _conv_ffn(hp, buf0, w_ffn_up[i], ffn_conv_w[i], ffn_conv_b[i], w_ffn_down[i])
        fs, bs = _conv_ffn(hs, state_ffn_conv[i], w_ffn_up[i], ffn_conv_w[i], ffn_conv_b[i], w_ffn_down[i])
        conv_p.append(bp)
        conv_s.append(bs)
        xp = xp + mp[5] * fp
        xs = xs + ms[5] * fs
    y_prompt = _rmsnorm(xp, final_norm_g)
    y_sample = _rmsnorm(xs, final_norm_g)
    return (y_prompt, y_sample, jnp.stack(nsa_kv_p), jnp.stack(nsa_kv_s), jnp.stack(nsa_win_p), jnp.stack(nsa_win_s),
            jnp.stack(fox_kv_p), jnp.stack(fox_kv_s), jnp.stack(fox_lf_p), jnp.stack(fox_lf_s),
            jnp.stack(conv_p), jnp.stack(conv_s))
```

```python
import functools

import jax
import jax.numpy as jnp
from jax import lax
from jax.experimental import pallas as pl
from jax.experimental.pallas import tpu as pltpu

F32 = jnp.float32
BF16 = jnp.bfloat16

HEAD_DIM = 64
N_HEADS = 16
N_GROUPS = 4
GROUP = N_HEADS // N_GROUPS
BLOCK = 64
N_SELECT = 16
WINDOW = 512
ROT_DIM = 16
ROPE_THETA = 500000.0
PAGE = 128
Q_BLOCK = 128
EPS = 1e-6
NEG = -1e30
SCALE = HEAD_DIM ** -0.5

LANES = 128
VMEM_LIMIT = 56 * 1024 * 1024


def _params(*sem):
    return pltpu.CompilerParams(dimension_semantics=sem, vmem_limit_bytes=VMEM_LIMIT)


def _dot(a, b):
    return jnp.dot(a, b, preferred_element_type=F32)


def _dot_nt(a, b):
    return lax.dot_general(a, b, (((1,), (1,)), ((), ())), preferred_element_type=F32)


def _split3(x):
    hi = x.astype(BF16)
    r1 = x - hi.astype(F32)
    mid = r1.astype(BF16)
    lo = (r1 - mid.astype(F32)).astype(BF16)
    return hi, mid, lo


def _ada_kernel(c_ref, w_ref, b_ref, o_ref):
    c = c_ref[...]
    a = (c * jax.nn.sigmoid(c)).astype(BF16)
    o_ref[...] = _dot(a, w_ref[...]) + b_ref[...]


def _adaln(c, w, b):
    r, d = c.shape
    n = w.shape[1]
    tn = n // 4
    return pl.pallas_call(
        _ada_kernel,
        grid=(n // tn,),
        in_specs=[pl.BlockSpec((r, d), lambda j: (0, 0)),
                  pl.BlockSpec((d, tn), lambda j: (0, j)),
                  pl.BlockSpec((1, tn), lambda j: (0, j))],
        out_specs=pl.BlockSpec((r, tn), lambda j: (0, j)),
        out_shape=jax.ShapeDtypeStruct((r, n), F32),
        compiler_params=_params("arbitrary"),
        name="adaln",
    )(c, w, b)


def _norm_mod(x, g, scale, shift):
    ms = jnp.mean(x * x, axis=-1, keepdims=True)
    return (x * lax.rsqrt(ms + EPS) * g) * (1.0 + scale) + shift


def _rope_tables(pos):
    freqs = ROPE_THETA ** (-jnp.arange(0, ROT_DIM, 2, dtype=F32) / ROT_DIM)
    ang = pos.astype(F32)[:, None] * freqs[None, :]
    cos, sin = jnp.cos(ang), jnp.sin(ang)
    half = ROT_DIM // 2
    one = jnp.ones((pos.shape[0], HEAD_DIM - ROT_DIM), F32)
    zero = jnp.zeros_like(one)
    zh = jnp.zeros_like(cos)
    c = jnp.concatenate([cos, cos, one], axis=1)
    s_lo = jnp.concatenate([zh, sin, zero], axis=1)
    s_hi = jnp.concatenate([-sin, zh, zero], axis=1)
    rep = LANES // HEAD_DIM
    return jnp.tile(c, (1, rep)), jnp.tile(s_lo, (1, rep)), jnp.tile(s_hi, (1, rep))


def _rope(v, c, s_lo, s_hi):
    half = ROT_DIM // 2
    return v * c + pltpu.roll(v, half, 1) * s_lo + pltpu.roll(v, LANES - half, 1) * s_hi


def _row_specs(mode, tm, tpb, d):
    if mode == "prompt":
        mod = pl.BlockSpec((1, 1, d), lambda i: (i // tpb, 0, 0))
        tab = pl.BlockSpec((tm, LANES), lambda i: (i % tpb, 0))
    else:
        mod = pl.BlockSpec((1, tm, d), lambda i: (0, 0, 0))
        tab = pl.BlockSpec((tm, LANES), lambda i: (0, 0))
    return mod, tab


NSA_DUP = 4 * N_GROUPS * LANES
NSA_W_COLS = 1024 + 6 * 256 + N_GROUPS * LANES


def _nsa_proj_kernel(x_ref, g_ref, sc_ref, sh_ref, w_ref, tc_ref, tl_ref, th_ref,
                     qc_ref, qr_ref, rows_ref, win_ref, dup_ref, gates_ref):
    h = _norm_mod(x_ref[...], g_ref[...], sc_ref[0], sh_ref[0]).astype(BF16)
    z = _dot(h, w_ref[...])
    tc, tl, th = tc_ref[...], tl_ref[...], th_ref[...]
    lo = lax.broadcasted_iota(jnp.int32, (z.shape[0], LANES), 1) < HEAD_DIM

    def chunk(j):
        return z[:, j * LANES:(j + 1) * LANES]

    def put_dup(kind, pair, v):
        vr = pltpu.roll(v, HEAD_DIM, 1)
        base = (kind * N_GROUPS + 2 * pair) * LANES
        dup_ref[:, base:base + LANES] = jnp.where(lo, v, vr).astype(BF16)
        dup_ref[:, base + LANES:base + 2 * LANES] = jnp.where(lo, vr, v).astype(BF16)

    for j in range(8):
        v = chunk(j)
        sl = slice(j * LANES, (j + 1) * LANES)
        qc_ref[:, sl] = (v * SCALE).astype(BF16)
        qr_ref[:, sl] = (_rope(v, tc, tl, th) * SCALE).astype(BF16)
    for j in range(4):
        rows_ref[:, j * LANES:(j + 1) * LANES] = chunk(8 + j)
    for j in range(2):
        ks = _rope(chunk(12 + j), tc, tl, th)
        vs = chunk(14 + j)
        rows_ref[:, (4 + j) * LANES:(5 + j) * LANES] = ks
        rows_ref[:, (6 + j) * LANES:(7 + j) * LANES] = vs
        put_dup(0, j, ks)
        put_dup(1, j, vs)
    for j in range(2):
        kw = _rope(chunk(16 + j), tc, tl, th)
        vw = chunk(18 + j)
        win_ref[:, j * LANES:(j + 1) * LANES] = kw
        win_ref[:, (2 + j) * LANES:(3 + j) * LANES] = vw
        put_dup(2, j, kw)
        put_dup(3, j, vw)
    for j in range(N_GROUPS):
        gates_ref[:, j * LANES:(j + 1) * LANES] = jax.nn.sigmoid(chunk(20 + j))


def _nsa_proj(x, g, scale, shift, w, tabs, mode, tm, tpb):
    r, d = x.shape
    mod, tab = _row_specs(mode, tm, tpb, d)
    row = lambda n: pl.BlockSpec((tm, n), lambda i: (i, 0))
    outs = [(1024, BF16), (1024, BF16), (1024, F32), (512, F32), (NSA_DUP, BF16), (N_GROUPS * LANES, F32)]
    return pl.pallas_call(
        _nsa_proj_kernel,
        grid=(r // tm,),
        in_specs=[row(d), pl.BlockSpec((1, d), lambda i: (0, 0)), mod, mod,
                  pl.BlockSpec(w.shape, lambda i: (0, 0)), tab, tab, tab],
        out_specs=[row(n) for n, _ in outs],
        out_shape=[jax.ShapeDtypeStruct((r, n), t) for n, t in outs],
        compiler_params=_params("parallel"),
        name="nsa_proj",
    )(x, g, scale, shift, w, *tabs)


def _nsa_weight(w_in):
    d = w_in.shape[0]
    main = w_in[:, :1024 + 6 * 256]
    gates = w_in[:, 1024 + 6 * 256:].reshape(d, N_GROUPS, GROUP * 3)
    gates = jnp.pad(gates, ((0, 0), (0, 0), (0, LANES - GROUP * 3))).reshape(d, N_GROUPS * LANES)
    return jnp.concatenate([main, gates], axis=1).astype(BF16)


def _cmp_kernel(x_ref, pe_ref, w1_ref, w2_ref, o_ref, acc_ref):
    lc = pl.program_id(2)

    @pl.when(lc == 0)
    def _():
        acc_ref[...] = jnp.zeros_like(acc_ref)

    acc = acc_ref[...]
    for l in range(x_ref.shape[1]):
        xl = (x_ref[0, l] + pe_ref[0, l:l + 1, :]).astype(BF16)
        acc = acc + _dot(xl, w1_ref[0, l])
    acc_ref[...] = acc

    @pl.when(lc == pl.num_programs(2) - 1)
    def _():
        hid = acc * jax.nn.sigmoid(acc)
        o_ref[0, 0] = _dot(hid.astype(BF16), w2_ref[0]).astype(BF16)


def _block_diag(w, n):
    eye = jnp.eye(n, dtype=w.dtype)
    out = jnp.einsum("ij,...ab->...iajb", eye, w)
    return out.reshape(w.shape[:-2] + (n * w.shape[-2], n * w.shape[-1]))


def _cmp_weights(pe, w1, w2):
    pe_t = jnp.tile(pe, (1, 1, N_GROUPS))
    w1_bd = _block_diag(w1, N_GROUPS).astype(BF16)
    w2_dup = jnp.concatenate([w2, w2], axis=-1)
    w2_bd = _block_diag(w2_dup, N_GROUPS).astype(BF16)
    return pe_t, w1_bd, w2_bd


def _compress_prompt(rows, pe_t, w1_bd, w2_bd, b, s):
    nb = s // BLOCK
    gd = N_GROUPS * HEAD_DIM
    xt = rows.reshape(b, nb, BLOCK, -1)[..., :2 * gd].transpose(0, 2, 1, 3)
    lstep = 8
    return pl.pallas_call(
        _cmp_kernel,
        grid=(b, 2, BLOCK // lstep),
        in_specs=[pl.BlockSpec((1, lstep, nb, gd), lambda i, kv, lc: (i, lc, 0, kv)),
                  pl.BlockSpec((1, lstep, gd), lambda i, kv, lc: (kv, lc, 0)),
                  pl.BlockSpec((1, lstep, gd, gd), lambda i, kv, lc: (kv, lc, 0, 0)),
                  pl.BlockSpec((1, gd, N_GROUPS * LANES), lambda i, kv, lc: (kv, 0, 0))],
        out_specs=pl.BlockSpec((1, 1, nb, N_GROUPS * LANES), lambda i, kv, lc: (i, kv, 0, 0)),
        out_shape=jax.ShapeDtypeStruct((b, 2, nb, N_GROUPS * LANES), BF16),
        scratch_shapes=[pltpu.VMEM((nb, gd), F32)],
        compiler_params=_params("parallel", "parallel", "arbitrary"),
        name="nsa_compress",
    )(xt, pe_t, w1_bd, w2_bd)


def _softmax_rows(s):
    m = jnp.max(s, axis=-1, keepdims=True)
    p = jnp.exp(s - m)
    return p, jnp.sum(p, axis=-1, keepdims=True)


def _top_blocks(v, n_sel):
    nb = v.shape[-1]
    n = lax.broadcasted_iota(jnp.int32, v.shape, 1)
    sel = jnp.zeros(v.shape, F32)
    for _ in range(n_sel):
        mx = jnp.max(v, axis=-1, keepdims=True)
        idx = jnp.min(jnp.where(v == mx, n, nb), axis=-1, keepdims=True)
        hit = n == idx
        sel = jnp.where(hit, 1.0, sel)
        v = jnp.where(hit, -jnp.inf, v)
    return sel


def _stack_heads(ref):
    q = ref.shape[1]
    lo = lax.broadcasted_iota(jnp.int32, (q, LANES), 1) < HEAD_DIM
    parts = []
    for r in range(GROUP):
        pair = ref[0, :, (r // 2) * LANES:(r // 2 + 1) * LANES]
        parts.append(jnp.where(lo if r % 2 == 0 else jnp.logical_not(lo), pair, jnp.zeros_like(pair)))
    return jnp.concatenate(parts, axis=0)


def _nsa_attn_kernel(qc_ref, qr_ref, kc_ref, vc_ref, ks_ref, vs_ref, kw_ref, vw_ref, e_ref, gt_ref, o_ref,
                     *, tk, wlen):
    nq = Q_BLOCK
    s0 = pl.program_id(2) * nq
    t_q = s0 + lax.broadcasted_iota(jnp.int32, (nq, 1), 0)
    rep = lambda a: jnp.concatenate([a] * GROUP, axis=0)

    qc = _stack_heads(qc_ref)
    kc = kc_ref[0, 0]
    nb = kc.shape[0]
    n = lax.broadcasted_iota(jnp.int32, (nq, nb), 1)
    cmask = rep(n * BLOCK + (BLOCK - 1) <= t_q)
    sc = jnp.where(cmask, _dot_nt(qc, kc), NEG)
    e = jnp.where(cmask, jnp.exp(sc - jnp.max(sc, axis=-1, keepdims=True)), 0.0)
    l = jnp.sum(e, axis=-1, keepdims=True)
    pc = e / jnp.where(l > 0.0, l, 1.0)
    o_c = _dot(pc.astype(BF16), vc_ref[0, 0])

    imp = pc[0:nq] + pc[nq:2 * nq] + pc[2 * nq:3 * nq] + pc[3 * nq:4 * nq]
    cur = t_q // BLOCK
    forced = (n == 0) | (n == cur) | (n == cur - 1)
    future = n * BLOCK > t_q
    sel = _top_blocks(jnp.where(forced, jnp.inf, jnp.where(future, -jnp.inf, imp)), min(N_SELECT, nb))
    selb = sel.astype(BF16)

    qr = _stack_heads(qr_ref)
    col = lax.broadcasted_iota(jnp.int32, (nq, tk), 1)

    def body(j, carry):
        m_i, l_i, acc = carry
        k0 = pl.multiple_of(j * tk, tk)
        s = _dot_nt(qr, ks_ref[0, pl.ds(k0, tk), :])
        ok = (_dot(selb, e_ref[j]) > 0.5) & (k0 + col <= t_q)
        s = jnp.where(rep(ok), s, NEG)
        m_n = jnp.maximum(m_i, jnp.max(s, axis=-1, keepdims=True))
        alpha = jnp.exp(m_i - m_n)
        p = jnp.exp(s - m_n)
        l_n = alpha * l_i + jnp.sum(p, axis=-1, keepdims=True)
        acc = alpha * acc + _dot(p.astype(BF16), vs_ref[0, pl.ds(k0, tk), :])
        return m_n, l_n, acc

    init = (jnp.full((GROUP * nq, 1), NEG, F32), jnp.zeros((GROUP * nq, 1), F32),
            jnp.zeros((GROUP * nq, LANES), F32))
    _, l_s, acc_s = lax.fori_loop(0, (s0 + nq + tk - 1) // tk, body, init)
    o_s = acc_s / l_s

    w0 = pl.multiple_of(jnp.maximum(s0 + nq - wlen, 0), nq)
    kpos = w0 + lax.broadcasted_iota(jnp.int32, (nq, wlen), 1)
    dpos = t_q - kpos
    wmask = rep((dpos >= 0) & (dpos < WINDOW))
    sw = jnp.where(wmask, _dot_nt(qr, kw_ref[0, pl.ds(w0, wlen), :]), NEG)
    pw, lw = _softmax_rows(sw)
    o_w = _dot(pw.astype(BF16), vw_ref[0, pl.ds(w0, wlen), :]) / lw

    gt = gt_ref[0]
    lo = lax.broadcasted_iota(jnp.int32, (nq, LANES), 1) < HEAD_DIM
    outs = []
    for r in range(GROUP):
        sl = slice(r * nq, (r + 1) * nq)
        outs.append(gt[:, 3 * r:3 * r + 1] * o_c[sl] + gt[:, 3 * r + 1:3 * r + 2] * o_s[sl]
                    + gt[:, 3 * r + 2:3 * r + 3] * o_w[sl])
    for pr in range(GROUP // 2):
        o_ref[0, :, pr * LANES:(pr + 1) * LANES] = jnp.where(lo, outs[2 * pr], outs[2 * pr + 1]).astype(BF16)


def _nsa_attention(qc, qr, cdup, dup, gates, b, s):
    nb = s // BLOCK
    tk = min(512, s)
    wlen = min(WINDOW + Q_BLOCK, s)
    blk = (jnp.arange(s) // BLOCK).reshape(s // tk, 1, tk)
    expand = (jnp.arange(nb)[None, :, None] == blk).astype(BF16)
    g4 = N_GROUPS
    qspec = pl.BlockSpec((1, Q_BLOCK, 2 * LANES), lambda i, g, q: (i, q, g))
    dspec = lambda kind: pl.BlockSpec((1, s, LANES), lambda i, g, q: (i, 0, kind * g4 + g))
    return pl.pallas_call(
        functools.partial(_nsa_attn_kernel, tk=tk, wlen=wlen),
        grid=(b, N_GROUPS, s // Q_BLOCK),
        in_specs=[qspec, qspec,
                  pl.BlockSpec((1, 1, nb, LANES), lambda i, g, q: (i, 0, 0, g)),
                  pl.BlockSpec((1, 1, nb, LANES), lambda i, g, q: (i, 1, 0, g)),
                  dspec(0), dspec(1), dspec(2), dspec(3),
                  pl.BlockSpec(expand.shape, lambda i, g, q: (0, 0, 0)),
                  pl.BlockSpec((1, Q_BLOCK, LANES), lambda i, g, q: (i, q, g))],
        out_specs=qspec,
        out_shape=jax.ShapeDtypeStruct((b, s, 1024), BF16),
        compiler_params=_params("parallel", "parallel", "arbitrary"),
        name="nsa_attention",
    )(qc.reshape(b, s, -1), qr.reshape(b, s, -1), cdup, cdup,
      dup.reshape(b, s, -1), dup.reshape(b, s, -1), dup.reshape(b, s, -1), dup.reshape(b, s, -1),
      expand, gates.reshape(b, s, -1))


FFN_CHUNK = 256


def _ffn_kernel(*refs, u, tpb, chain, final, nf):
    (x_ref, a_ref, wo_ref, gm_ref, g_ref, sc_ref, sh_ref, gf_ref, wu_ref, cw_ref, cb_ref, wd_ref) = refs[:12]
    k = 12
    st_ref = gfin_ref = y_ref = carry_scr = None
    if not chain:
        st_ref = refs[k]
        k += 1
    if final:
        gfin_ref = refs[k]
        k += 1
    xo_ref, tail_ref = refs[k], refs[k + 1]
    k += 2
    if final:
        y_ref = refs[k]
        k += 1
    ext_scr = refs[k]
    if chain:
        carry_scr = refs[k + 1]

    fc = FFN_CHUNK
    tm = x_ref.shape[0]
    base = ext_scr.shape[0] - tm
    tail = tail_ref.shape[0]
    x1 = x_ref[...] + gm_ref[0] * _dot(a_ref[...], wo_ref[...])
    h = _norm_mod(x1, g_ref[...], sc_ref[0], sh_ref[0]).astype(BF16)
    if chain:
        first = (pl.program_id(0) % tpb) == 0
    acc = jnp.zeros((tm, x_ref.shape[1]), F32)
    for f in range(nf):
        cs = slice(f * 2 * fc, (f + 1) * 2 * fc)
        up = _dot(h, wu_ref[:, cs])
        if chain:
            ext_scr[0:base, :] = jnp.where(first, 0.0, carry_scr[f])
            carry_scr[f] = up[tm - base:, :]
        else:
            ext_scr[0:base, :] = st_ref[:, cs]
        ext_scr[base:, :] = up
        tail_ref[:, cs] = up[tm - tail:, :]
        cw = cw_ref[:, cs]
        mixed = (cb_ref[:, cs] + cw[0:1] * ext_scr[base - 2 * u:base - 2 * u + tm, :]
                 + cw[1:2] * ext_scr[base - u:base - u + tm, :] + cw[2:3] * up)
        a, g = mixed[:, :fc], mixed[:, fc:]
        act = (g * jax.nn.sigmoid(g) * a).astype(BF16)
        acc = acc + _dot(act, wd_ref[f * fc:(f + 1) * fc, :])
    xn = x1 + gf_ref[0] * acc
    xo_ref[...] = xn
    if final:
        ms = jnp.mean(xn * xn, axis=-1, keepdims=True)
        y_ref[...] = xn * lax.rsqrt(ms + EPS) * gfin_ref[...]


def _ffn_perm(f_dim):
    nf = f_dim // FFN_CHUNK
    idx = []
    for f in range(nf):
        idx.append(jnp.arange(f * FFN_CHUNK, (f + 1) * FFN_CHUNK))
        idx.append(f_dim + jnp.arange(f * FFN_CHUNK, (f + 1) * FFN_CHUNK))
    return jnp.concatenate(idx)


def _ffn(x, attn, w_out, mods, g, wu, cw, cb, wd, mode, tm, tpb, u, state=None, final_g=None):
    r, d = x.shape
    f2 = wu.shape[1]
    nf = f2 // (2 * FFN_CHUNK)
    chain = state is None
    final = final_g is not None
    base = 8 if chain else 2 * u
    tail = 8 if chain else 2 * u
    mod, _ = _row_specs(mode, tm, tpb, d)
    row = lambda n: pl.BlockSpec((tm, n), lambda i: (i, 0))
    full = lambda a: pl.BlockSpec(a.shape, lambda i: (0,) * a.ndim)
    args = [x, attn, w_out, mods[0], g, mods[1], mods[2], mods[3], wu, cw, cb, wd]
    specs = [row(d), row(d), full(w_out), mod, full(g), mod, mod, mod, full(wu), full(cw), full(cb), full(wd)]
    if not chain:
        args.append(state)
        specs.append(full(state))
    if final:
        args.append(final_g)
        specs.append(full(final_g))
    out_specs = [row(d), pl.BlockSpec((tail, f2), lambda i: (i, 0))]
    out_shape = [jax.ShapeDtypeStruct((r, d), F32), jax.ShapeDtypeStruct((r // tm * tail, f2), F32)]
    if final:
        out_specs.append(row(d))
        out_shape.append(jax.ShapeDtypeStruct((r, d), F32))
    scratch = [pltpu.VMEM((base + tm, 2 * FFN_CHUNK), F32)]
    if chain:
        scratch.append(pltpu.VMEM((nf, base, 2 * FFN_CHUNK), F32))
    return pl.pallas_call(
        functools.partial(_ffn_kernel, u=u, tpb=tpb, chain=chain, final=final, nf=nf),
        grid=(r // tm,),
        in_specs=specs, out_specs=out_specs, out_shape=out_shape, scratch_shapes=scratch,
        compiler_params=_params("arbitrary"),
        name="out_proj_ffn",
    )(*args)


FOX_W_COLS = 4 * 1024 + LANES


def _fox_proj_kernel(x_ref, g_ref, sc_ref, sh_ref, w_ref, ind_ref, indt_ref, gq_ref, gk_ref, bf_ref,
                     q_ref, kf_ref, kb_ref, vf_ref, vb_ref, og_ref, lf_ref, c_ref, carry_scr, *, tpb):
    h = _norm_mod(x_ref[...], g_ref[...], sc_ref[0], sh_ref[0]).astype(BF16)
    z = _dot(h, w_ref[...])
    tm = z.shape[0]
    ind, indt = ind_ref[...], indt_ref[...]

    def head_norm(zc, gain):
        sq = zc * zc
        hi = sq.astype(BF16)
        lo = (sq - hi.astype(F32)).astype(BF16)
        ms = (_dot(hi, ind) + _dot(lo, ind)) * (1.0 / HEAD_DIM)
        rinv = lax.rsqrt(ms + EPS)
        rh = rinv.astype(BF16)
        rl = (rinv - rh.astype(F32)).astype(BF16)
        return zc * (_dot(rh, indt) + _dot(rl, indt)) * gain

    q_ref[...] = (head_norm(z[:, 0:1024], gq_ref[...]) * SCALE).astype(BF16)
    kn = head_norm(z[:, 1024:2048], gk_ref[...])
    kf_ref[...] = kn
    kb_ref[...] = kn.astype(BF16)
    v = z[:, 2048:3072]
    vf_ref[...] = v
    vb_ref[...] = v.astype(BF16)
    og_ref[...] = jax.nn.sigmoid(z[:, 3072:4096])
    zf = z[:, 4096:4096 + LANES] + bf_ref[...]
    lf = jnp.minimum(zf, 0.0) - jnp.log1p(jnp.exp(-jnp.abs(zf)))
    lf = jnp.where(lax.broadcasted_iota(jnp.int32, lf.shape, 1) < N_HEADS, lf, 0.0)
    lf_ref[...] = lf

    @pl.when(pl.program_id(0) % tpb == 0)
    def _():
        carry_scr[...] = jnp.zeros_like(carry_scr)

    tri = (lax.broadcasted_iota(jnp.int32, (tm, tm), 0) >= lax.broadcasted_iota(jnp.int32, (tm, tm), 1)).astype(BF16)
    hi, mid, lo = _split3(lf)
    c = _dot(tri, hi) + _dot(tri, mid) + _dot(tri, lo) + carry_scr[0:1, :]
    c_ref[...] = c
    carry_scr[0:1, :] = c[tm - 1:tm, :]


def _fox_weight(w_in):
    d = w_in.shape[0]
    return jnp.pad(w_in, ((0, 0), (0, FOX_W_COLS - w_in.shape[1]))).astype(BF16)


def _fox_proj(x, g, scale, shift, w, gq, gk, bf, mode, tm, tpb):
    r, d = x.shape
    mod, _ = _row_specs(mode, tm, tpb, d)
    row = lambda n: pl.BlockSpec((tm, n), lambda i: (i, 0))
    full = lambda a: pl.BlockSpec(a.shape, lambda i: (0,) * a.ndim)
    head_of = jnp.arange(1024) // HEAD_DIM
    ind = (head_of[:, None] == jnp.arange(LANES)[None, :]).astype(BF16)
    gq_t = jnp.tile(gq, N_HEADS)[None]
    gk_t = jnp.tile(gk, N_HEADS)[None]
    bf_p = jnp.pad(bf, (0, LANES - bf.shape[0]))[None]
    outs = [(1024, BF16), (1024, F32), (1024, BF16), (1024, F32), (1024, BF16), (1024, F32), (LANES, F32), (LANES, F32)]
    return pl.pallas_call(
        functools.partial(_fox_proj_kernel, tpb=tpb),
        grid=(r // tm,),
        in_specs=[row(d), full(g), mod, mod, full(w), full(ind), full(ind.T), full(gq_t), full(gk_t), full(bf_p)],
        out_specs=[row(n) for n, _ in outs],
        out_shape=[jax.ShapeDtypeStruct((r, n), t) for n, t in outs],
        scratch_shapes=[pltpu.VMEM((8, LANES), F32)],
        compiler_params=_params("arbitrary"),
        name="fox_proj",
    )(x, g, scale, shift, w, ind, ind.T, gq_t, gk_t, bf_p)


def _fox_attn_kernel(q_ref, k_ref, v_ref, nc_ref, og_ref, o_ref, *, t, nt):
    qi = pl.program_id(2)
    q = q_ref[0]
    lo = lax.broadcasted_iota(jnp.int32, (t, LANES), 1) < HEAD_DIM
    t_q = qi * t + lax.broadcasted_iota(jnp.int32, (t, 1), 0)
    col = lax.broadcasted_iota(jnp.int32, (t, t), 1)
    outs = []
    for h2 in range(2):
        qh = jnp.where(lo if h2 == 0 else jnp.logical_not(lo), q, jnp.zeros_like(q))

        def tile(j, carry, masked):
            m_i, l_i, acc = carry
            k0 = pl.multiple_of(j * t, t)
            s = _dot_nt(qh, k_ref[0, pl.ds(k0, t), :]) + nc_ref[0, 0, h2 * nt + j]
            if masked:
                s = jnp.where(k0 + col <= t_q, s, NEG)
            m_n = jnp.maximum(m_i, jnp.max(s, axis=-1, keepdims=True))
            alpha = jnp.exp(m_i - m_n)
            p = jnp.exp(s - m_n)
            l_n = alpha * l_i + jnp.sum(p, axis=-1, keepdims=True)
            acc = alpha * acc + _dot(p.astype(BF16), v_ref[0, pl.ds(k0, t), :])
            return m_n, l_n, acc

        init = (jnp.full((t, 1), NEG, F32), jnp.zeros((t, 1), F32), jnp.zeros((t, LANES), F32))
        carry = lax.fori_loop(0, qi, lambda j, c: tile(j, c, False), init)
        _, l_f, acc_f = tile(qi, carry, True)
        outs.append(acc_f / l_f)
    o_ref[0] = (jnp.where(lo, outs[0], outs[1]) * og_ref[0]).astype(BF16)


def _fox_attention(q, kb, vb, c, og, b, s):
    t = min(512, s)
    nt = s // t
    hp = N_HEADS // 2
    negc = -c[:, :N_HEADS].reshape(b, nt, t, hp, 2).transpose(0, 3, 4, 1, 2).reshape(b, hp, 2 * nt, 1, t)
    qspec = pl.BlockSpec((1, t, LANES), lambda i, p, qi: (i, qi, p))
    kspec = pl.BlockSpec((1, s, LANES), lambda i, p, qi: (i, 0, p))
    return pl.pallas_call(
        functools.partial(_fox_attn_kernel, t=t, nt=nt),
        grid=(b, hp, nt),
        in_specs=[qspec, kspec, kspec,
                  pl.BlockSpec((1, 1, 2 * nt, 1, t), lambda i, p, qi: (i, p, 0, 0, 0)), qspec],
        out_specs=qspec,
        out_shape=jax.ShapeDtypeStruct((b, s, 1024), BF16),
        compiler_params=_params("parallel", "parallel", "arbitrary"),
        name="fox_attention",
    )(q.reshape(b, s, -1), kb.reshape(b, s, -1), vb.reshape(b, s, -1), negc, og.reshape(b, s, -1))


def _page_spec(shape, slot_block, n_pages):
    nd = len(shape)
    return pl.BlockSpec((1,) + shape, lambda i, p, pt: (pt[i * n_pages + p], slot_block) + (0,) * (nd - 1))


def _nsa_cmp_sample_kernel(*refs, n_pages):
    pt_ref = refs[0]
    pages = refs[1:1 + n_pages]
    pe_ref, w1_ref, w2_ref, o_ref, x_scr = refs[1 + n_pages:]
    del pt_ref
    rows = n_pages * N_GROUPS
    for j in range(n_pages):
        for slot in range(2):
            for g in range(N_GROUPS):
                r0 = (g * n_pages + j) * HEAD_DIM
                x_scr[slot, r0:r0 + HEAD_DIM, :] = pages[j][0, slot, g]
    for slot in range(2):
        acc = jnp.zeros((rows, LANES), F32)
        for dd in range(HEAD_DIM):
            xl = x_scr[slot, pl.ds(dd, rows, stride=HEAD_DIM), :] + pe_ref[slot, dd:dd + 1, :]
            acc = acc + _dot(xl.astype(BF16), w1_ref[slot, dd])
        hid = acc * jax.nn.sigmoid(acc)
        o_ref[0, slot] = _dot(hid.astype(BF16), w2_ref[slot])


def _nsa_cmp_sample(pool_t, pt, pe, w1, w2, db, n_pages):
    per = PAGE // BLOCK
    pe_t = jnp.tile(pe.transpose(0, 2, 1), (1, 1, per))
    w1_d = _block_diag(w1.transpose(0, 2, 1, 3), per).astype(BF16)
    w2_d = _block_diag(w2, per).astype(BF16)
    rows = n_pages * N_GROUPS
    page = lambda j: pl.BlockSpec((1, 2, N_GROUPS, HEAD_DIM, PAGE),
                                  lambda i, pt, j=j: (pt[i * n_pages + j], 0, 0, 0, 0))
    full = lambda a: pl.BlockSpec(a.shape, lambda i, pt: (0,) * a.ndim)
    return pl.pallas_call(
        functools.partial(_nsa_cmp_sample_kernel, n_pages=n_pages),
        grid_spec=pltpu.PrefetchScalarGridSpec(
            num_scalar_prefetch=1, grid=(db,),
            in_specs=[page(j) for j in range(n_pages)] + [full(pe_t), full(w1_d), full(w2_d)],
            out_specs=pl.BlockSpec((1, 2, rows, per * HEAD_DIM), lambda i, pt: (i, 0, 0, 0)),
            scratch_shapes=[pltpu.VMEM((2, rows * HEAD_DIM, PAGE), F32)]),
        out_shape=jax.ShapeDtypeStruct((db, 2, rows, per * HEAD_DIM), F32),
        compiler_params=_params("arbitrary"),
        name="nsa_compress_sample",
    )(pt, *([pool_t] * n_pages), pe_t, w1_d, w2_d)


def _nsa_sel_sample_kernel(qc_ref, qr_ref, kc_ref, vc_ref, win_ref, kn_ref, vn_ref,
                           oc_ref, ow_ref, sel_ref, *, past, dt, nbs):
    nr = qc_ref.shape[1]
    per_r = dt * N_GROUPS
    row = lax.broadcasted_iota(jnp.int32, (nr, 1), 0)
    t_pos = past + (row // N_GROUPS) % dt

    kc = kc_ref[0]
    nb = kc.shape[0]
    n = lax.broadcasted_iota(jnp.int32, (nr, nb), 1)
    cmask = n * BLOCK + (BLOCK - 1) <= t_pos
    sc = jnp.where(cmask, _dot_nt(qc_ref[0], kc), NEG)
    e = jnp.where(cmask, jnp.exp(sc - jnp.max(sc, axis=-1, keepdims=True)), 0.0)
    l = jnp.sum(e, axis=-1, keepdims=True)
    pc = e / jnp.where(l > 0.0, l, 1.0)
    oc_ref[0] = _dot(pc.astype(BF16), vc_ref[0])

    imp = pc[0:per_r]
    for r in range(1, GROUP):
        imp = imp + pc[r * per_r:(r + 1) * per_r]
    n1 = lax.broadcasted_iota(jnp.int32, (per_r, nb), 1)
    t1 = past + lax.broadcasted_iota(jnp.int32, (per_r, 1), 0) // N_GROUPS
    cur = t1 // BLOCK
    forced = (n1 == 0) | (n1 == cur) | (n1 == cur - 1)
    dead = (n1 * BLOCK > t1) | (n1 >= nbs)
    sel = _top_blocks(jnp.where(forced, jnp.inf, jnp.where(dead, -jnp.inf, imp)), min(N_SELECT, nbs))
    sel_ref[0] = jnp.concatenate([sel] * GROUP, axis=0).astype(BF16)

    qr = qr_ref[0]
    wb = win_ref.shape[-1]
    kw = win_ref[0, 0].reshape(N_GROUPS * HEAD_DIM, wb).astype(BF16)
    vw = win_ref[0, 1].reshape(N_GROUPS * HEAD_DIM, wb).astype(BF16)
    d_old = t_pos - (past - wb + lax.broadcasted_iota(jnp.int32, (nr, wb), 1))
    s_old = jnp.where((d_old >= 0) & (d_old < WINDOW), _dot(qr, kw), NEG)
    tn = lax.broadcasted_iota(jnp.int32, (nr, LANES), 1)
    d_new = t_pos - (past + tn)
    s_new = jnp.where((d_new >= 0) & (tn < dt), _dot(qr, kn_ref[0]), NEG)
    m = jnp.maximum(jnp.max(s_old, axis=-1, keepdims=True), jnp.max(s_new, axis=-1, keepdims=True))
    p_old = jnp.exp(s_old - m)
    p_new = jnp.exp(s_new - m)
    lw = jnp.sum(p_old, axis=-1, keepdims=True) + jnp.sum(p_new, axis=-1, keepdims=True)
    ow_ref[0] = (_dot_nt(p_old.astype(BF16), vw) + _dot_nt(p_new.astype(BF16), vn_ref[0])) / lw


def _nsa_sel_sample(qbd_c, qbd_r, kc, vc, win_t, kn_t, vn_t, past, dt, nbs):
    db, nr, gd = qbd_c.shape
    blk = lambda a: pl.BlockSpec((1,) + a.shape[1:], lambda i: (i,) + (0,) * (a.ndim - 1))
    args = (qbd_c, qbd_r, kc, vc, win_t, kn_t, vn_t)
    return pl.pallas_call(
        functools.partial(_nsa_sel_sample_kernel, past=past, dt=dt, nbs=nbs),
        grid=(db,),
        in_specs=[blk(a) for a in args],
        out_specs=[pl.BlockSpec((1, nr, gd), lambda i: (i, 0, 0)), pl.BlockSpec((1, nr, gd), lambda i: (i, 0, 0)),
                   pl.BlockSpec((1, nr, LANES), lambda i: (i, 0, 0))],
        out_shape=[jax.ShapeDtypeStruct((db, nr, gd), F32), jax.ShapeDtypeStruct((db, nr, gd), F32),
                   jax.ShapeDtypeStruct((db, nr, LANES), BF16)],
        compiler_params=_params("parallel"),
        name="nsa_select_window_sample",
    )(*args)


def _online_update(s, v_t, m_scr, l_scr, acc_scr):
    m_i = m_scr[...]
    m_n = jnp.maximum(m_i, jnp.max(s, axis=-1, keepdims=True))
    alpha = jnp.exp(m_i - m_n)
    p = jnp.exp(s - m_n)
    l_scr[...] = alpha * l_scr[...] + jnp.sum(p, axis=-1, keepdims=True)
    acc_scr[...] = alpha * acc_scr[...] + _dot_nt(p.astype(BF16), v_t)
    m_scr[...] = m_n


def _nsa_slc_sample_kernel(pt_ref, page_ref, q_ref, sel_ref, e_ref, en_ref, kn_ref, vn_ref, oc_ref, ow_ref, gt_ref,
                           o_ref, m_scr, l_scr, acc_scr, *, past, dt):
    del pt_ref
    p = pl.program_id(1)
    gd = N_GROUPS * HEAD_DIM

    @pl.when(p == 0)
    def _():
        m_scr[...] = jnp.full(m_scr.shape, NEG, F32)
        l_scr[...] = jnp.zeros_like(l_scr)
        acc_scr[...] = jnp.zeros_like(acc_scr)

    q = q_ref[0]
    sel = sel_ref[0]
    k_t = page_ref[0, 0].reshape(gd, PAGE).astype(BF16)
    v_t = page_ref[0, 1].reshape(gd, PAGE).astype(BF16)
    s = jnp.where(_dot(sel, e_ref[0]) > 0.5, _dot(q, k_t), NEG)
    _online_update(s, v_t, m_scr, l_scr, acc_scr)

    @pl.when(p == pl.num_programs(1) - 1)
    def _():
        nr = q.shape[0]
        row = lax.broadcasted_iota(jnp.int32, (nr, 1), 0)
        t_row = (row // N_GROUPS) % dt
        tn = lax.broadcasted_iota(jnp.int32, (nr, LANES), 1)
        ok = (_dot(sel, en_ref[...]) > 0.5) & (tn <= t_row) & (tn < dt)
        _online_update(jnp.where(ok, _dot(q, kn_ref[0]), NEG), vn_ref[0], m_scr, l_scr, acc_scr)
        gt = gt_ref[0]
        o_ref[0] = (gt[:, 0:1] * oc_ref[0] + gt[:, 1:2] * (acc_scr[...] / l_scr[...]) + gt[:, 2:3] * ow_ref[0])


def _nsa_slc_sample(pool_t, pt, qbd_r, sel, kn_t, vn_t, o_c, o_w, gates, past, dt, n_pages):
    db, nr, gd = qbd_r.shape
    per = PAGE // BLOCK
    key_blk = jnp.arange(n_pages * PAGE) // BLOCK
    e_tab = (jnp.arange(LANES)[None, :, None] == key_blk.reshape(n_pages, 1, PAGE)).astype(BF16)
    new_blk = jnp.where(jnp.arange(LANES) < dt, (past + jnp.arange(LANES)) // BLOCK, -1)
    e_new = (jnp.arange(LANES)[:, None] == new_blk[None, :]).astype(BF16)
    del per
    blk = lambda a: pl.BlockSpec((1,) + a.shape[1:], lambda i, p, pt: (i,) + (0,) * (a.ndim - 1))
    return pl.pallas_call(
        functools.partial(_nsa_slc_sample_kernel, past=past, dt=dt),
        grid_spec=pltpu.PrefetchScalarGridSpec(
            num_scalar_prefetch=1, grid=(db, n_pages),
            in_specs=[_page_spec((2, N_GROUPS, HEAD_DIM, PAGE), 1, n_pages), blk(qbd_r), blk(sel),
                      pl.BlockSpec((1, LANES, PAGE), lambda i, p, pt: (p, 0, 0)),
                      pl.BlockSpec(e_new.shape, lambda i, p, pt: (0, 0)),
                      blk(kn_t), blk(vn_t), blk(o_c), blk(o_w), blk(gates)],
            out_specs=pl.BlockSpec((1, nr, gd), lambda i, p, pt: (i, 0, 0)),
            scratch_shapes=[pltpu.VMEM((nr, 1), F32), pltpu.VMEM((nr, 1), F32), pltpu.VMEM((nr, gd), F32)]),
        out_shape=jax.ShapeDtypeStruct((db, nr, gd), F32),
        compiler_params=_params("parallel", "arbitrary"),
        name="nsa_selected_sample",
    )(pt, pool_t, qbd_r, sel, e_tab, e_new, kn_t, vn_t, o_c, o_w, gates)


def _fox_sample_kernel(pt_ref, page_ref, lf_ref, q_ref, kn_ref, vn_ref, lfn_ref, og_ref,
                       o_ref, m_scr, l_scr, acc_scr, c_scr, *, dt):
    del pt_ref
    p = pl.program_id(1)
    hd = N_HEADS * HEAD_DIM

    @pl.when(p == 0)
    def _():
        m_scr[...] = jnp.full(m_scr.shape, NEG, F32)
        l_scr[...] = jnp.zeros_like(l_scr)
        acc_scr[...] = jnp.zeros_like(acc_scr)
        c_scr[...] = jnp.zeros_like(c_scr)

    tri = (lax.broadcasted_iota(jnp.int32, (PAGE, PAGE), 0) <= lax.broadcasted_iota(jnp.int32, (PAGE, PAGE), 1)).astype(BF16)

    def cum(lf):
        hi, mid, lo = _split3(lf)
        return _dot(hi, tri) + _dot(mid, tri) + _dot(lo, tri) + c_scr[...]

    q = q_ref[0]
    c_page = cum(lf_ref[0])
    k_t = page_ref[0, 0].reshape(hd, PAGE).astype(BF16)
    v_t = page_ref[0, 1].reshape(hd, PAGE).astype(BF16)
    s = _dot(q, k_t) - jnp.concatenate([c_page] * dt, axis=0)
    _online_update(s, v_t, m_scr, l_scr, acc_scr)
    c_scr[...] = jnp.broadcast_to(c_page[:, PAGE - 1:PAGE], c_scr.shape)

    @pl.when(p == pl.num_programs(1) - 1)
    def _():
        nr = q.shape[0]
        t_row = lax.broadcasted_iota(jnp.int32, (nr, 1), 0) // N_HEADS
        tn = lax.broadcasted_iota(jnp.int32, (nr, LANES), 1)
        c_new = cum(lfn_ref[0])
        s_n = _dot(q, kn_ref[0]) - jnp.concatenate([c_new] * dt, axis=0)
        s_n = jnp.where((tn <= t_row) & (tn < dt), s_n, NEG)
        _online_update(s_n, vn_ref[0], m_scr, l_scr, acc_scr)
        o_ref[0] = acc_scr[...] / l_scr[...] * og_ref[0]


def _fox_sample(pool_t, lf_pool_t, pt, qbd, kn_t, vn_t, lfn_t, og_t, dt, n_pages):
    db, nr, hd = qbd.shape
    blk = lambda a: pl.BlockSpec((1,) + a.shape[1:], lambda i, p, pt: (i,) + (0,) * (a.ndim - 1))
    return pl.pallas_call(
        functools.partial(_fox_sample_kernel, dt=dt),
        grid_spec=pltpu.PrefetchScalarGridSpec(
            num_scalar_prefetch=1, grid=(db, n_pages),
            in_specs=[_page_spec((2, N_HEADS, HEAD_DIM, PAGE), 0, n_pages),
                      pl.BlockSpec((1, N_HEADS, PAGE), lambda i, p, pt: (pt[i * n_pages + p], 0, 0)),
                      blk(qbd), blk(kn_t), blk(vn_t), blk(lfn_t), blk(og_t)],
            out_specs=pl.BlockSpec((1, nr, hd), lambda i, p, pt: (i, 0, 0)),
            scratch_shapes=[pltpu.VMEM((nr, 1), F32), pltpu.VMEM((nr, 1), F32), pltpu.VMEM((nr, hd), F32),
                            pltpu.VMEM((N_HEADS, PAGE), F32)]),
        out_shape=jax.ShapeDtypeStruct((db, nr, hd), F32),
        compiler_params=_params("parallel", "arbitrary"),
        name="fox_attention_sample",
    )(pt, pool_t, lf_pool_t, qbd, kn_t, vn_t, lfn_t, og_t)


def _per_batch(a, dt, db):
    return a.reshape(dt, db, -1).transpose(1, 0, 2)


def _new_keys_t(a, dt, db):
    a = _per_batch(a, dt, db).transpose(0, 2, 1)
    return jnp.pad(a, ((0, 0), (0, 0), (0, LANES - dt))).astype(a.dtype)


def _nsa_qbd(q, dt, db):
    q5 = q.reshape(dt, db, N_GROUPS, GROUP, HEAD_DIM).transpose(1, 3, 0, 2, 4)
    eye = jnp.eye(N_GROUPS, dtype=q.dtype)
    out = q5[:, :, :, :, None, :] * eye[None, None, None, :, :, None]
    return out.reshape(db, GROUP * dt * N_GROUPS, N_GROUPS * HEAD_DIM)


def _nsa_undiag(o, dt, db):
    o6 = o.reshape(db, GROUP, dt, N_GROUPS, N_GROUPS, HEAD_DIM)
    dg = jnp.diagonal(o6, axis1=3, axis2=4)
    return dg.transpose(2, 0, 4, 1, 3).reshape(dt * db, N_HEADS * HEAD_DIM)


def _fox_qbd(q, dt, db):
    q4 = q.reshape(dt, db, N_HEADS, HEAD_DIM).transpose(1, 0, 2, 3)
    eye = jnp.eye(N_HEADS, dtype=q.dtype)
    out = q4[:, :, :, None, :] * eye[None, None, :, :, None]
    return out.reshape(db, dt * N_HEADS, N_HEADS * HEAD_DIM)


def _fox_undiag(o, dt, db):
    o5 = o.reshape(db, dt, N_HEADS, N_HEADS, HEAD_DIM)
    dg = jnp.diagonal(o5, axis1=2, axis2=3)
    return dg.transpose(1, 0, 3, 2).reshape(dt * db, N_HEADS * HEAD_DIM)


def _prompt_mods(mod, b):
    return [m.reshape(b, 1, -1) for m in jnp.split(mod[:b], 6, axis=-1)]


def _sample_mods(mod, b, t):
    return [jnp.tile(m, (t, 1))[None] for m in jnp.split(mod[b:], 6, axis=-1)]


def kernel(x_prompt, x_sample, cache_nsa_kv, cache_nsa_win, cache_fox_kv, cache_fox_logf, state_ffn_conv, page_table, c_prompt, c_sample, w_ada, b_ada, norm_mix_g, norm_ffn_g, w_nsa_in, pe_cmp, w_cmp1, w_cmp2, w_nsa_out, w_fox_in, b_fox_f, fox_q_norm_g, fox_k_norm_g, w_fox_out, w_ffn_up, ffn_conv_w, ffn_conv_b, w_ffn_down, final_norm_g):
    b, s, d = x_prompt.shape
    db, dt, _ = x_sample.shape
    f_dim = w_ffn_down.shape[1]
    depth = w_ada.shape[0]
    tm = 256
    tpb = s // tm
    perm = _ffn_perm(f_dim)

    n_pages = page_table.shape[1]
    past = n_pages * PAGE
    nbs = -(-(past + dt) // BLOCK)
    per = PAGE // BLOCK
    r_s = dt * db
    pt = page_table.reshape(-1).astype(jnp.int32)
    key_last = (0, 2, 3, 4, 1)

    c_all = jnp.concatenate([c_prompt, c_sample], axis=0)
    xp = x_prompt.reshape(b * s, d)
    xs = x_sample.transpose(1, 0, 2).reshape(r_s, d)
    tabs_p = _rope_tables(jnp.arange(s, dtype=jnp.int32))
    tabs_s = _rope_tables(past + jnp.arange(r_s, dtype=jnp.int32) // db)

    nsa_kv_p, nsa_win_p, fox_kv_p, fox_lf_p, conv_p = [], [], [], [], []
    nsa_kv_s, nsa_win_s, fox_kv_s, fox_lf_s, conv_s = [], [], [], [], []
    y_prompt = y_sample = None
    for i in range(depth):
        j = i // 2
        mod = _adaln(c_all, w_ada[i].astype(BF16), b_ada[i][None])
        mp = _prompt_mods(mod, b)
        ms = _sample_mods(mod, b, dt)
        g_mix = norm_mix_g[i][None]
        if i % 2 == 0:
            w_in = _nsa_weight(w_nsa_in[j])
            qc, qr, rows, win, dup, gates = _nsa_proj(xp, g_mix, mp[1], mp[0], w_in, tabs_p, "prompt", tm, tpb)
            cdup = _compress_prompt(rows, *_cmp_weights(pe_cmp[j], w_cmp1[j], w_cmp2[j]), b, s)
            attn_p = _nsa_attention(qc, qr, cdup, dup, gates, b, s).reshape(b * s, d)
            w_out = w_nsa_out[j].astype(BF16)
            nsa_kv_p.append(rows.reshape(b, s, 4, N_GROUPS, HEAD_DIM))
            nsa_win_p.append(win.reshape(b, s, 2, N_GROUPS, HEAD_DIM)[:, s - min(WINDOW, s):])

            qc, qr, rows, win, _, gates = _nsa_proj(xs, g_mix, ms[1], ms[0], w_in, tabs_s, "sample", r_s, 1)
            nsa_kv_s.append(rows.reshape(dt, db, 4, N_GROUPS, HEAD_DIM).transpose(1, 0, 2, 3, 4))
            nsa_win_s.append(win.reshape(dt, db, 2, N_GROUPS, HEAD_DIM).transpose(1, 0, 2, 3, 4))
            pool_t = jnp.transpose(cache_nsa_kv[j], key_last)
            kc2 = _nsa_cmp_sample(pool_t, pt, pe_cmp[j], w_cmp1[j], w_cmp2[j], db, n_pages)
            kc = kc2.reshape(db, 2, N_GROUPS, n_pages, per, HEAD_DIM).transpose(0, 1, 3, 4, 2, 5)
            kc = kc.reshape(db, 2, n_pages * per, N_GROUPS * HEAD_DIM)
            kc = jnp.pad(kc, ((0, 0), (0, 0), (0, LANES - n_pages * per), (0, 0))).astype(BF16)
            qbd_c, qbd_r = _nsa_qbd(qc, dt, db), _nsa_qbd(qr, dt, db)
            gd = N_GROUPS * HEAD_DIM
            newt = lambda a: _new_keys_t(a.astype(BF16), dt, db)
            o_c, o_w, sel = _nsa_sel_sample(qbd_c, qbd_r, kc[:, 0], kc[:, 1], jnp.transpose(cache_nsa_win[j], key_last),
                                            newt(win[:, :gd]), newt(win[:, gd:]), past, dt, nbs)
            g_s = gates.reshape(dt, db, N_GROUPS, LANES)[..., :GROUP * 3].reshape(dt, db, N_GROUPS, GROUP, 3)
            g_s = g_s.transpose(1, 3, 0, 2, 4).reshape(db, GROUP * dt * N_GROUPS, 3)
            g_s = jnp.pad(g_s, ((0, 0), (0, 0), (0, LANES - 3)))
            o_s = _nsa_slc_sample(pool_t, pt, qbd_r, sel, newt(rows[:, 2 * gd:3 * gd]), newt(rows[:, 3 * gd:]),
                                  o_c, o_w, g_s, past, dt, n_pages)
            attn_s = _nsa_undiag(o_s, dt, db).astype(BF16)
        else:
            w_in = _fox_weight(w_fox_in[j])
            fox = lambda x, sc, sh, mode, t, n: _fox_proj(x, g_mix, sc, sh, w_in, fox_q_norm_g[j], fox_k_norm_g[j],
                                                         b_fox_f[j], mode, t, n)
            q, kf, kb, vf, vb, og, lf, c = fox(xp, mp[1], mp[0], "prompt", tm, tpb)
            attn_p = _fox_attention(q, kb, vb, c, og, b, s).reshape(b * s, d)
            w_out = w_fox_out[j].astype(BF16)
            hs = (N_HEADS, HEAD_DIM)
            fox_kv_p.append(jnp.stack([kf.reshape((b, s) + hs), vf.reshape((b, s) + hs)], axis=2))
            fox_lf_p.append(lf[:, :N_HEADS].reshape(b, s, N_HEADS))

            q, kf, kb, vf, vb, og, lf, _ = fox(xs, ms[1], ms[0], "sample", r_s, 1)
            fox_kv_s.append(jnp.stack([kf.reshape((dt, db) + hs), vf.reshape((dt, db) + hs)], axis=2).transpose(1, 0, 2, 3, 4))
            fox_lf_s.append(lf[:, :N_HEADS].reshape(dt, db, N_HEADS).transpose(1, 0, 2))
            og_t = jnp.repeat(_per_batch(og, dt, db)[:, :, None, :], N_HEADS, axis=2).reshape(db, dt * N_HEADS, -1)
            o_full = _fox_sample(jnp.transpose(cache_fox_kv[j], key_last), jnp.transpose(cache_fox_logf[j], (0, 2, 1)),
                                 pt, _fox_qbd(q, dt, db), _new_keys_t(kb, dt, db), _new_keys_t(vb, dt, db),
                                 _new_keys_t(lf[:, :N_HEADS], dt, db), og_t, dt, n_pages)
            attn_s = _fox_undiag(o_full, dt, db).astype(BF16)

        final = final_norm_g[None] if i == depth - 1 else None
        ffn_w = (norm_ffn_g[i][None], w_ffn_up[i][:, perm].astype(BF16), ffn_conv_w[i][:, perm],
                 ffn_conv_b[i][perm][None], w_ffn_down[i].astype(BF16))
        unperm = lambda a: jnp.zeros_like(a).at[..., perm].set(a)

        res = _ffn(xp, attn_p, w_out, (mp[2], mp[4], mp[3], mp[5]), *ffn_w, "prompt", tm, tpb, 1, final_g=final)
        xp = res[0]
        conv_p.append(unperm(res[1].reshape(b, tpb, 8, 2 * f_dim)[:, -1, 6:, :]))
        if final is not None:
            y_prompt = res[2].reshape(b, s, d)

        state = state_ffn_conv[i].transpose(1, 0, 2).reshape(2 * db, 2 * f_dim)[:, perm]
        res = _ffn(xs, attn_s, w_out, (ms[2], ms[4], ms[3], ms[5]), *ffn_w, "sample", r_s, 1, db, state=state,
                   final_g=final)
        xs = res[0]
        conv_s.append(unperm(res[1]).reshape(2, db, 2 * f_dim).transpose(1, 0, 2))
        if final is not None:
            y_sample = res[2].reshape(dt, db, d).transpose(1, 0, 2)

    return (y_prompt, y_sample, jnp.stack(nsa_kv_p), jnp.stack(nsa_kv_s), jnp.stack(nsa_win_p), jnp.stack(nsa_win_s),
            jnp.stack(fox_kv_p), jnp.stack(fox_kv_s), jnp.stack(fox_lf_p), jnp.stack(fox_lf_s),
            jnp.stack(conv_p), jnp.stack(conv_s))
```

```python
import functools

import jax
import jax.numpy as jnp
from jax import lax
from jax.experimental import pallas as pl
from jax.experimental.pallas import tpu as pltpu

F32 = jnp.float32
BF16 = jnp.bfloat16

HEAD_DIM = 64
N_HEADS = 16
N_GROUPS = 4
GROUP = N_HEADS // N_GROUPS
BLOCK = 64
N_SELECT = 16
WINDOW = 512
ROT_DIM = 16
ROPE_THETA = 500000.0
PAGE = 128
Q_BLOCK = 128
EPS = 1e-6
NEG = -1e30
SCALE = HEAD_DIM ** -0.5

LANES = 128
VMEM_LIMIT = 56 * 1024 * 1024


def _params(*sem):
    return pltpu.CompilerParams(dimension_semantics=sem, vmem_limit_bytes=VMEM_LIMIT)


def _dot(a, b):
    return jnp.dot(a, b, preferred_element_type=F32)


def _dot_nt(a, b):
    return lax.dot_general(a, b, (((1,), (1,)), ((), ())), preferred_element_type=F32)


def _split3(x):
    hi = x.astype(BF16)
    r1 = x - hi.astype(F32)
    mid = r1.astype(BF16)
    lo = (r1 - mid.astype(F32)).astype(BF16)
    return hi, mid, lo


def _ada_kernel(c_ref, w_ref, b_ref, o_ref):
    c = c_ref[...]
    a = (c * jax.nn.sigmoid(c)).astype(BF16)
    o_ref[...] = _dot(a, w_ref[...]) + b_ref[...]


def _adaln(c, w, b):
    r, d = c.shape
    n = w.shape[1]
    tn = n // 4
    return pl.pallas_call(
        _ada_kernel,
        grid=(n // tn,),
        in_specs=[pl.BlockSpec((r, d), lambda j: (0, 0)),
                  pl.BlockSpec((d, tn), lambda j: (0, j)),
                  pl.BlockSpec((1, tn), lambda j: (0, j))],
        out_specs=pl.BlockSpec((r, tn), lambda j: (0, j)),
        out_shape=jax.ShapeDtypeStruct((r, n), F32),
        compiler_params=_params("arbitrary"),
        name="adaln",
    )(c, w, b)


def _norm_mod(x, g, scale, shift):
    ms = jnp.mean(x * x, axis=-1, keepdims=True)
    return (x * lax.rsqrt(ms + EPS) * g) * (1.0 + scale) + shift


def _rope_tables(pos):
    freqs = ROPE_THETA ** (-jnp.arange(0, ROT_DIM, 2, dtype=F32) / ROT_DIM)
    ang = pos.astype(F32)[:, None] * freqs[None, :]
    cos, sin = jnp.cos(ang), jnp.sin(ang)
    half = ROT_DIM // 2
    one = jnp.ones((pos.shape[0], HEAD_DIM - ROT_DIM), F32)
    zero = jnp.zeros_like(one)
    zh = jnp.zeros_like(cos)
    c = jnp.concatenate([cos, cos, one], axis=1)
    s_lo = jnp.concatenate([zh, sin, zero], axis=1)
    s_hi = jnp.concatenate([-sin, zh, zero], axis=1)
    rep = LANES // HEAD_DIM
    return jnp.tile(c, (1, rep)), jnp.tile(s_lo, (1, rep)), jnp.tile(s_hi, (1, rep))


def _rope(v, c, s_lo, s_hi):
    half = ROT_DIM // 2
    return v * c + pltpu.roll(v, half, 1) * s_lo + pltpu.roll(v, LANES - half, 1) * s_hi


def _row_specs(mode, tm, tpb, d):
    if mode == "prompt":
        mod = pl.BlockSpec((1, 1, d), lambda i: (i // tpb, 0, 0))
        tab = pl.BlockSpec((tm, LANES), lambda i: (i % tpb, 0))
    else:
        mod = pl.BlockSpec((1, tm, d), lambda i: (0, 0, 0))
        tab = pl.BlockSpec((tm, LANES), lambda i: (0, 0))
    return mod, tab


NSA_DUP = 4 * N_GROUPS * LANES
NSA_W_COLS = 1024 + 6 * 256 + N_GROUPS * LANES


def _nsa_proj_kernel(x_ref, g_ref, sc_ref, sh_ref, w_ref, tc_ref, tl_ref, th_ref,
                     qc_ref, qr_ref, rows_ref, win_ref, dup_ref, gates_ref):
    h = _norm_mod(x_ref[...], g_ref[...], sc_ref[0], sh_ref[0]).astype(BF16)
    z = _dot(h, w_ref[...])
    tc, tl, th = tc_ref[...], tl_ref[...], th_ref[...]
    lo = lax.broadcasted_iota(jnp.int32, (z.shape[0], LANES), 1) < HEAD_DIM

    def chunk(j):
        return z[:, j * LANES:(j + 1) * LANES]

    def put_dup(kind, pair, v):
        vr = pltpu.roll(v, HEAD_DIM, 1)
        base = (kind * N_GROUPS + 2 * pair) * LANES
        dup_ref[:, base:base + LANES] = jnp.where(lo, v, vr).astype(BF16)
        dup_ref[:, base + LANES:base + 2 * LANES] = jnp.where(lo, vr, v).astype(BF16)

    for j in range(8):
        v = chunk(j)
        sl = slice(j * LANES, (j + 1) * LANES)
        qc_ref[:, sl] = (v * SCALE).astype(BF16)
        qr_ref[:, sl] = (_rope(v, tc, tl, th) * SCALE).astype(BF16)
    for j in range(4):
        rows_ref[:, j * LANES:(j + 1) * LANES] = chunk(8 + j)
    for j in range(2):
        ks = _rope(chunk(12 + j), tc, tl, th)
        vs = chunk(14 + j)
        rows_ref[:, (4 + j) * LANES:(5 + j) * LANES] = ks
        rows_ref[:, (6 + j) * LANES:(7 + j) * LANES] = vs
        put_dup(0, j, ks)
        put_dup(1, j, vs)
    for j in range(2):
        kw = _rope(chunk(16 + j), tc, tl, th)
        vw = chunk(18 + j)
        win_ref[:, j * LANES:(j + 1) * LANES] = kw
        win_ref[:, (2 + j) * LANES:(3 + j) * LANES] = vw
        put_dup(2, j, kw)
        put_dup(3, j, vw)
    for j in range(N_GROUPS):
        gates_ref[:, j * LANES:(j + 1) * LANES] = jax.nn.sigmoid(chunk(20 + j))


def _nsa_proj(x, g, scale, shift, w, tabs, mode, tm, tpb):
    r, d = x.shape
    mod, tab = _row_specs(mode, tm, tpb, d)
    row = lambda n: pl.BlockSpec((tm, n), lambda i: (i, 0))
    outs = [(1024, BF16), (1024, BF16), (1024, F32), (512, F32), (NSA_DUP, BF16), (N_GROUPS * LANES, F32)]
    return pl.pallas_call(
        _nsa_proj_kernel,
        grid=(r // tm,),
        in_specs=[row(d), pl.BlockSpec((1, d), lambda i: (0, 0)), mod, mod,
                  pl.BlockSpec(w.shape, lambda i: (0, 0)), tab, tab, tab],
        out_specs=[row(n) for n, _ in outs],
        out_shape=[jax.ShapeDtypeStruct((r, n), t) for n, t in outs],
        compiler_params=_params("parallel"),
        name="nsa_proj",
    )(x, g, scale, shift, w, *tabs)


def _nsa_weight(w_in):
    d = w_in.shape[0]
    main = w_in[:, :1024 + 6 * 256]
    gates = w_in[:, 1024 + 6 * 256:].reshape(d, N_GROUPS, GROUP * 3)
    gates = jnp.pad(gates, ((0, 0), (0, 0), (0, LANES - GROUP * 3))).reshape(d, N_GROUPS * LANES)
    return jnp.concatenate([main, gates], axis=1).astype(BF16)


def _cmp_kernel(x_ref, pe_ref, w1_ref, w2_ref, o_ref, acc_ref):
    lc = pl.program_id(2)

    @pl.when(lc == 0)
    def _():
        acc_ref[...] = jnp.zeros_like(acc_ref)

    acc = acc_ref[...]
    for l in range(x_ref.shape[1]):
        xl = (x_ref[0, l] + pe_ref[0, l:l + 1, :]).astype(BF16)
        acc = acc + _dot(xl, w1_ref[0, l])
    acc_ref[...] = acc

    @pl.when(lc == pl.num_programs(2) - 1)
    def _():
        hid = acc * jax.nn.sigmoid(acc)
        o_ref[0, 0] = _dot(hid.astype(BF16), w2_ref[0]).astype(BF16)


def _block_diag(w, n):
    eye = jnp.eye(n, dtype=w.dtype)
    out = jnp.einsum("ij,...ab->...iajb", eye, w)
    return out.reshape(w.shape[:-2] + (n * w.shape[-2], n * w.shape[-1]))


def _cmp_weights(pe, w1, w2):
    pe_t = jnp.tile(pe, (1, 1, N_GROUPS))
    w1_bd = _block_diag(w1, N_GROUPS).astype(BF16)
    w2_dup = jnp.concatenate([w2, w2], axis=-1)
    w2_bd = _block_diag(w2_dup, N_GROUPS).astype(BF16)
    return pe_t, w1_bd, w2_bd


def _compress_prompt(rows, pe_t, w1_bd, w2_bd, b, s):
    nb = s // BLOCK
    gd = N_GROUPS * HEAD_DIM
    xt = rows.reshape(b, nb, BLOCK, -1)[..., :2 * gd].transpose(0, 2, 1, 3)
    lstep = 8
    return pl.pallas_call(
        _cmp_kernel,
        grid=(b, 2, BLOCK // lstep),
        in_specs=[pl.BlockSpec((1, lstep, nb, gd), lambda i, kv, lc: (i, lc, 0, kv)),
                  pl.BlockSpec((1, lstep, gd), lambda i, kv, lc: (kv, lc, 0)),
                  pl.BlockSpec((1, lstep, gd, gd), lambda i, kv, lc: (kv, lc, 0, 0)),
                  pl.BlockSpec((1, gd, N_GROUPS * LANES), lambda i, kv, lc: (kv, 0, 0))],
        out_specs=pl.BlockSpec((1, 1, nb, N_GROUPS * LANES), lambda i, kv, lc: (i, kv, 0, 0)),
        out_shape=jax.ShapeDtypeStruct((b, 2, nb, N_GROUPS * LANES), BF16),
        scratch_shapes=[pltpu.VMEM((nb, gd), F32)],
        compiler_params=_params("parallel", "parallel", "arbitrary"),
        name="nsa_compress",
    )(xt, pe_t, w1_bd, w2_bd)


def _softmax_rows(s):
    m = jnp.max(s, axis=-1, keepdims=True)
    p = jnp.exp(s - m)
    return p, jnp.sum(p, axis=-1, keepdims=True)


def _top_blocks(v, n_sel):
    rows, nb = v.shape
    pad = -rows % LANES
    if pad:
        v = jnp.concatenate([v, jnp.zeros((pad, nb), v.dtype)], axis=0)
    v = v.T
    n = lax.broadcasted_iota(jnp.int32, v.shape, 0)
    sel = jnp.zeros(v.shape, F32)
    for _ in range(n_sel):
        mx = jnp.max(v, axis=0, keepdims=True)
        idx = jnp.min(jnp.where(v == mx, n, nb), axis=0, keepdims=True)
        hit = n == idx
        sel = jnp.where(hit, 1.0, sel)
        v = jnp.where(hit, -jnp.inf, v)
    return sel.T[:rows]


def _stack_heads(ref):
    q = ref.shape[1]
    lo = lax.broadcasted_iota(jnp.int32, (q, LANES), 1) < HEAD_DIM
    parts = []
    for r in range(GROUP):
        pair = ref[0, :, (r // 2) * LANES:(r // 2 + 1) * LANES]
        parts.append(jnp.where(lo if r % 2 == 0 else jnp.logical_not(lo), pair, jnp.zeros_like(pair)))
    return jnp.concatenate(parts, axis=0)


def _nsa_attn_kernel(qc_ref, qr_ref, kc_ref, vc_ref, ks_ref, vs_ref, kw_ref, vw_ref, e_ref, gt_ref, o_ref,
                     *, tk, wlen):
    nq = Q_BLOCK
    s0 = pl.program_id(2) * nq
    t_q = s0 + lax.broadcasted_iota(jnp.int32, (nq, 1), 0)
    rep = lambda a: jnp.concatenate([a] * GROUP, axis=0)

    qc = _stack_heads(qc_ref)
    kc = kc_ref[0, 0]
    nb = kc.shape[0]
    n = lax.broadcasted_iota(jnp.int32, (nq, nb), 1)
    cmask = rep(n * BLOCK + (BLOCK - 1) <= t_q)
    sc = jnp.where(cmask, _dot_nt(qc, kc), NEG)
    e = jnp.where(cmask, jnp.exp(sc - jnp.max(sc, axis=-1, keepdims=True)), 0.0)
    l = jnp.sum(e, axis=-1, keepdims=True)
    pc = e / jnp.where(l > 0.0, l, 1.0)
    o_c = _dot(pc.astype(BF16), vc_ref[0, 0])

    imp = pc[0:nq] + pc[nq:2 * nq] + pc[2 * nq:3 * nq] + pc[3 * nq:4 * nq]
    cur = t_q // BLOCK
    forced = (n == 0) | (n == cur) | (n == cur - 1)
    future = n * BLOCK > t_q
    sel = _top_blocks(jnp.where(forced, jnp.inf, jnp.where(future, -jnp.inf, imp)), min(N_SELECT, nb))
    selb = sel.astype(BF16)

    qr = _stack_heads(qr_ref)
    w0 = pl.multiple_of(jnp.maximum(s0 + nq - wlen, 0), nq)
    kpos = w0 + lax.broadcasted_iota(jnp.int32, (nq, wlen), 1)
    dpos = t_q - kpos
    wmask = rep((dpos >= 0) & (dpos < WINDOW))
    sw = jnp.where(wmask, _dot_nt(qr, kw_ref[0, pl.ds(w0, wlen), :]), NEG)
    pw, lw = _softmax_rows(sw)
    o_w = _dot(pw.astype(BF16), vw_ref[0, pl.ds(w0, wlen), :]) / lw

    col = lax.broadcasted_iota(jnp.int32, (nq, tk), 1)
    half = GROUP * nq // 2
    q_halves = (qr[:half], qr[half:])

    def body(j, carry):
        k0 = pl.multiple_of(j * tk, tk)
        k = ks_ref[0, pl.ds(k0, tk), :]
        v = vs_ref[0, pl.ds(k0, tk), :]
        ok = (_dot(selb, e_ref[j]) > 0.5) & (k0 + col <= t_q)
        ok2 = jnp.concatenate([ok, ok], axis=0)
        out = []
        for q_h, (m_i, l_i, acc) in zip(q_halves, carry):
            s = jnp.where(ok2, _dot_nt(q_h, k), NEG)
            m_n = jnp.maximum(m_i, jnp.max(s, axis=-1, keepdims=True))
            alpha = jnp.exp(m_i - m_n)
            p = jnp.exp(s - m_n)
            l_n = alpha * l_i + jnp.sum(p, axis=-1, keepdims=True)
            out.append((m_n, l_n, alpha * acc + _dot(p.astype(BF16), v)))
        return tuple(out)

    init = (jnp.full((half, 1), NEG, F32), jnp.zeros((half, 1), F32), jnp.zeros((half, LANES), F32))
    (_, l_a, acc_a), (_, l_b, acc_b) = lax.fori_loop(0, (s0 + nq + tk - 1) // tk, body, (init, init))
    o_s = jnp.concatenate([acc_a / l_a, acc_b / l_b], axis=0)

    gt = gt_ref[0]
    lo = lax.broadcasted_iota(jnp.int32, (nq, LANES), 1) < HEAD_DIM
    outs = []
    for r in range(GROUP):
        sl = slice(r * nq, (r + 1) * nq)
        outs.append(gt[:, 3 * r:3 * r + 1] * o_c[sl] + gt[:, 3 * r + 1:3 * r + 2] * o_s[sl]
                    + gt[:, 3 * r + 2:3 * r + 3] * o_w[sl])
    for pr in range(GROUP // 2):
        o_ref[0, :, pr * LANES:(pr + 1) * LANES] = jnp.where(lo, outs[2 * pr], outs[2 * pr + 1]).astype(BF16)


def _nsa_attention(qc, qr, cdup, dup, gates, b, s):
    nb = s // BLOCK
    tk = min(512, s)
    wlen = min(WINDOW + Q_BLOCK, s)
    blk = (jnp.arange(s) // BLOCK).reshape(s // tk, 1, tk)
    expand = (jnp.arange(nb)[None, :, None] == blk).astype(BF16)
    g4 = N_GROUPS
    qspec = pl.BlockSpec((1, Q_BLOCK, 2 * LANES), lambda i, g, q: (i, q, g))
    dspec = lambda kind: pl.BlockSpec((1, s, LANES), lambda i, g, q: (i, 0, kind * g4 + g))
    return pl.pallas_call(
        functools.partial(_nsa_attn_kernel, tk=tk, wlen=wlen),
        grid=(b, N_GROUPS, s // Q_BLOCK),
        in_specs=[qspec, qspec,
                  pl.BlockSpec((1, 1, nb, LANES), lambda i, g, q: (i, 0, 0, g)),
                  pl.BlockSpec((1, 1, nb, LANES), lambda i, g, q: (i, 1, 0, g)),
                  dspec(0), dspec(1), dspec(2), dspec(3),
                  pl.BlockSpec(expand.shape, lambda i, g, q: (0, 0, 0)),
                  pl.BlockSpec((1, Q_BLOCK, LANES), lambda i, g, q: (i, q, g))],
        out_specs=qspec,
        out_shape=jax.ShapeDtypeStruct((b, s, 1024), BF16),
        compiler_params=_params("parallel", "parallel", "arbitrary"),
        name="nsa_attention",
    )(qc.reshape(b, s, -1), qr.reshape(b, s, -1), cdup, cdup,
      dup.reshape(b, s, -1), dup.reshape(b, s, -1), dup.reshape(b, s, -1), dup.reshape(b, s, -1),
      expand, gates.reshape(b, s, -1))


FFN_CHUNK = 256


def _ffn_kernel(*refs, u, tpb, chain, final, nf):
    (x_ref, a_ref, wo_ref, gm_ref, g_ref, sc_ref, sh_ref, gf_ref, wu_ref, cw_ref, cb_ref, wd_ref) = refs[:12]
    k = 12
    st_ref = gfin_ref = y_ref = carry_scr = None
    if not chain:
        st_ref = refs[k]
        k += 1
    if final:
        gfin_ref = refs[k]
        k += 1
    xo_ref, tail_ref = refs[k], refs[k + 1]
    k += 2
    if final:
        y_ref = refs[k]
        k += 1
    ext_scr = refs[k]
    if chain:
        carry_scr = refs[k + 1]

    fc = FFN_CHUNK
    tm = x_ref.shape[0]
    base = ext_scr.shape[0] - tm
    tail = tail_ref.shape[0]
    x1 = x_ref[...] + gm_ref[0] * _dot(a_ref[...], wo_ref[...])
    h = _norm_mod(x1, g_ref[...], sc_ref[0], sh_ref[0]).astype(BF16)
    if chain:
        first = (pl.program_id(0) % tpb) == 0
    acc = jnp.zeros((tm, x_ref.shape[1]), F32)
    for f in range(nf):
        cs = slice(f * 2 * fc, (f + 1) * 2 * fc)
        up = _dot(h, wu_ref[:, cs])
        if chain:
            ext_scr[0:base, :] = jnp.where(first, 0.0, carry_scr[f])
            carry_scr[f] = up[tm - base:, :]
        else:
            ext_scr[0:base, :] = st_ref[:, cs]
        ext_scr[base:, :] = up
        tail_ref[:, cs] = up[tm - tail:, :]
        cw = cw_ref[:, cs]
        mixed = (cb_ref[:, cs] + cw[0:1] * ext_scr[base - 2 * u:base - 2 * u + tm, :]
                 + cw[1:2] * ext_scr[base - u:base - u + tm, :] + cw[2:3] * up)
        a, g = mixed[:, :fc], mixed[:, fc:]
        act = (g * jax.nn.sigmoid(g) * a).astype(BF16)
        acc = acc + _dot(act, wd_ref[f * fc:(f + 1) * fc, :])
    xn = x1 + gf_ref[0] * acc
    xo_ref[...] = xn
    if final:
        ms = jnp.mean(xn * xn, axis=-1, keepdims=True)
        y_ref[...] = xn * lax.rsqrt(ms + EPS) * gfin_ref[...]


def _ffn_cols(a, inverse=False):
    n = a.shape[-1]
    nf = n // (2 * FFN_CHUNK)
    mid = (nf, 2, FFN_CHUNK) if inverse else (2, nf, FFN_CHUNK)
    return a.reshape(a.shape[:-1] + mid).swapaxes(-3, -2).reshape(a.shape)


def _ffn(x, attn, w_out, mods, g, wu, cw, cb, wd, mode, tm, tpb, u, state=None, final_g=None):
    r, d = x.shape
    f2 = wu.shape[1]
    nf = f2 // (2 * FFN_CHUNK)
    chain = state is None
    final = final_g is not None
    base = 8 if chain else 2 * u
    tail = 8 if chain else 2 * u
    mod, _ = _row_specs(mode, tm, tpb, d)
    row = lambda n: pl.BlockSpec((tm, n), lambda i: (i, 0))
    full = lambda a: pl.BlockSpec(a.shape, lambda i: (0,) * a.ndim)
    args = [x, attn, w_out, mods[0], g, mods[1], mods[2], mods[3], wu, cw, cb, wd]
    specs = [row(d), row(d), full(w_out), mod, full(g), mod, mod, mod, full(wu), full(cw), full(cb), full(wd)]
    if not chain:
        args.append(state)
        specs.append(full(state))
    if final:
        args.append(final_g)
        specs.append(full(final_g))
    out_specs = [row(d), pl.BlockSpec((tail, f2), lambda i: (i, 0))]
    out_shape = [jax.ShapeDtypeStruct((r, d), F32), jax.ShapeDtypeStruct((r // tm * tail, f2), F32)]
    if final:
        out_specs.append(row(d))
        out_shape.append(jax.ShapeDtypeStruct((r, d), F32))
    scratch = [pltpu.VMEM((base + tm, 2 * FFN_CHUNK), F32)]
    if chain:
        scratch.append(pltpu.VMEM((nf, base, 2 * FFN_CHUNK), F32))
    return pl.pallas_call(
        functools.partial(_ffn_kernel, u=u, tpb=tpb, chain=chain, final=final, nf=nf),
        grid=(r // tm,),
        in_specs=specs, out_specs=out_specs, out_shape=out_shape, scratch_shapes=scratch,
        compiler_params=_params("arbitrary"),
        name="out_proj_ffn",
    )(*args)


FOX_W_COLS = 4 * 1024 + LANES


def _fox_proj_kernel(x_ref, g_ref, sc_ref, sh_ref, w_ref, ind_ref, indt_ref, gq_ref, gk_ref, bf_ref,
                     q_ref, kf_ref, kb_ref, vf_ref, vb_ref, og_ref, lf_ref, c_ref, carry_scr, *, tpb):
    h = _norm_mod(x_ref[...], g_ref[...], sc_ref[0], sh_ref[0]).astype(BF16)
    z = _dot(h, w_ref[...])
    tm = z.shape[0]
    ind, indt = ind_ref[...], indt_ref[...]

    def head_norm(zc, gain):
        sq = zc * zc
        hi = sq.astype(BF16)
        lo = (sq - hi.astype(F32)).astype(BF16)
        ms = (_dot(hi, ind) + _dot(lo, ind)) * (1.0 / HEAD_DIM)
        rinv = lax.rsqrt(ms + EPS)
        rh = rinv.astype(BF16)
        rl = (rinv - rh.astype(F32)).astype(BF16)
        return zc * (_dot(rh, indt) + _dot(rl, indt)) * gain

    q_ref[...] = (head_norm(z[:, 0:1024], gq_ref[...]) * SCALE).astype(BF16)
    kn = head_norm(z[:, 1024:2048], gk_ref[...])
    kf_ref[...] = kn
    kb_ref[...] = kn.astype(BF16)
    v = z[:, 2048:3072]
    vf_ref[...] = v
    vb_ref[...] = v.astype(BF16)
    og_ref[...] = jax.nn.sigmoid(z[:, 3072:4096])
    zf = z[:, 4096:4096 + LANES] + bf_ref[...]
    lf = jnp.minimum(zf, 0.0) - jnp.log1p(jnp.exp(-jnp.abs(zf)))
    lf = jnp.where(lax.broadcasted_iota(jnp.int32, lf.shape, 1) < N_HEADS, lf, 0.0)
    lf_ref[...] = lf

    @pl.when(pl.program_id(0) % tpb == 0)
    def _():
        carry_scr[...] = jnp.zeros_like(carry_scr)

    tri = (lax.broadcasted_iota(jnp.int32, (tm, tm), 0) >= lax.broadcasted_iota(jnp.int32, (tm, tm), 1)).astype(BF16)
    hi, mid, lo = _split3(lf)
    c = _dot(tri, hi) + _dot(tri, mid) + _dot(tri, lo) + carry_scr[0:1, :]
    c_ref[...] = c
    carry_scr[0:1, :] = c[tm - 1:tm, :]


def _fox_weight(w_in):
    d = w_in.shape[0]
    return jnp.pad(w_in, ((0, 0), (0, FOX_W_COLS - w_in.shape[1]))).astype(BF16)


def _fox_proj(x, g, scale, shift, w, gq, gk, bf, mode, tm, tpb):
    r, d = x.shape
    mod, _ = _row_specs(mode, tm, tpb, d)
    row = lambda n: pl.BlockSpec((tm, n), lambda i: (i, 0))
    full = lambda a: pl.BlockSpec(a.shape, lambda i: (0,) * a.ndim)
    head_of = jnp.arange(1024) // HEAD_DIM
    ind = (head_of[:, None] == jnp.arange(LANES)[None, :]).astype(BF16)
    gq_t = jnp.tile(gq, N_HEADS)[None]
    gk_t = jnp.tile(gk, N_HEADS)[None]
    bf_p = jnp.pad(bf, (0, LANES - bf.shape[0]))[None]
    outs = [(1024, BF16), (1024, F32), (1024, BF16), (1024, F32), (1024, BF16), (1024, F32), (LANES, F32), (LANES, F32)]
    return pl.pallas_call(
        functools.partial(_fox_proj_kernel, tpb=tpb),
        grid=(r // tm,),
        in_specs=[row(d), full(g), mod, mod, full(w), full(ind), full(ind.T), full(gq_t), full(gk_t), full(bf_p)],
        out_specs=[row(n) for n, _ in outs],
        out_shape=[jax.ShapeDtypeStruct((r, n), t) for n, t in outs],
        scratch_shapes=[pltpu.VMEM((8, LANES), F32)],
        compiler_params=_params("arbitrary"),
        name="fox_proj",
    )(x, g, scale, shift, w, ind, ind.T, gq_t, gk_t, bf_p)


def _fox_attn_kernel(q_ref, k_ref, v_ref, nc_ref, og_ref, o_ref, *, t, nt):
    qi = pl.program_id(2)
    q = q_ref[0]
    lo = lax.broadcasted_iota(jnp.int32, (t, LANES), 1) < HEAD_DIM
    t_q = qi * t + lax.broadcasted_iota(jnp.int32, (t, 1), 0)
    col = lax.broadcasted_iota(jnp.int32, (t, t), 1)
    q_heads = (jnp.where(lo, q, jnp.zeros_like(q)), jnp.where(lo, jnp.zeros_like(q), q))

    def tile(j, carry, masked):
        k0 = pl.multiple_of(j * t, t)
        k = k_ref[0, pl.ds(k0, t), :]
        v = v_ref[0, pl.ds(k0, t), :]
        out = []
        for h2, (m_i, l_i, acc) in enumerate(carry):
            s = _dot_nt(q_heads[h2], k) + nc_ref[0, 0, h2 * nt + j]
            if masked:
                s = jnp.where(k0 + col <= t_q, s, NEG)
            m_n = jnp.maximum(m_i, jnp.max(s, axis=-1, keepdims=True))
            alpha = jnp.exp(m_i - m_n)
            p = jnp.exp(s - m_n)
            l_n = alpha * l_i + jnp.sum(p, axis=-1, keepdims=True)
            out.append((m_n, l_n, alpha * acc + _dot(p.astype(BF16), v)))
        return tuple(out)

    init = (jnp.full((t, 1), NEG, F32), jnp.zeros((t, 1), F32), jnp.zeros((t, LANES), F32))
    carry = lax.fori_loop(0, qi, lambda j, c: tile(j, c, False), (init, init))
    (_, l_0, acc_0), (_, l_1, acc_1) = tile(qi, carry, True)
    o_ref[0] = (jnp.where(lo, acc_0 / l_0, acc_1 / l_1) * og_ref[0]).astype(BF16)


def _fox_attention(q, kb, vb, c, og, b, s):
    t = min(512, s)
    nt = s // t
    hp = N_HEADS // 2
    negc = -c[:, :N_HEADS].reshape(b, nt, t, hp, 2).transpose(0, 3, 4, 1, 2).reshape(b, hp, 2 * nt, 1, t)
    qspec = pl.BlockSpec((1, t, LANES), lambda i, p, qi: (i, qi, p))
    kspec = pl.BlockSpec((1, s, LANES), lambda i, p, qi: (i, 0, p))
    return pl.pallas_call(
        functools.partial(_fox_attn_kernel, t=t, nt=nt),
        grid=(b, hp, nt),
        in_specs=[qspec, kspec, kspec,
                  pl.BlockSpec((1, 1, 2 * nt, 1, t), lambda i, p, qi: (i, p, 0, 0, 0)), qspec],
        out_specs=qspec,
        out_shape=jax.ShapeDtypeStruct((b, s, 1024), BF16),
        compiler_params=_params("parallel", "parallel", "arbitrary"),
        name="fox_attention",
    )(q.reshape(b, s, -1), kb.reshape(b, s, -1), vb.reshape(b, s, -1), negc, og.reshape(b, s, -1))


def _page_spec(shape, slot_block, n_pages):
    nd = len(shape)
    return pl.BlockSpec((1,) + shape, lambda i, p, pt: (pt[i * n_pages + p], slot_block) + (0,) * (nd - 1))


def _nsa_cmp_sample_kernel(*refs, n_pages):
    pt_ref = refs[0]
    pages = refs[1:1 + n_pages]
    pe_ref, w1_ref, w2_ref, o_ref, x_scr = refs[1 + n_pages:]
    del pt_ref
    rows = n_pages * N_GROUPS
    for j in range(n_pages):
        for slot in range(2):
            for g in range(N_GROUPS):
                r0 = (g * n_pages + j) * HEAD_DIM
                x_scr[slot, r0:r0 + HEAD_DIM, :] = pages[j][0, slot, g]
    for slot in range(2):
        acc = jnp.zeros((rows, LANES), F32)
        for dd in range(HEAD_DIM):
            xl = x_scr[slot, pl.ds(dd, rows, stride=HEAD_DIM), :] + pe_ref[slot, dd:dd + 1, :]
            acc = acc + _dot(xl.astype(BF16), w1_ref[slot, dd])
        hid = acc * jax.nn.sigmoid(acc)
        o_ref[0, slot] = _dot(hid.astype(BF16), w2_ref[slot])


def _nsa_cmp_sample(pool_t, pt, pe, w1, w2, db, n_pages):
    per = PAGE // BLOCK
    pe_t = jnp.tile(pe.transpose(0, 2, 1), (1, 1, per))
    w1_d = _block_diag(w1.transpose(0, 2, 1, 3), per).astype(BF16)
    w2_d = _block_diag(w2, per).astype(BF16)
    rows = n_pages * N_GROUPS
    page = lambda j: pl.BlockSpec((1, 2, N_GROUPS, HEAD_DIM, PAGE),
                                  lambda i, pt, j=j: (pt[i * n_pages + j], 0, 0, 0, 0))
    full = lambda a: pl.BlockSpec(a.shape, lambda i, pt: (0,) * a.ndim)
    return pl.pallas_call(
        functools.partial(_nsa_cmp_sample_kernel, n_pages=n_pages),
        grid_spec=pltpu.PrefetchScalarGridSpec(
            num_scalar_prefetch=1, grid=(db,),
            in_specs=[page(j) for j in range(n_pages)] + [full(pe_t), full(w1_d), full(w2_d)],
            out_specs=pl.BlockSpec((1, 2, rows, per * HEAD_DIM), lambda i, pt: (i, 0, 0, 0)),
            scratch_shapes=[pltpu.VMEM((2, rows * HEAD_DIM, PAGE), F32)]),
        out_shape=jax.ShapeDtypeStruct((db, 2, rows, per * HEAD_DIM), F32),
        compiler_params=_params("arbitrary"),
        name="nsa_compress_sample",
    )(pt, *([pool_t] * n_pages), pe_t, w1_d, w2_d)


def _nsa_sel_sample_kernel(qc_ref, qr_ref, kc_ref, vc_ref, win_ref, kn_ref, vn_ref,
                           oc_ref, ow_ref, sel_ref, *, past, dt, nbs):
    nr = qc_ref.shape[1]
    per_r = dt * N_GROUPS
    row = lax.broadcasted_iota(jnp.int32, (nr, 1), 0)
    t_pos = past + (row // N_GROUPS) % dt

    kc = kc_ref[0]
    nb = kc.shape[0]
    n = lax.broadcasted_iota(jnp.int32, (nr, nb), 1)
    cmask = n * BLOCK + (BLOCK - 1) <= t_pos
    sc = jnp.where(cmask, _dot_nt(qc_ref[0], kc), NEG)
    e = jnp.where(cmask, jnp.exp(sc - jnp.max(sc, axis=-1, keepdims=True)), 0.0)
    l = jnp.sum(e, axis=-1, keepdims=True)
    pc = e / jnp.where(l > 0.0, l, 1.0)
    oc_ref[0] = _dot(pc.astype(BF16), vc_ref[0])

    imp = pc[0:per_r]
    for r in range(1, GROUP):
        imp = imp + pc[r * per_r:(r + 1) * per_r]
    n1 = lax.broadcasted_iota(jnp.int32, (per_r, nb), 1)
    t1 = past + lax.broadcasted_iota(jnp.int32, (per_r, 1), 0) // N_GROUPS
    cur = t1 // BLOCK
    forced = (n1 == 0) | (n1 == cur) | (n1 == cur - 1)
    dead = (n1 * BLOCK > t1) | (n1 >= nbs)
    sel = _top_blocks(jnp.where(forced, jnp.inf, jnp.where(dead, -jnp.inf, imp)), min(N_SELECT, nbs))
    sel_ref[0] = jnp.concatenate([sel] * GROUP, axis=0).astype(BF16)

    qr = qr_ref[0]
    wb = win_ref.shape[-1]
    kw = win_ref[0, 0].reshape(N_GROUPS * HEAD_DIM, wb).astype(BF16)
    vw = win_ref[0, 1].reshape(N_GROUPS * HEAD_DIM, wb).astype(BF16)
    d_old = t_pos - (past - wb + lax.broadcasted_iota(jnp.int32, (nr, wb), 1))
    s_old = jnp.where((d_old >= 0) & (d_old < WINDOW), _dot(qr, kw), NEG)
    tn = lax.broadcasted_iota(jnp.int32, (nr, LANES), 1)
    d_new = t_pos - (past + tn)
    s_new = jnp.where((d_new >= 0) & (tn < dt), _dot(qr, kn_ref[0]), NEG)
    m = jnp.maximum(jnp.max(s_old, axis=-1, keepdims=True), jnp.max(s_new, axis=-1, keepdims=True))
    p_old = jnp.exp(s_old - m)
    p_new = jnp.exp(s_new - m)
    lw = jnp.sum(p_old, axis=-1, keepdims=True) + jnp.sum(p_new, axis=-1, keepdims=True)
    ow_ref[0] = (_dot_nt(p_old.astype(BF16), vw) + _dot_nt(p_new.astype(BF16), vn_ref[0])) / lw


def _nsa_sel_sample(qbd_c, qbd_r, kc, vc, win_t, kn_t, vn_t, past, dt, nbs):
    db, nr, gd = qbd_c.shape
    blk = lambda a: pl.BlockSpec((1,) + a.shape[1:], lambda i: (i,) + (0,) * (a.ndim - 1))
    args = (qbd_c, qbd_r, kc, vc, win_t, kn_t, vn_t)
    return pl.pallas_call(
        functools.partial(_nsa_sel_sample_kernel, past=past, dt=dt, nbs=nbs),
        grid=(db,),
        in_specs=[blk(a) for a in args],
        out_specs=[pl.BlockSpec((1, nr, gd), lambda i: (i, 0, 0)), pl.BlockSpec((1, nr, gd), lambda i: (i, 0, 0)),
                   pl.BlockSpec((1, nr, LANES), lambda i: (i, 0, 0))],
        out_shape=[jax.ShapeDtypeStruct((db, nr, gd), F32), jax.ShapeDtypeStruct((db, nr, gd), F32),
                   jax.ShapeDtypeStruct((db, nr, LANES), BF16)],
        compiler_params=_params("parallel"),
        name="nsa_select_window_sample",
    )(*args)


def _online_update(s, v_t, m_scr, l_scr, acc_scr):
    m_i = m_scr[...]
    m_n = jnp.maximum(m_i, jnp.max(s, axis=-1, keepdims=True))
    alpha = jnp.exp(m_i - m_n)
    p = jnp.exp(s - m_n)
    l_scr[...] = alpha * l_scr[...] + jnp.sum(p, axis=-1, keepdims=True)
    acc_scr[...] = alpha * acc_scr[...] + _dot_nt(p.astype(BF16), v_t)
    m_scr[...] = m_n


def _nsa_slc_sample_kernel(pt_ref, page_ref, q_ref, sel_ref, e_ref, en_ref, kn_ref, vn_ref, oc_ref, ow_ref, gt_ref,
                           o_ref, m_scr, l_scr, acc_scr, *, past, dt):
    del pt_ref
    p = pl.program_id(1)
    gd = N_GROUPS * HEAD_DIM

    @pl.when(p == 0)
    def _():
        m_scr[...] = jnp.full(m_scr.shape, NEG, F32)
        l_scr[...] = jnp.zeros_like(l_scr)
        acc_scr[...] = jnp.zeros_like(acc_scr)

    q = q_ref[0]
    sel = sel_ref[0]
    k_t = page_ref[0, 0].reshape(gd, PAGE).astype(BF16)
    v_t = page_ref[0, 1].reshape(gd, PAGE).astype(BF16)
    s = jnp.where(_dot(sel, e_ref[0]) > 0.5, _dot(q, k_t), NEG)
    _online_update(s, v_t, m_scr, l_scr, acc_scr)

    @pl.when(p == pl.num_programs(1) - 1)
    def _():
        nr = q.shape[0]
        row = lax.broadcasted_iota(jnp.int32, (nr, 1), 0)
        t_row = (row // N_GROUPS) % dt
        tn = lax.broadcasted_iota(jnp.int32, (nr, LANES), 1)
        ok = (_dot(sel, en_ref[...]) > 0.5) & (tn <= t_row) & (tn < dt)
        _online_update(jnp.where(ok, _dot(q, kn_ref[0]), NEG), vn_ref[0], m_scr, l_scr, acc_scr)
        gt = gt_ref[0]
        o_ref[0] = (gt[:, 0:1] * oc_ref[0] + gt[:, 1:2] * (acc_scr[...] / l_scr[...]) + gt[:, 2:3] * ow_ref[0])


def _nsa_slc_sample(pool_t, pt, qbd_r, sel, kn_t, vn_t, o_c, o_w, gates, past, dt, n_pages):
    db, nr, gd = qbd_r.shape
    per = PAGE // BLOCK
    key_blk = jnp.arange(n_pages * PAGE) // BLOCK
    e_tab = (jnp.arange(LANES)[None, :, None] == key_blk.reshape(n_pages, 1, PAGE)).astype(BF16)
    new_blk = jnp.where(jnp.arange(LANES) < dt, (past + jnp.arange(LANES)) // BLOCK, -1)
    e_new = (jnp.arange(LANES)[:, None] == new_blk[None, :]).astype(BF16)
    del per
    blk = lambda a: pl.BlockSpec((1,) + a.shape[1:], lambda i, p, pt: (i,) + (0,) * (a.ndim - 1))
    return pl.pallas_call(
        functools.partial(_nsa_slc_sample_kernel, past=past, dt=dt),
        grid_spec=pltpu.PrefetchScalarGridSpec(
            num_scalar_prefetch=1, grid=(db, n_pages),
            in_specs=[_page_spec((2, N_GROUPS, HEAD_DIM, PAGE), 1, n_pages), blk(qbd_r), blk(sel),
                      pl.BlockSpec((1, LANES, PAGE), lambda i, p, pt: (p, 0, 0)),
                      pl.BlockSpec(e_new.shape, lambda i, p, pt: (0, 0)),
                      blk(kn_t), blk(vn_t), blk(o_c), blk(o_w), blk(gates)],
            out_specs=pl.BlockSpec((1, nr, gd), lambda i, p, pt: (i, 0, 0)),
            scratch_shapes=[pltpu.VMEM((nr, 1), F32), pltpu.VMEM((nr, 1), F32), pltpu.VMEM((nr, gd), F32)]),
        out_shape=jax.ShapeDtypeStruct((db, nr, gd), F32),
        compiler_params=_params("parallel", "arbitrary"),
        name="nsa_selected_sample",
    )(pt, pool_t, qbd_r, sel, e_tab, e_new, kn_t, vn_t, o_c, o_w, gates)


def _fox_sample_kernel(pt_ref, page_ref, lf_ref, q_ref, kn_ref, vn_ref, lfn_ref, og_ref,
                       o_ref, m_scr, l_scr, acc_scr, c_scr, *, dt):
    del pt_ref
    p = pl.program_id(1)
    hd = N_HEADS * HEAD_DIM

    @pl.when(p == 0)
    def _():
        m_scr[...] = jnp.full(m_scr.shape, NEG, F32)
        l_scr[...] = jnp.zeros_like(l_scr)
        acc_scr[...] = jnp.zeros_like(acc_scr)
        c_scr[...] = jnp.zeros_like(c_scr)

    tri = (lax.broadcasted_iota(jnp.int32, (PAGE, PAGE), 0) <= lax.broadcasted_iota(jnp.int32, (PAGE, PAGE), 1)).astype(BF16)

    def cum(lf):
        hi, mid, lo = _split3(lf)
        return _dot(hi, tri) + _dot(mid, tri) + _dot(lo, tri) + c_scr[...]

    q = q_ref[0]
    c_page = cum(lf_ref[0])
    k_t = page_ref[0, 0].reshape(hd, PAGE).astype(BF16)
    v_t = page_ref[0, 1].reshape(hd, PAGE).astype(BF16)
    s = _dot(q, k_t) - jnp.concatenate([c_page] * dt, axis=0)
    _online_update(s, v_t, m_scr, l_scr, acc_scr)
    c_scr[...] = jnp.broadcast_to(c_page[:, PAGE - 1:PAGE], c_scr.shape)

    @pl.when(p == pl.num_programs(1) - 1)
    def _():
        nr = q.shape[0]
        t_row = lax.broadcasted_iota(jnp.int32, (nr, 1), 0) // N_HEADS
        tn = lax.broadcasted_iota(jnp.int32, (nr, LANES), 1)
        c_new = cum(lfn_ref[0])
        s_n = _dot(q, kn_ref[0]) - jnp.concatenate([c_new] * dt, axis=0)
        s_n = jnp.where((tn <= t_row) & (tn < dt), s_n, NEG)
        _online_update(s_n, vn_ref[0], m_scr, l_scr, acc_scr)
        o_ref[0] = acc_scr[...] / l_scr[...] * og_ref[0]


def _fox_sample(pool_t, lf_pool_t, pt, qbd, kn_t, vn_t, lfn_t, og_t, dt, n_pages):
    db, nr, hd = qbd.shape
    blk = lambda a: pl.BlockSpec((1,) + a.shape[1:], lambda i, p, pt: (i,) + (0,) * (a.ndim - 1))
    return pl.pallas_call(
        functools.partial(_fox_sample_kernel, dt=dt),
        grid_spec=pltpu.PrefetchScalarGridSpec(
            num_scalar_prefetch=1, grid=(db, n_pages),
            in_specs=[_page_spec((2, N_HEADS, HEAD_DIM, PAGE), 0, n_pages),
                      pl.BlockSpec((1, N_HEADS, PAGE), lambda i, p, pt: (pt[i * n_pages + p], 0, 0)),
                      blk(qbd), blk(kn_t), blk(vn_t), blk(lfn_t), blk(og_t)],
            out_specs=pl.BlockSpec((1, nr, hd), lambda i, p, pt: (i, 0, 0)),
            scratch_shapes=[pltpu.VMEM((nr, 1), F32), pltpu.VMEM((nr, 1), F32), pltpu.VMEM((nr, hd), F32),
                            pltpu.VMEM((N_HEADS, PAGE), F32)]),
        out_shape=jax.ShapeDtypeStruct((db, nr, hd), F32),
        compiler_params=_params("parallel", "arbitrary"),
        name="fox_attention_sample",
    )(pt, pool_t, lf_pool_t, qbd, kn_t, vn_t, lfn_t, og_t)


def _page_specs(block, slot_block, n_pages):
    nd = len(block)
    return [pl.BlockSpec((1,) + block, lambda i, pt, j=j: (pt[i * n_pages + j], slot_block) + (0,) * (nd - 1))
            for j in range(n_pages)]


def _softmax_chunks(scores):
    m = scores[0]
    for s in scores[1:]:
        m = jnp.maximum(m, s)
    m = jnp.max(m, axis=-1, keepdims=True)
    ps = [jnp.exp(s - m) for s in scores]
    tot = ps[0]
    for p in ps[1:]:
        tot = tot + p
    return ps, jnp.sum(tot, axis=-1, keepdims=True)


def _nsa_attn_sample_kernel(*refs, n_pages, past, dt, nbs):
    pages = refs[1:1 + n_pages]
    (qc_ref, qr_ref, kc_ref, vc_ref, win_ref, kwn_ref, vwn_ref, ksn_ref, vsn_ref, e_ref, en_ref, gt_ref,
     o_ref) = refs[1 + n_pages:]
    gd = N_GROUPS * HEAD_DIM
    nr = qc_ref.shape[1]
    per_r = dt * N_GROUPS
    row = lax.broadcasted_iota(jnp.int32, (nr, 1), 0)
    t_row = (row // N_GROUPS) % dt
    t_pos = past + t_row
    tn = lax.broadcasted_iota(jnp.int32, (nr, LANES), 1)

    kc = kc_ref[0]
    nb = kc.shape[0]
    n = lax.broadcasted_iota(jnp.int32, (nr, nb), 1)
    cmask = n * BLOCK + (BLOCK - 1) <= t_pos
    sc = jnp.where(cmask, _dot_nt(qc_ref[0], kc), NEG)
    e = jnp.where(cmask, jnp.exp(sc - jnp.max(sc, axis=-1, keepdims=True)), 0.0)
    l = jnp.sum(e, axis=-1, keepdims=True)
    pc = e / jnp.where(l > 0.0, l, 1.0)
    o_c = _dot(pc.astype(BF16), vc_ref[0])

    imp = pc[0:per_r]
    for r in range(1, GROUP):
        imp = imp + pc[r * per_r:(r + 1) * per_r]
    n1 = lax.broadcasted_iota(jnp.int32, (per_r, nb), 1)
    t1 = past + lax.broadcasted_iota(jnp.int32, (per_r, 1), 0) // N_GROUPS
    cur = t1 // BLOCK
    forced = (n1 == 0) | (n1 == cur) | (n1 == cur - 1)
    dead = (n1 * BLOCK > t1) | (n1 >= nbs)
    sel = _top_blocks(jnp.where(forced, jnp.inf, jnp.where(dead, -jnp.inf, imp)), min(N_SELECT, nbs))
    selb = jnp.concatenate([sel] * GROUP, axis=0).astype(BF16)

    qr = qr_ref[0]
    wb = win_ref.shape[-1]
    kw = win_ref[0, 0].reshape(gd, wb).astype(BF16)
    vw = win_ref[0, 1].reshape(gd, wb).astype(BF16)
    d_old = t_pos - (past - wb + lax.broadcasted_iota(jnp.int32, (nr, wb), 1))
    s_old = jnp.where((d_old >= 0) & (d_old < WINDOW), _dot(qr, kw), NEG)
    s_new = jnp.where((tn <= t_row) & (tn < dt), _dot(qr, kwn_ref[0]), NEG)
    m = jnp.maximum(jnp.max(s_old, axis=-1, keepdims=True), jnp.max(s_new, axis=-1, keepdims=True))
    p_old = jnp.exp(s_old - m)
    p_new = jnp.exp(s_new - m)
    lw = jnp.sum(p_old, axis=-1, keepdims=True) + jnp.sum(p_new, axis=-1, keepdims=True)
    o_w = (_dot_nt(p_old.astype(BF16), vw) + _dot_nt(p_new.astype(BF16), vwn_ref[0])) / lw

    scores = []
    for j in range(n_pages):
        k_t = pages[j][0, 0].reshape(gd, PAGE).astype(BF16)
        scores.append(jnp.where(_dot(selb, e_ref[j]) > 0.5, _dot(qr, k_t), NEG))
    ok = (_dot(selb, en_ref[...]) > 0.5) & (tn <= t_row) & (tn < dt)
    scores.append(jnp.where(ok, _dot(qr, ksn_ref[0]), NEG))
    ps, ls = _softmax_chunks(scores)
    acc = _dot_nt(ps[n_pages].astype(BF16), vsn_ref[0])
    for j in range(n_pages):
        acc = acc + _dot_nt(ps[j].astype(BF16), pages[j][0, 1].reshape(gd, PAGE).astype(BF16))
    gt = gt_ref[0]
    o_ref[0] = gt[:, 0:1] * o_c + gt[:, 1:2] * (acc / ls) + gt[:, 2:3] * o_w


def _nsa_attn_sample(pool_t, pt, qbd_c, qbd_r, kc, vc, win_t, kwn, vwn, ksn, vsn, gates, past, dt, nbs, n_pages):
    db, nr, gd = qbd_r.shape
    key_blk = jnp.arange(n_pages * PAGE) // BLOCK
    e_tab = (jnp.arange(LANES)[None, :, None] == key_blk.reshape(n_pages, 1, PAGE)).astype(BF16)
    new_blk = jnp.where(jnp.arange(LANES) < dt, (past + jnp.arange(LANES)) // BLOCK, -1)
    e_new = (jnp.arange(LANES)[:, None] == new_blk[None, :]).astype(BF16)
    blk = lambda a: pl.BlockSpec((1,) + a.shape[1:], lambda i, pt: (i,) + (0,) * (a.ndim - 1))
    full = lambda a: pl.BlockSpec(a.shape, lambda i, pt: (0,) * a.ndim)
    per_batch = (qbd_c, qbd_r, kc, vc, win_t, kwn, vwn, ksn, vsn)
    return pl.pallas_call(
        functools.partial(_nsa_attn_sample_kernel, n_pages=n_pages, past=past, dt=dt, nbs=nbs),
        grid_spec=pltpu.PrefetchScalarGridSpec(
            num_scalar_prefetch=1, grid=(db,),
            in_specs=(_page_specs((2, N_GROUPS, HEAD_DIM, PAGE), 1, n_pages) + [blk(a) for a in per_batch]
                      + [full(e_tab), full(e_new), blk(gates)]),
            out_specs=pl.BlockSpec((1, nr, gd), lambda i, pt: (i, 0, 0))),
        out_shape=jax.ShapeDtypeStruct((db, nr, gd), F32),
        compiler_params=_params("parallel"),
        name="nsa_attention_sample",
    )(pt, *([pool_t] * n_pages), *per_batch, e_tab, e_new, gates)


def _fox_attn_sample_kernel(*refs, n_pages, dt):
    pages = refs[1:1 + n_pages]
    lfs = refs[1 + n_pages:1 + 2 * n_pages]
    q_ref, kn_ref, vn_ref, lfn_ref, og_ref, o_ref = refs[1 + 2 * n_pages:]
    hd = N_HEADS * HEAD_DIM
    tri = (lax.broadcasted_iota(jnp.int32, (PAGE, PAGE), 0) <= lax.broadcasted_iota(jnp.int32, (PAGE, PAGE), 1)).astype(BF16)

    def local_cum(lf):
        hi, mid, lo = _split3(lf)
        return _dot(hi, tri) + _dot(mid, tri) + _dot(lo, tri)

    q = q_ref[0]
    nr = q.shape[0]
    scores = []
    prefix = jnp.zeros((N_HEADS, 1), F32)
    for j in range(n_pages):
        loc = local_cum(lfs[j][0])
        c_page = loc + prefix
        prefix = prefix + loc[:, PAGE - 1:PAGE]
        k_t = pages[j][0, 0].reshape(hd, PAGE).astype(BF16)
        scores.append(_dot(q, k_t) - jnp.concatenate([c_page] * dt, axis=0))
    c_new = local_cum(lfn_ref[0]) + prefix
    t_row = lax.broadcasted_iota(jnp.int32, (nr, 1), 0) // N_HEADS
    tn = lax.broadcasted_iota(jnp.int32, (nr, LANES), 1)
    s_n = _dot(q, kn_ref[0]) - jnp.concatenate([c_new] * dt, axis=0)
    scores.append(jnp.where((tn <= t_row) & (tn < dt), s_n, NEG))
    ps, ls = _softmax_chunks(scores)
    acc = _dot_nt(ps[n_pages].astype(BF16), vn_ref[0])
    for j in range(n_pages):
        acc = acc + _dot_nt(ps[j].astype(BF16), pages[j][0, 1].reshape(hd, PAGE).astype(BF16))
    o_ref[0] = acc / ls * og_ref[0]


def _fox_attn_sample(pool_t, lf_pool_t, pt, qbd, kn_t, vn_t, lfn_t, og_t, dt, n_pages):
    db, nr, hd = qbd.shape
    blk = lambda a: pl.BlockSpec((1,) + a.shape[1:], lambda i, pt: (i,) + (0,) * (a.ndim - 1))
    lf_specs = [pl.BlockSpec((1, N_HEADS, PAGE), lambda i, pt, j=j: (pt[i * n_pages + j], 0, 0))
                for j in range(n_pages)]
    per_batch = (qbd, kn_t, vn_t, lfn_t, og_t)
    return pl.pallas_call(
        functools.partial(_fox_attn_sample_kernel, n_pages=n_pages, dt=dt),
        grid_spec=pltpu.PrefetchScalarGridSpec(
            num_scalar_prefetch=1, grid=(db,),
            in_specs=_page_specs((2, N_HEADS, HEAD_DIM, PAGE), 0, n_pages) + lf_specs + [blk(a) for a in per_batch],
            out_specs=pl.BlockSpec((1, nr, hd), lambda i, pt: (i, 0, 0))),
        out_shape=jax.ShapeDtypeStruct((db, nr, hd), F32),
        compiler_params=_params("parallel"),
        name="fox_attention_sample",
    )(pt, *([pool_t] * n_pages), *([lf_pool_t] * n_pages), *per_batch)


def _per_batch(a, dt, db):
    return a.reshape(dt, db, -1).transpose(1, 0, 2)


def _new_keys_t(a, dt, db):
    a = _per_batch(a, dt, db).transpose(0, 2, 1)
    return jnp.pad(a, ((0, 0), (0, 0), (0, LANES - dt))).astype(a.dtype)


def _nsa_qbd(q, dt, db):
    q5 = q.reshape(dt, db, N_GROUPS, GROUP, HEAD_DIM).transpose(1, 3, 0, 2, 4)
    eye = jnp.eye(N_GROUPS, dtype=q.dtype)
    out = q5[:, :, :, :, None, :] * eye[None, None, None, :, :, None]
    return out.reshape(db, GROUP * dt * N_GROUPS, N_GROUPS * HEAD_DIM)


def _nsa_undiag(o, dt, db):
    o6 = o.reshape(db, GROUP, dt, N_GROUPS, N_GROUPS, HEAD_DIM)
    dg = jnp.diagonal(o6, axis1=3, axis2=4)
    return dg.transpose(2, 0, 4, 1, 3).reshape(dt * db, N_HEADS * HEAD_DIM)


def _fox_qbd(q, dt, db):
    q4 = q.reshape(dt, db, N_HEADS, HEAD_DIM).transpose(1, 0, 2, 3)
    eye = jnp.eye(N_HEADS, dtype=q.dtype)
    out = q4[:, :, :, None, :] * eye[None, None, :, :, None]
    return out.reshape(db, dt * N_HEADS, N_HEADS * HEAD_DIM)


def _fox_undiag(o, dt, db):
    o5 = o.reshape(db, dt, N_HEADS, N_HEADS, HEAD_DIM)
    dg = jnp.diagonal(o5, axis1=2, axis2=3)
    return dg.transpose(1, 0, 3, 2).reshape(dt * db, N_HEADS * HEAD_DIM)


def _prompt_mods(mod, b):
    return [m.reshape(b, 1, -1) for m in jnp.split(mod[:b], 6, axis=-1)]


def _sample_mods(mod, b, t):
    return [jnp.tile(m, (t, 1))[None] for m in jnp.split(mod[b:], 6, axis=-1)]


def kernel(x_prompt, x_sample, cache_nsa_kv, cache_nsa_win, cache_fox_kv, cache_fox_logf, state_ffn_conv, page_table, c_prompt, c_sample, w_ada, b_ada, norm_mix_g, norm_ffn_g, w_nsa_in, pe_cmp, w_cmp1, w_cmp2, w_nsa_out, w_fox_in, b_fox_f, fox_q_norm_g, fox_k_norm_g, w_fox_out, w_ffn_up, ffn_conv_w, ffn_conv_b, w_ffn_down, final_norm_g):
    b, s, d = x_prompt.shape
    db, dt, _ = x_sample.shape
    f_dim = w_ffn_down.shape[1]
    depth = w_ada.shape[0]
    tm = 256
    tpb = s // tm

    n_pages = page_table.shape[1]
    past = n_pages * PAGE
    nbs = -(-(past + dt) // BLOCK)
    per = PAGE // BLOCK
    r_s = dt * db
    pt = page_table.reshape(-1).astype(jnp.int32)
    key_last = (0, 2, 3, 4, 1)

    c_all = jnp.concatenate([c_prompt, c_sample], axis=0)
    xp = x_prompt.reshape(b * s, d)
    xs = x_sample.transpose(1, 0, 2).reshape(r_s, d)
    tabs_p = _rope_tables(jnp.arange(s, dtype=jnp.int32))
    tabs_s = _rope_tables(past + jnp.arange(r_s, dtype=jnp.int32) // db)

    nsa_kv_p, nsa_win_p, fox_kv_p, fox_lf_p, conv_p = [], [], [], [], []
    nsa_kv_s, nsa_win_s, fox_kv_s, fox_lf_s, conv_s = [], [], [], [], []
    y_prompt = y_sample = None
    for i in range(depth):
        j = i // 2
        mod = _adaln(c_all, w_ada[i].astype(BF16), b_ada[i][None])
        mp = _prompt_mods(mod, b)
        ms = _sample_mods(mod, b, dt)
        g_mix = norm_mix_g[i][None]
        if i % 2 == 0:
            w_in = _nsa_weight(w_nsa_in[j])
            qc, qr, rows, win, dup, gates = _nsa_proj(xp, g_mix, mp[1], mp[0], w_in, tabs_p, "prompt", tm, tpb)
            cdup = _compress_prompt(rows, *_cmp_weights(pe_cmp[j], w_cmp1[j], w_cmp2[j]), b, s)
            attn_p = _nsa_attention(qc, qr, cdup, dup, gates, b, s).reshape(b * s, d)
            w_out = w_nsa_out[j].astype(BF16)
            nsa_kv_p.append(rows.reshape(b, s, 4, N_GROUPS, HEAD_DIM))
            nsa_win_p.append(win.reshape(b, s, 2, N_GROUPS, HEAD_DIM)[:, s - min(WINDOW, s):])

            qc, qr, rows, win, _, gates = _nsa_proj(xs, g_mix, ms[1], ms[0], w_in, tabs_s, "sample", r_s, 1)
            nsa_kv_s.append(rows.reshape(dt, db, 4, N_GROUPS, HEAD_DIM).transpose(1, 0, 2, 3, 4))
            nsa_win_s.append(win.reshape(dt, db, 2, N_GROUPS, HEAD_DIM).transpose(1, 0, 2, 3, 4))
            pool_t = jnp.transpose(cache_nsa_kv[j], key_last)
            kc2 = _nsa_cmp_sample(pool_t, pt, pe_cmp[j], w_cmp1[j], w_cmp2[j], db, n_pages)
            kc = kc2.reshape(db, 2, N_GROUPS, n_pages, per, HEAD_DIM).transpose(0, 1, 3, 4, 2, 5)
            kc = kc.reshape(db, 2, n_pages * per, N_GROUPS * HEAD_DIM)
            kc = jnp.pad(kc, ((0, 0), (0, 0), (0, LANES - n_pages * per), (0, 0))).astype(BF16)
            qbd_c, qbd_r = _nsa_qbd(qc, dt, db), _nsa_qbd(qr, dt, db)
            gd = N_GROUPS * HEAD_DIM
            newt = lambda a: _new_keys_t(a.astype(BF16), dt, db)
            g_s = gates.reshape(dt, db, N_GROUPS, LANES)[..., :GROUP * 3].reshape(dt, db, N_GROUPS, GROUP, 3)
            g_s = g_s.transpose(1, 3, 0, 2, 4).reshape(db, GROUP * dt * N_GROUPS, 3)
            g_s = jnp.pad(g_s, ((0, 0), (0, 0), (0, LANES - 3)))
            o_s = _nsa_attn_sample(pool_t, pt, qbd_c, qbd_r, kc[:, 0], kc[:, 1],
                                   jnp.transpose(cache_nsa_win[j], key_last), newt(win[:, :gd]), newt(win[:, gd:]),
                                   newt(rows[:, 2 * gd:3 * gd]), newt(rows[:, 3 * gd:]), g_s, past, dt, nbs, n_pages)
            attn_s = _nsa_undiag(o_s, dt, db).astype(BF16)
        else:
            w_in = _fox_weight(w_fox_in[j])
            fox = lambda x, sc, sh, mode, t, n: _fox_proj(x, g_mix, sc, sh, w_in, fox_q_norm_g[j], fox_k_norm_g[j],
                                                         b_fox_f[j], mode, t, n)
            q, kf, kb, vf, vb, og, lf, c = fox(xp, mp[1], mp[0], "prompt", tm, tpb)
            attn_p = _fox_attention(q, kb, vb, c, og, b, s).reshape(b * s, d)
            w_out = w_fox_out[j].astype(BF16)
            hs = (N_HEADS, HEAD_DIM)
            fox_kv_p.append(jnp.stack([kf.reshape((b, s) + hs), vf.reshape((b, s) + hs)], axis=2))
            fox_lf_p.append(lf[:, :N_HEADS].reshape(b, s, N_HEADS))

            q, kf, kb, vf, vb, og, lf, _ = fox(xs, ms[1], ms[0], "sample", r_s, 1)
            fox_kv_s.append(jnp.stack([kf.reshape((dt, db) + hs), vf.reshape((dt, db) + hs)], axis=2).transpose(1, 0, 2, 3, 4))
            fox_lf_s.append(lf[:, :N_HEADS].reshape(dt, db, N_HEADS).transpose(1, 0, 2))
            og_t = jnp.repeat(_per_batch(og, dt, db)[:, :, None, :], N_HEADS, axis=2).reshape(db, dt * N_HEADS, -1)
            o_full = _fox_attn_sample(jnp.transpose(cache_fox_kv[j], key_last),
                                      jnp.transpose(cache_fox_logf[j], (0, 2, 1)), pt, _fox_qbd(q, dt, db),
                                      _new_keys_t(kb, dt, db), _new_keys_t(vb, dt, db),
                                      _new_keys_t(lf[:, :N_HEADS], dt, db), og_t, dt, n_pages)
            attn_s = _fox_undiag(o_full, dt, db).astype(BF16)

        final = final_norm_g[None] if i == depth - 1 else None
        ffn_w = (norm_ffn_g[i][None], _ffn_cols(w_ffn_up[i].astype(BF16)), _ffn_cols(ffn_conv_w[i]),
                 _ffn_cols(ffn_conv_b[i][None]), w_ffn_down[i].astype(BF16))
        unperm = lambda a: _ffn_cols(a, inverse=True)

        res = _ffn(xp, attn_p, w_out, (mp[2], mp[4], mp[3], mp[5]), *ffn_w, "prompt", tm, tpb, 1, final_g=final)
        xp = res[0]
        conv_p.append(unperm(res[1].reshape(b, tpb, 8, 2 * f_dim)[:, -1, 6:, :]))
        if final is not None:
            y_prompt = res[2].reshape(b, s, d)

        state = _ffn_cols(state_ffn_conv[i].transpose(1, 0, 2).reshape(2 * db, 2 * f_dim))
        res = _ffn(xs, attn_s, w_out, (ms[2], ms[4], ms[3], ms[5]), *ffn_w, "sample", r_s, 1, db, state=state,
                   final_g=final)
        xs = res[0]
        conv_s.append(unperm(res[1]).reshape(2, db, 2 * f_dim).transpose(1, 0, 2))
        if final is not None:
            y_sample = res[2].reshape(dt, db, d).transpose(1, 0, 2)

    return (y_prompt, y_sample, jnp.stack(nsa_kv_p), jnp.stack(nsa_kv_s), jnp.stack(nsa_win_p), jnp.stack(nsa_win_s),
            jnp.stack(fox_kv_p), jnp.stack(fox_kv_s), jnp.stack(fox_lf_p), jnp.stack(fox_lf_s),
            jnp.stack(conv_p), jnp.stack(conv_s))
```

```python
import functools

import jax
import jax.numpy as jnp
from jax import lax
from jax.experimental import pallas as pl
from jax.experimental.pallas import tpu as pltpu

F32 = jnp.float32
BF16 = jnp.bfloat16

HEAD_DIM = 64
N_HEADS = 16
N_GROUPS = 4
GROUP = N_HEADS // N_GROUPS
BLOCK = 64
N_SELECT = 16
WINDOW = 512
ROT_DIM = 16
ROPE_THETA = 500000.0
PAGE = 128
Q_BLOCK = 128
EPS = 1e-6
NEG = -1e30
SCALE = HEAD_DIM ** -0.5

LANES = 128
VMEM_LIMIT = 56 * 1024 * 1024


def _params(*sem, flags=None):
    return pltpu.CompilerParams(dimension_semantics=sem, vmem_limit_bytes=VMEM_LIMIT, flags=flags)


def _dot(a, b):
    return jnp.dot(a, b, preferred_element_type=F32)


def _dot_nt(a, b):
    return lax.dot_general(a, b, (((1,), (1,)), ((), ())), preferred_element_type=F32)


def _split3(x):
    hi = x.astype(BF16)
    r1 = x - hi.astype(F32)
    mid = r1.astype(BF16)
    lo = (r1 - mid.astype(F32)).astype(BF16)
    return hi, mid, lo


def _ada_kernel(c_ref, w_ref, b_ref, o_ref):
    c = c_ref[...]
    a = (c * jax.nn.sigmoid(c)).astype(BF16)
    o_ref[...] = _dot(a, w_ref[...]) + b_ref[...]


def _adaln(c, w, b):
    r, d = c.shape
    n = w.shape[1]
    tn = n // 4
    return pl.pallas_call(
        _ada_kernel,
        grid=(n // tn,),
        in_specs=[pl.BlockSpec((r, d), lambda j: (0, 0)),
                  pl.BlockSpec((d, tn), lambda j: (0, j)),
                  pl.BlockSpec((1, tn), lambda j: (0, j))],
        out_specs=pl.BlockSpec((r, tn), lambda j: (0, j)),
        out_shape=jax.ShapeDtypeStruct((r, n), F32),
        compiler_params=_params("arbitrary"),
        name="adaln",
    )(c, w, b)


def _norm_mod(x, g, scale, shift):
    ms = jnp.mean(x * x, axis=-1, keepdims=True)
    return (x * lax.rsqrt(ms + EPS) * g) * (1.0 + scale) + shift


def _rope_tables(pos):
    freqs = ROPE_THETA ** (-jnp.arange(0, ROT_DIM, 2, dtype=F32) / ROT_DIM)
    ang = pos.astype(F32)[:, None] * freqs[None, :]
    cos, sin = jnp.cos(ang), jnp.sin(ang)
    half = ROT_DIM // 2
    one = jnp.ones((pos.shape[0], HEAD_DIM - ROT_DIM), F32)
    zero = jnp.zeros_like(one)
    zh = jnp.zeros_like(cos)
    c = jnp.concatenate([cos, cos, one], axis=1)
    s_lo = jnp.concatenate([zh, sin, zero], axis=1)
    s_hi = jnp.concatenate([-sin, zh, zero], axis=1)
    rep = LANES // HEAD_DIM
    return jnp.tile(c, (1, rep)), jnp.tile(s_lo, (1, rep)), jnp.tile(s_hi, (1, rep))


def _rope(v, c, s_lo, s_hi):
    half = ROT_DIM // 2
    return v * c + pltpu.roll(v, half, 1) * s_lo + pltpu.roll(v, LANES - half, 1) * s_hi


def _row_specs(mode, tm, tpb, d):
    if mode == "prompt":
        mod = pl.BlockSpec((1, 1, d), lambda i: (i // tpb, 0, 0))
        tab = pl.BlockSpec((tm, LANES), lambda i: (i % tpb, 0))
    else:
        mod = pl.BlockSpec((1, tm, d), lambda i: (0, 0, 0))
        tab = pl.BlockSpec((tm, LANES), lambda i: (0, 0))
    return mod, tab


NSA_DUP = 4 * N_GROUPS * LANES
NSA_W_COLS = 1024 + 6 * 256 + N_GROUPS * LANES


def _nsa_proj_kernel(x_ref, g_ref, sc_ref, sh_ref, w_ref, tc_ref, tl_ref, th_ref,
                     qc_ref, qr_ref, rows_ref, win_ref, dup_ref, gates_ref):
    h = _norm_mod(x_ref[...], g_ref[...], sc_ref[0], sh_ref[0]).astype(BF16)
    z = _dot(h, w_ref[...])
    tc, tl, th = tc_ref[...], tl_ref[...], th_ref[...]
    lo = lax.broadcasted_iota(jnp.int32, (z.shape[0], LANES), 1) < HEAD_DIM

    def chunk(j):
        return z[:, j * LANES:(j + 1) * LANES]

    def put_dup(kind, pair, v):
        vr = pltpu.roll(v, HEAD_DIM, 1)
        base = (kind * N_GROUPS + 2 * pair) * LANES
        dup_ref[:, base:base + LANES] = jnp.where(lo, v, vr).astype(BF16)
        dup_ref[:, base + LANES:base + 2 * LANES] = jnp.where(lo, vr, v).astype(BF16)

    for j in range(8):
        v = chunk(j)
        sl = slice(j * LANES, (j + 1) * LANES)
        qc_ref[:, sl] = (v * SCALE).astype(BF16)
        qr_ref[:, sl] = (_rope(v, tc, tl, th) * SCALE).astype(BF16)
    for j in range(4):
        rows_ref[:, j * LANES:(j + 1) * LANES] = chunk(8 + j)
    for j in range(2):
        ks = _rope(chunk(12 + j), tc, tl, th)
        vs = chunk(14 + j)
        rows_ref[:, (4 + j) * LANES:(5 + j) * LANES] = ks
        rows_ref[:, (6 + j) * LANES:(7 + j) * LANES] = vs
        put_dup(0, j, ks)
        put_dup(1, j, vs)
    for j in range(2):
        kw = _rope(chunk(16 + j), tc, tl, th)
        vw = chunk(18 + j)
        win_ref[:, j * LANES:(j + 1) * LANES] = kw
        win_ref[:, (2 + j) * LANES:(3 + j) * LANES] = vw
        put_dup(2, j, kw)
        put_dup(3, j, vw)
    for j in range(N_GROUPS):
        gates_ref[:, j * LANES:(j + 1) * LANES] = jax.nn.sigmoid(chunk(20 + j))


def _nsa_proj(x, g, scale, shift, w, tabs, mode, tm, tpb):
    r, d = x.shape
    mod, tab = _row_specs(mode, tm, tpb, d)
    row = lambda n: pl.BlockSpec((tm, n), lambda i: (i, 0))
    outs = [(1024, BF16), (1024, BF16), (1024, F32), (512, F32), (NSA_DUP, BF16), (N_GROUPS * LANES, F32)]
    return pl.pallas_call(
        _nsa_proj_kernel,
        grid=(r // tm,),
        in_specs=[row(d), pl.BlockSpec((1, d), lambda i: (0, 0)), mod, mod,
                  pl.BlockSpec(w.shape, lambda i: (0, 0)), tab, tab, tab],
        out_specs=[row(n) for n, _ in outs],
        out_shape=[jax.ShapeDtypeStruct((r, n), t) for n, t in outs],
        compiler_params=_params("parallel"),
        name="nsa_proj",
    )(x, g, scale, shift, w, *tabs)


def _nsa_weight(w_in):
    d = w_in.shape[0]
    main = w_in[:, :1024 + 6 * 256]
    gates = w_in[:, 1024 + 6 * 256:].reshape(d, N_GROUPS, GROUP * 3)
    gates = jnp.pad(gates, ((0, 0), (0, 0), (0, LANES - GROUP * 3))).reshape(d, N_GROUPS * LANES)
    return jnp.concatenate([main, gates], axis=1).astype(BF16)


def _cmp_kernel(x_ref, pe_ref, w1_ref, w2_ref, o_ref, acc_ref):
    lc = pl.program_id(2)

    @pl.when(lc == 0)
    def _():
        acc_ref[...] = jnp.zeros_like(acc_ref)

    acc = acc_ref[...]
    for l in range(x_ref.shape[1]):
        xl = (x_ref[0, l] + pe_ref[0, l:l + 1, :]).astype(BF16)
        acc = acc + _dot(xl, w1_ref[0, l])
    acc_ref[...] = acc

    @pl.when(lc == pl.num_programs(2) - 1)
    def _():
        hid = acc * jax.nn.sigmoid(acc)
        o_ref[0, 0] = _dot(hid.astype(BF16), w2_ref[0]).astype(BF16)


def _block_diag(w, n):
    eye = jnp.eye(n, dtype=w.dtype)
    out = jnp.einsum("ij,...ab->...iajb", eye, w)
    return out.reshape(w.shape[:-2] + (n * w.shape[-2], n * w.shape[-1]))


def _cmp_weights(pe, w1, w2):
    pe_t = jnp.tile(pe, (1, 1, N_GROUPS))
    w1_bd = _block_diag(w1, N_GROUPS).astype(BF16)
    w2_dup = jnp.concatenate([w2, w2], axis=-1)
    w2_bd = _block_diag(w2_dup, N_GROUPS).astype(BF16)
    return pe_t, w1_bd, w2_bd


def _compress_prompt(rows, pe_t, w1_bd, w2_bd, b, s):
    nb = s // BLOCK
    gd = N_GROUPS * HEAD_DIM
    xt = rows.reshape(b, nb, BLOCK, -1)[..., :2 * gd].transpose(0, 2, 1, 3)
    lstep = 8
    return pl.pallas_call(
        _cmp_kernel,
        grid=(b, 2, BLOCK // lstep),
        in_specs=[pl.BlockSpec((1, lstep, nb, gd), lambda i, kv, lc: (i, lc, 0, kv)),
                  pl.BlockSpec((1, lstep, gd), lambda i, kv, lc: (kv, lc, 0)),
                  pl.BlockSpec((1, lstep, gd, gd), lambda i, kv, lc: (kv, lc, 0, 0)),
                  pl.BlockSpec((1, gd, N_GROUPS * LANES), lambda i, kv, lc: (kv, 0, 0))],
        out_specs=pl.BlockSpec((1, 1, nb, N_GROUPS * LANES), lambda i, kv, lc: (i, kv, 0, 0)),
        out_shape=jax.ShapeDtypeStruct((b, 2, nb, N_GROUPS * LANES), BF16),
        scratch_shapes=[pltpu.VMEM((nb, gd), F32)],
        compiler_params=_params("parallel", "parallel", "arbitrary"),
        name="nsa_compress",
    )(xt, pe_t, w1_bd, w2_bd)


def _softmax_rows(s):
    m = jnp.max(s, axis=-1, keepdims=True)
    p = jnp.exp(s - m)
    return p, jnp.sum(p, axis=-1, keepdims=True)


def _top_rows(v, n_sel):
    nb = v.shape[0]
    n = lax.broadcasted_iota(jnp.int32, v.shape, 0)
    sel = jnp.zeros(v.shape, F32)
    for _ in range(n_sel):
        mx = jnp.max(v, axis=0, keepdims=True)
        idx = jnp.min(jnp.where(v == mx, n, nb), axis=0, keepdims=True)
        hit = n == idx
        sel = jnp.where(hit, 1.0, sel)
        v = jnp.where(hit, -jnp.inf, v)
    return sel


def _top_blocks(v, n_sel):
    rows, nb = v.shape
    pad = -rows % LANES
    if pad:
        v = jnp.concatenate([v, jnp.zeros((pad, nb), v.dtype)], axis=0)
    return _top_rows(v.T, n_sel).T[:rows]


def _stack_heads(ref):
    q = ref.shape[1]
    lo = lax.broadcasted_iota(jnp.int32, (q, LANES), 1) < HEAD_DIM
    parts = []
    for r in range(GROUP):
        pair = ref[0, :, (r // 2) * LANES:(r // 2 + 1) * LANES]
        parts.append(jnp.where(lo if r % 2 == 0 else jnp.logical_not(lo), pair, jnp.zeros_like(pair)))
    return jnp.concatenate(parts, axis=0)


def _nsa_attn_kernel(qc_ref, qr_ref, kc_ref, vc_ref, ks_ref, vs_ref, kw_ref, vw_ref, e_ref, gt_ref, o_ref,
                     *, tk, wlen):
    nq = Q_BLOCK
    s0 = pl.program_id(2) * nq
    t_q = s0 + lax.broadcasted_iota(jnp.int32, (nq, 1), 0)
    rep = lambda a: jnp.concatenate([a] * GROUP, axis=0)

    qc = _stack_heads(qc_ref)
    kc = kc_ref[0, 0]
    nb = kc.shape[0]
    n = lax.broadcasted_iota(jnp.int32, (nq, nb), 1)
    cmask = rep(n * BLOCK + (BLOCK - 1) <= t_q)
    sc = jnp.where(cmask, _dot_nt(qc, kc), NEG)
    e = jnp.where(cmask, jnp.exp(sc - jnp.max(sc, axis=-1, keepdims=True)), 0.0)
    l = jnp.sum(e, axis=-1, keepdims=True)
    pc = e / jnp.where(l > 0.0, l, 1.0)
    o_c = _dot(pc.astype(BF16), vc_ref[0, 0])

    imp = pc[0:nq] + pc[nq:2 * nq] + pc[2 * nq:3 * nq] + pc[3 * nq:4 * nq]
    cur = t_q // BLOCK
    forced = (n == 0) | (n == cur) | (n == cur - 1)
    future = n * BLOCK > t_q
    sel = _top_blocks(jnp.where(forced, jnp.inf, jnp.where(future, -jnp.inf, imp)), min(N_SELECT, nb))
    selb = sel.astype(BF16)

    qr = _stack_heads(qr_ref)
    w0 = pl.multiple_of(jnp.maximum(s0 + nq - wlen, 0), nq)
    kpos = w0 + lax.broadcasted_iota(jnp.int32, (nq, wlen), 1)
    dpos = t_q - kpos
    wmask = rep((dpos >= 0) & (dpos < WINDOW))
    sw = jnp.where(wmask, _dot_nt(qr, kw_ref[0, pl.ds(w0, wlen), :]), NEG)
    pw, lw = _softmax_rows(sw)
    o_w = _dot(pw.astype(BF16), vw_ref[0, pl.ds(w0, wlen), :]) / lw

    col = lax.broadcasted_iota(jnp.int32, (nq, tk), 1)
    half = GROUP * nq // 2
    q_halves = (qr[:half], qr[half:])

    def body(j, carry):
        k0 = pl.multiple_of(j * tk, tk)
        k = ks_ref[0, pl.ds(k0, tk), :]
        v = vs_ref[0, pl.ds(k0, tk), :]
        ok = (_dot(selb, e_ref[j]) > 0.5) & (k0 + col <= t_q)
        ok2 = jnp.concatenate([ok, ok], axis=0)
        out = []
        for q_h, (m_i, l_i, acc) in zip(q_halves, carry):
            s = jnp.where(ok2, _dot_nt(q_h, k), NEG)
            m_n = jnp.maximum(m_i, jnp.max(s, axis=-1, keepdims=True))
            alpha = jnp.exp(m_i - m_n)
            p = jnp.exp(s - m_n)
            l_n = alpha * l_i + jnp.sum(p, axis=-1, keepdims=True)
            out.append((m_n, l_n, alpha * acc + _dot(p.astype(BF16), v)))
        return tuple(out)

    init = (jnp.full((half, 1), NEG, F32), jnp.zeros((half, 1), F32), jnp.zeros((half, LANES), F32))
    (_, l_a, acc_a), (_, l_b, acc_b) = lax.fori_loop(0, (s0 + nq + tk - 1) // tk, body, (init, init))
    o_s = jnp.concatenate([acc_a / l_a, acc_b / l_b], axis=0)

    gt = gt_ref[0]
    lo = lax.broadcasted_iota(jnp.int32, (nq, LANES), 1) < HEAD_DIM
    outs = []
    for r in range(GROUP):
        sl = slice(r * nq, (r + 1) * nq)
        outs.append(gt[:, 3 * r:3 * r + 1] * o_c[sl] + gt[:, 3 * r + 1:3 * r + 2] * o_s[sl]
                    + gt[:, 3 * r + 2:3 * r + 3] * o_w[sl])
    for pr in range(GROUP // 2):
        o_ref[0, :, pr * LANES:(pr + 1) * LANES] = jnp.where(lo, outs[2 * pr], outs[2 * pr + 1]).astype(BF16)


def _nsa_attention(qc, qr, cdup, dup, gates, b, s):
    nb = s // BLOCK
    tk = min(512, s)
    wlen = min(WINDOW + Q_BLOCK, s)
    blk = (jnp.arange(s) // BLOCK).reshape(s // tk, 1, tk)
    expand = (jnp.arange(nb)[None, :, None] == blk).astype(BF16)
    g4 = N_GROUPS
    qspec = pl.BlockSpec((1, Q_BLOCK, 2 * LANES), lambda i, g, q: (i, q, g))
    dspec = lambda kind: pl.BlockSpec((1, s, LANES), lambda i, g, q: (i, 0, kind * g4 + g))
    return pl.pallas_call(
        functools.partial(_nsa_attn_kernel, tk=tk, wlen=wlen),
        grid=(b, N_GROUPS, s // Q_BLOCK),
        in_specs=[qspec, qspec,
                  pl.BlockSpec((1, 1, nb, LANES), lambda i, g, q: (i, 0, 0, g)),
                  pl.BlockSpec((1, 1, nb, LANES), lambda i, g, q: (i, 1, 0, g)),
                  dspec(0), dspec(1), dspec(2), dspec(3),
                  pl.BlockSpec(expand.shape, lambda i, g, q: (0, 0, 0)),
                  pl.BlockSpec((1, Q_BLOCK, LANES), lambda i, g, q: (i, q, g))],
        out_specs=qspec,
        out_shape=jax.ShapeDtypeStruct((b, s, 1024), BF16),
        compiler_params=_params("parallel", "parallel", "arbitrary"),
        name="nsa_attention",
    )(qc.reshape(b, s, -1), qr.reshape(b, s, -1), cdup, cdup,
      dup.reshape(b, s, -1), dup.reshape(b, s, -1), dup.reshape(b, s, -1), dup.reshape(b, s, -1),
      expand, gates.reshape(b, s, -1))


NSA_TK = 1024


def _nsa_attn_t_kernel(qc_ref, qr_ref, kc_ref, vc_ref, ks_ref, vs_ref, kw_ref, vw_ref, e_ref, gt_ref, o_ref,
                       *, tk, wlen):
    nq = Q_BLOCK
    ncol = GROUP * nq
    s0 = pl.program_id(2) * nq
    t_q = s0 + lax.broadcasted_iota(jnp.int32, (1, nq), 1)
    t_col = jnp.concatenate([t_q] * GROUP, axis=1)
    qc, qr = qc_ref[0, 0], qr_ref[0, 0]

    kc = kc_ref[0, 0]
    nb = kc.shape[0]
    cmask = lax.broadcasted_iota(jnp.int32, (nb, 1), 0) * BLOCK + (BLOCK - 1) <= t_col
    sc = jnp.where(cmask, _dot(kc, qc), NEG)
    e = jnp.where(cmask, jnp.exp(sc - jnp.max(sc, axis=0, keepdims=True)), 0.0)
    l = jnp.sum(e, axis=0, keepdims=True)
    pc = e / jnp.where(l > 0.0, l, 1.0)
    o_c = _dot(vc_ref[0, 0], pc.astype(BF16))

    imp = pc[:, 0:nq]
    for r in range(1, GROUP):
        imp = imp + pc[:, r * nq:(r + 1) * nq]
    n = lax.broadcasted_iota(jnp.int32, (nb, nq), 0)
    cur = t_q // BLOCK
    forced = (n == 0) | (n == cur) | (n == cur - 1)
    future = n * BLOCK > t_q
    sel = _top_rows(jnp.where(forced, jnp.inf, jnp.where(future, -jnp.inf, imp)), min(N_SELECT, nb))
    selb = sel.astype(BF16)

    w0 = pl.multiple_of(jnp.maximum(s0 + nq - wlen, 0), nq)
    dpos = t_col - (w0 + lax.broadcasted_iota(jnp.int32, (wlen, 1), 0))
    sw = jnp.where((dpos >= 0) & (dpos < WINDOW), _dot(kw_ref[0, pl.ds(w0, wlen), :], qr), NEG)
    pw = jnp.exp(sw - jnp.max(sw, axis=0, keepdims=True))
    lw = jnp.sum(pw, axis=0, keepdims=True)
    pwb = pw.astype(BF16)
    c0 = w0 // nq
    o_w = _dot(vw_ref[0, 0, c0], pwb[0:nq])
    for c in range(1, wlen // nq):
        o_w = o_w + _dot(vw_ref[0, 0, c0 + c], pwb[c * nq:(c + 1) * nq])
    o_w = o_w / lw

    sub = lax.broadcasted_iota(jnp.int32, (tk, 1), 0)

    def body(j, carry):
        m_i, l_i, acc = carry
        k0 = pl.multiple_of(j * tk, tk)
        ok = (_dot(e_ref[j], selb) > 0.5) & (k0 + sub <= t_q)
        s = jnp.where(jnp.concatenate([ok] * GROUP, axis=1), _dot(ks_ref[0, pl.ds(k0, tk), :], qr), NEG)
        m_n = jnp.maximum(m_i, jnp.max(s, axis=0, keepdims=True))
        alpha = jnp.exp(m_i - m_n)
        p = jnp.exp(s - m_n)
        l_n = alpha * l_i + jnp.sum(p, axis=0, keepdims=True)
        return m_n, l_n, alpha * acc + _dot(vs_ref[0, 0, j], p.astype(BF16))

    init = (jnp.full((1, ncol), NEG, F32), jnp.zeros((1, ncol), F32), jnp.zeros((LANES, ncol), F32))
    _, l_s, acc_s = lax.fori_loop(0, (s0 + nq + tk - 1) // tk, body, init)

    gt = gt_ref[0, 0]
    out = gt[0:1] * o_c + gt[1:2] * (acc_s / l_s) + gt[2:3] * o_w
    o_ref[0, 0] = out[0:HEAD_DIM].astype(BF16)


def _nsa_attention_t(qc, qr, cdup, dup, gates, b, s):
    nb = s // BLOCK
    tk = min(NSA_TK, s)
    nt = s // tk
    wlen = min(WINDOW + Q_BLOCK, s)
    nqb = s // Q_BLOCK
    g4 = N_GROUPS
    ncol = GROUP * Q_BLOCK

    def q_t(q):
        q6 = q.reshape(b, nqb, Q_BLOCK, g4, GROUP, HEAD_DIM).transpose(0, 3, 5, 1, 4, 2)
        q6 = q6.reshape(b, g4, HEAD_DIM, nqb * ncol)
        return jnp.pad(q6, ((0, 0), (0, 0), (0, LANES - HEAD_DIM), (0, 0)))

    d4 = dup.reshape(b, s, 4, g4, LANES)
    vc_t = cdup[:, 1].reshape(b, nb, g4, LANES).transpose(0, 2, 3, 1)
    vs_t = d4[:, :, 1].reshape(b, nt, tk, g4, LANES).transpose(0, 3, 1, 4, 2)
    vw_t = d4[:, :, 3].reshape(b, nqb, Q_BLOCK, g4, LANES).transpose(0, 3, 1, 4, 2)
    blk = (jnp.arange(s) // BLOCK).reshape(nt, tk, 1)
    expand = (jnp.arange(nb)[None, None, :] == blk).astype(BF16)
    g_t = gates.reshape(b, nqb, Q_BLOCK, g4, LANES)[..., :GROUP * 3].reshape(b, nqb, Q_BLOCK, g4, GROUP, 3)
    g_t = g_t.transpose(0, 3, 5, 1, 4, 2).reshape(b, g4, 3, nqb * ncol)
    g_t = jnp.pad(g_t, ((0, 0), (0, 0), (0, 5), (0, 0)))

    qspec = pl.BlockSpec((1, 1, LANES, ncol), lambda i, g, q: (i, g, 0, q))
    dspec = lambda kind: pl.BlockSpec((1, s, LANES), lambda i, g, q: (i, 0, kind * g4 + g))
    o_t = pl.pallas_call(
        functools.partial(_nsa_attn_t_kernel, tk=tk, wlen=wlen),
        grid=(b, g4, nqb),
        in_specs=[qspec, qspec,
                  pl.BlockSpec((1, 1, nb, LANES), lambda i, g, q: (i, 0, 0, g)),
                  pl.BlockSpec((1, 1, LANES, nb), lambda i, g, q: (i, g, 0, 0)),
                  dspec(0), pl.BlockSpec((1, 1, nt, LANES, tk), lambda i, g, q: (i, g, 0, 0, 0)),
                  dspec(2), pl.BlockSpec((1, 1, nqb, LANES, Q_BLOCK), lambda i, g, q: (i, g, 0, 0, 0)),
                  pl.BlockSpec(expand.shape, lambda i, g, q: (0, 0, 0)),
                  pl.BlockSpec((1, 1, 8, ncol), lambda i, g, q: (i, g, 0, q))],
        out_specs=pl.BlockSpec((1, 1, HEAD_DIM, ncol), lambda i, g, q: (i, g, 0, q)),
        out_shape=jax.ShapeDtypeStruct((b, g4, HEAD_DIM, nqb * ncol), BF16),
        compiler_params=_params("parallel", "parallel", "arbitrary"),
        name="nsa_attention",
    )(q_t(qc), q_t(qr), cdup, vc_t, dup.reshape(b, s, -1), vs_t, dup.reshape(b, s, -1), vw_t, expand, g_t)
    o6 = o_t.reshape(b, g4, HEAD_DIM, nqb, GROUP, Q_BLOCK).transpose(0, 3, 5, 1, 4, 2)
    return o6.reshape(b * s, g4 * GROUP * HEAD_DIM)


FFN_CHUNK = 256


def _ffn_kernel(*refs, u, tpb, chain, final, gated, nf):
    (x_ref, a_ref, wo_ref, gm_ref, g_ref, sc_ref, sh_ref, gf_ref, wu_ref, cw_ref, cb_ref, wd_ref) = refs[:12]
    k = 12
    og_ref = st_ref = gfin_ref = y_ref = carry_scr = None
    if gated:
        og_ref = refs[k]
        k += 1
    if not chain:
        st_ref = refs[k]
        k += 1
    if final:
        gfin_ref = refs[k]
        k += 1
    xo_ref, tail_ref = refs[k], refs[k + 1]
    k += 2
    if final:
        y_ref = refs[k]
        k += 1
    ext_scr = refs[k]
    if chain:
        carry_scr = refs[k + 1]

    fc = FFN_CHUNK
    tm = x_ref.shape[0]
    base = ext_scr.shape[0] - tm
    tail = tail_ref.shape[0]
    a = (a_ref[...] * og_ref[...]).astype(BF16) if gated else a_ref[...]
    x1 = x_ref[...] + gm_ref[0] * _dot(a, wo_ref[...])
    h = _norm_mod(x1, g_ref[...], sc_ref[0], sh_ref[0]).astype(BF16)
    if chain:
        first = (pl.program_id(0) % tpb) == 0
    acc = jnp.zeros((tm, x_ref.shape[1]), F32)
    for f in range(nf):
        cs = slice(f * 2 * fc, (f + 1) * 2 * fc)
        up = _dot(h, wu_ref[:, cs])
        if chain:
            ext_scr[0:base, :] = jnp.where(first, 0.0, carry_scr[f])
            carry_scr[f] = up[tm - base:, :]
        else:
            ext_scr[0:base, :] = st_ref[:, cs]
        ext_scr[base:, :] = up
        tail_ref[:, cs] = up[tm - tail:, :]
        cw = cw_ref[:, cs]
        mixed = (cb_ref[:, cs] + cw[0:1] * ext_scr[base - 2 * u:base - 2 * u + tm, :]
                 + cw[1:2] * ext_scr[base - u:base - u + tm, :] + cw[2:3] * up)
        a, g = mixed[:, :fc], mixed[:, fc:]
        act = (g * jax.nn.sigmoid(g) * a).astype(BF16)
        acc = acc + _dot(act, wd_ref[f * fc:(f + 1) * fc, :])
    xn = x1 + gf_ref[0] * acc
    xo_ref[...] = xn
    if final:
        ms = jnp.mean(xn * xn, axis=-1, keepdims=True)
        y_ref[...] = xn * lax.rsqrt(ms + EPS) * gfin_ref[...]


def _ffn_cols(a, inverse=False):
    n = a.shape[-1]
    nf = n // (2 * FFN_CHUNK)
    mid = (nf, 2, FFN_CHUNK) if inverse else (2, nf, FFN_CHUNK)
    return a.reshape(a.shape[:-1] + mid).swapaxes(-3, -2).reshape(a.shape)


def _ffn(x, attn, w_out, mods, g, wu, cw, cb, wd, mode, tm, tpb, u, state=None, final_g=None, og=None):
    r, d = x.shape
    f2 = wu.shape[1]
    nf = f2 // (2 * FFN_CHUNK)
    chain = state is None
    final = final_g is not None
    gated = og is not None
    base = 8 if chain else 2 * u
    tail = 8 if chain else 2 * u
    mod, _ = _row_specs(mode, tm, tpb, d)
    row = lambda n: pl.BlockSpec((tm, n), lambda i: (i, 0))
    full = lambda a: pl.BlockSpec(a.shape, lambda i: (0,) * a.ndim, pipeline_mode=pl.Buffered(1))
    args = [x, attn, w_out, mods[0], g, mods[1], mods[2], mods[3], wu, cw, cb, wd]
    specs = [row(d), row(d), full(w_out), mod, full(g), mod, mod, mod, full(wu), full(cw), full(cb), full(wd)]
    if gated:
        args.append(og)
        specs.append(row(d))
    if not chain:
        args.append(state)
        specs.append(full(state))
    if final:
        args.append(final_g)
        specs.append(full(final_g))
    out_specs = [row(d), pl.BlockSpec((tail, f2), lambda i: (i, 0))]
    out_shape = [jax.ShapeDtypeStruct((r, d), F32), jax.ShapeDtypeStruct((r // tm * tail, f2), F32)]
    if final:
        out_specs.append(row(d))
        out_shape.append(jax.ShapeDtypeStruct((r, d), F32))
    scratch = [pltpu.VMEM((base + tm, 2 * FFN_CHUNK), F32)]
    if chain:
        scratch.append(pltpu.VMEM((nf, base, 2 * FFN_CHUNK), F32))
    return pl.pallas_call(
        functools.partial(_ffn_kernel, u=u, tpb=tpb, chain=chain, final=final, gated=gated, nf=nf),
        grid=(r // tm,),
        in_specs=specs, out_specs=out_specs, out_shape=out_shape, scratch_shapes=scratch,
        compiler_params=_params("arbitrary"),
        name="out_proj_ffn",
    )(*args)


FOX_W_COLS = 4 * 1024 + LANES


def _fox_proj_kernel(x_ref, g_ref, sc_ref, sh_ref, w_ref, ind_ref, indt_ref, gq_ref, gk_ref, bf_ref,
                     q_ref, kf_ref, kb_ref, vf_ref, vb_ref, og_ref, lf_ref, c_ref, carry_scr, *, tpb):
    h = _norm_mod(x_ref[...], g_ref[...], sc_ref[0], sh_ref[0]).astype(BF16)
    z = _dot(h, w_ref[...])
    tm = z.shape[0]
    ind, indt = ind_ref[...], indt_ref[...]

    def head_norm(zc, gain):
        sq = zc * zc
        hi = sq.astype(BF16)
        lo = (sq - hi.astype(F32)).astype(BF16)
        ms = (_dot(hi, ind) + _dot(lo, ind)) * (1.0 / HEAD_DIM)
        rinv = lax.rsqrt(ms + EPS)
        rh = rinv.astype(BF16)
        rl = (rinv - rh.astype(F32)).astype(BF16)
        return zc * (_dot(rh, indt) + _dot(rl, indt)) * gain

    q_ref[...] = (head_norm(z[:, 0:1024], gq_ref[...]) * SCALE).astype(BF16)
    kn = head_norm(z[:, 1024:2048], gk_ref[...])
    kf_ref[...] = kn
    kb_ref[...] = kn.astype(BF16)
    v = z[:, 2048:3072]
    vf_ref[...] = v
    vb_ref[...] = v.astype(BF16)
    og_ref[...] = jax.nn.sigmoid(z[:, 3072:4096])
    zf = z[:, 4096:4096 + LANES] + bf_ref[...]
    lf = jnp.minimum(zf, 0.0) - jnp.log1p(jnp.exp(-jnp.abs(zf)))
    lf = jnp.where(lax.broadcasted_iota(jnp.int32, lf.shape, 1) < N_HEADS, lf, 0.0)
    lf_ref[...] = lf

    @pl.when(pl.program_id(0) % tpb == 0)
    def _():
        carry_scr[...] = jnp.zeros_like(carry_scr)

    tri = (lax.broadcasted_iota(jnp.int32, (tm, tm), 0) >= lax.broadcasted_iota(jnp.int32, (tm, tm), 1)).astype(BF16)
    hi, mid, lo = _split3(lf)
    c = _dot(tri, hi) + _dot(tri, mid) + _dot(tri, lo) + carry_scr[0:1, :]
    c_ref[...] = c
    carry_scr[0:1, :] = c[tm - 1:tm, :]


def _fox_weight(w_in):
    d = w_in.shape[0]
    return jnp.pad(w_in, ((0, 0), (0, FOX_W_COLS - w_in.shape[1]))).astype(BF16)


def _fox_proj(x, g, scale, shift, w, gq, gk, bf, mode, tm, tpb):
    r, d = x.shape
    mod, _ = _row_specs(mode, tm, tpb, d)
    row = lambda n: pl.BlockSpec((tm, n), lambda i: (i, 0))
    full = lambda a: pl.BlockSpec(a.shape, lambda i: (0,) * a.ndim)
    head_of = jnp.arange(1024) // HEAD_DIM
    ind = (head_of[:, None] == jnp.arange(LANES)[None, :]).astype(BF16)
    gq_t = jnp.tile(gq, N_HEADS)[None]
    gk_t = jnp.tile(gk, N_HEADS)[None]
    bf_p = jnp.pad(bf, (0, LANES - bf.shape[0]))[None]
    outs = [(1024, BF16), (1024, F32), (1024, BF16), (1024, F32), (1024, BF16), (1024, F32), (LANES, F32), (LANES, F32)]
    return pl.pallas_call(
        functools.partial(_fox_proj_kernel, tpb=tpb),
        grid=(r // tm,),
        in_specs=[row(d), full(g), mod, mod, full(w), full(ind), full(ind.T), full(gq_t), full(gk_t), full(bf_p)],
        out_specs=[row(n) for n, _ in outs],
        out_shape=[jax.ShapeDtypeStruct((r, n), t) for n, t in outs],
        scratch_shapes=[pltpu.VMEM((8, LANES), F32)],
        compiler_params=_params("arbitrary"),
        name="fox_proj",
    )(x, g, scale, shift, w, ind, ind.T, gq_t, gk_t, bf_p)


def _fox_attn_kernel(q_ref, k_ref, v_ref, nc_ref, og_ref, o_ref, *, t, tk, nt):
    qi = pl.program_id(2)
    q = q_ref[0]
    lo = lax.broadcasted_iota(jnp.int32, (t, LANES), 1) < HEAD_DIM
    t_q = qi * t + lax.broadcasted_iota(jnp.int32, (t, 1), 0)
    col = lax.broadcasted_iota(jnp.int32, (t, tk), 1)
    q_heads = (jnp.where(lo, q, jnp.zeros_like(q)), jnp.where(lo, jnp.zeros_like(q), q))

    def tile(j, carry, masked):
        k0 = pl.multiple_of(j * tk, tk)
        k = k_ref[0, pl.ds(k0, tk), :]
        v = v_ref[0, pl.ds(k0, tk), :]
        out = []
        for h2, (m_i, l_i, acc) in enumerate(carry):
            s = _dot_nt(q_heads[h2], k) + nc_ref[0, 0, h2 * nt + j]
            if masked:
                s = jnp.where(k0 + col <= t_q, s, NEG)
            m_n = jnp.maximum(m_i, jnp.max(s, axis=-1, keepdims=True))
            alpha = jnp.exp(m_i - m_n)
            p = jnp.exp(s - m_n)
            l_n = alpha * l_i + jnp.sum(p, axis=-1, keepdims=True)
            out.append((m_n, l_n, alpha * acc + _dot(p.astype(BF16), v)))
        return tuple(out)

    init = (jnp.full((t, 1), NEG, F32), jnp.zeros((t, 1), F32), jnp.zeros((t, LANES), F32))
    jd = (qi * t) // tk
    carry = lax.fori_loop(0, jd, lambda j, c: tile(j, c, False), (init, init))
    for dj in range(max(1, t // tk)):
        carry = tile(jd + dj, carry, True)
    (_, l_0, acc_0), (_, l_1, acc_1) = carry
    o_ref[0] = (jnp.where(lo, acc_0 / l_0, acc_1 / l_1) * og_ref[0]).astype(BF16)


FOX_TQ = 1024
FOX_TK = 1024


def _fox_attention(q, kb, vb, c, og, b, s):
    t, tk = min(FOX_TQ, s), min(FOX_TK, s)
    nt = s // tk
    hp = N_HEADS // 2
    negc = -c[:, :N_HEADS].reshape(b, nt, tk, hp, 2).transpose(0, 3, 4, 1, 2).reshape(b, hp, 2 * nt, 1, tk)
    qspec = pl.BlockSpec((1, t, LANES), lambda i, p, qi: (i, qi, p))
    kspec = pl.BlockSpec((1, s, LANES), lambda i, p, qi: (i, 0, p))
    return pl.pallas_call(
        functools.partial(_fox_attn_kernel, t=t, tk=tk, nt=nt),
        grid=(b, hp, s // t),
        in_specs=[qspec, kspec, kspec,
                  pl.BlockSpec((1, 1, 2 * nt, 1, tk), lambda i, p, qi: (i, p, 0, 0, 0)), qspec],
        out_specs=qspec,
        out_shape=jax.ShapeDtypeStruct((b, s, 1024), BF16),
        compiler_params=_params("parallel", "parallel", "arbitrary"),
        name="fox_attention",
    )(q.reshape(b, s, -1), kb.reshape(b, s, -1), vb.reshape(b, s, -1), negc, og.reshape(b, s, -1))


def _fox_attn_t_kernel(q_ref, k_ref, v_ref, o_ref, *, tq, tk):
    qi = pl.program_id(2)
    t_q = qi * tq + lax.broadcasted_iota(jnp.int32, (1, tq), 1)
    sub = lax.broadcasted_iota(jnp.int32, (tk, 1), 0)
    jd = (qi * tq) // tk

    def tile(j, carry, masked):
        k0 = pl.multiple_of(j * tk, tk)
        out = []
        for h, (m_i, l_i, acc) in enumerate(carry):
            s = _dot(k_ref[0, h, pl.ds(k0, tk), :], q_ref[0, h])
            if masked:
                s = jnp.where(k0 + sub <= t_q, s, NEG)
            m_n = jnp.maximum(m_i, jnp.max(s, axis=0, keepdims=True))
            alpha = jnp.exp(m_i - m_n)
            p = jnp.exp(s - m_n)
            l_n = alpha * l_i + jnp.sum(p, axis=0, keepdims=True)
            out.append((m_n, l_n, alpha * acc + _dot(v_ref[0, h, j], p.astype(BF16))))
        return tuple(out)

    init = (jnp.full((1, tq), NEG, F32), jnp.zeros((1, tq), F32), jnp.zeros((HEAD_DIM, tq), F32))
    carry = lax.fori_loop(0, jd, lambda j, c: tile(j, c, False), (init, init))
    for h, (_, l_f, acc_f) in enumerate(tile(jd, carry, True)):
        o_ref[0, h] = acc_f / l_f


def _fox_attention_t(q, kb, vb, c, b, s):
    tq, tk = min(256, s), min(512, s)
    nt = s // tk
    hs = (b, s, N_HEADS, HEAD_DIM)
    negc = _split3(-c[:, :N_HEADS].reshape(b, s, N_HEADS))
    fill = jnp.zeros(hs[:3] + (LANES - HEAD_DIM - 3,), BF16)
    k_aug = jnp.concatenate([kb.reshape(hs)] + [t[..., None] for t in negc] + [fill], axis=-1).transpose(0, 2, 1, 3)
    q_aug = jnp.concatenate([q.reshape(hs), jnp.ones(hs[:3] + (3,), BF16), fill], axis=-1).transpose(0, 2, 3, 1)
    v_t = vb.reshape(b, nt, tk, N_HEADS, HEAD_DIM).transpose(0, 3, 1, 4, 2)
    return pl.pallas_call(
        functools.partial(_fox_attn_t_kernel, tq=tq, tk=tk),
        grid=(b, N_HEADS // 2, s // tq),
        in_specs=[pl.BlockSpec((1, 2, LANES, tq), lambda i, p, qi: (i, p, 0, qi)),
                  pl.BlockSpec((1, 2, s, LANES), lambda i, p, qi: (i, p, 0, 0)),
                  pl.BlockSpec((1, 2, nt, HEAD_DIM, tk), lambda i, p, qi: (i, p, 0, 0, 0))],
        out_specs=pl.BlockSpec((1, 2, HEAD_DIM, tq), lambda i, p, qi: (i, p, 0, qi)),
        out_shape=jax.ShapeDtypeStruct((b, N_HEADS, HEAD_DIM, s), F32),
        compiler_params=_params("parallel", "parallel", "arbitrary"),
        name="fox_attention",
    )(q_aug, k_aug, v_t)


def _page_spec(shape, slot_block, n_pages):
    nd = len(shape)
    return pl.BlockSpec((1,) + shape, lambda i, p, pt: (pt[i * n_pages + p], slot_block) + (0,) * (nd - 1))


def _nsa_cmp_sample_kernel(*refs, n_pages):
    pt_ref = refs[0]
    pages = refs[1:1 + n_pages]
    pe_ref, w1_ref, w2_ref, o_ref, x_scr = refs[1 + n_pages:]
    del pt_ref
    rows = n_pages * N_GROUPS
    for j in range(n_pages):
        for slot in range(2):
            for g in range(N_GROUPS):
                r0 = (g * n_pages + j) * HEAD_DIM
                x_scr[slot, r0:r0 + HEAD_DIM, :] = pages[j][0, slot, g]
    for slot in range(2):
        acc = jnp.zeros((rows, LANES), F32)
        for dd in range(HEAD_DIM):
            xl = x_scr[slot, pl.ds(dd, rows, stride=HEAD_DIM), :] + pe_ref[slot, dd:dd + 1, :]
            acc = acc + _dot(xl.astype(BF16), w1_ref[slot, dd])
        hid = acc * jax.nn.sigmoid(acc)
        o_ref[0, slot] = _dot(hid.astype(BF16), w2_ref[slot])


def _nsa_cmp_sample(pool_t, pt, pe, w1, w2, db, n_pages):
    per = PAGE // BLOCK
    pe_t = jnp.tile(pe.transpose(0, 2, 1), (1, 1, per))
    w1_d = _block_diag(w1.transpose(0, 2, 1, 3), per).astype(BF16)
    w2_d = _block_diag(w2, per).astype(BF16)
    rows = n_pages * N_GROUPS
    page = lambda j: pl.BlockSpec((1, 2, N_GROUPS, HEAD_DIM, PAGE),
                                  lambda i, pt, j=j: (pt[i * n_pages + j], 0, 0, 0, 0))
    full = lambda a: pl.BlockSpec(a.shape, lambda i, pt: (0,) * a.ndim)
    return pl.pallas_call(
        functools.partial(_nsa_cmp_sample_kernel, n_pages=n_pages),
        grid_spec=pltpu.PrefetchScalarGridSpec(
            num_scalar_prefetch=1, grid=(db,),
            in_specs=[page(j) for j in range(n_pages)] + [full(pe_t), full(w1_d), full(w2_d)],
            out_specs=pl.BlockSpec((1, 2, rows, per * HEAD_DIM), lambda i, pt: (i, 0, 0, 0)),
            scratch_shapes=[pltpu.VMEM((2, rows * HEAD_DIM, PAGE), F32)]),
        out_shape=jax.ShapeDtypeStruct((db, 2, rows, per * HEAD_DIM), F32),
        compiler_params=_params("arbitrary"),
        name="nsa_compress_sample",
    )(pt, *([pool_t] * n_pages), pe_t, w1_d, w2_d)


def _nsa_sel_sample_kernel(qc_ref, qr_ref, kc_ref, vc_ref, win_ref, kn_ref, vn_ref,
                           oc_ref, ow_ref, sel_ref, *, past, dt, nbs):
    nr = qc_ref.shape[1]
    per_r = dt * N_GROUPS
    row = lax.broadcasted_iota(jnp.int32, (nr, 1), 0)
    t_pos = past + (row // N_GROUPS) % dt

    kc = kc_ref[0]
    nb = kc.shape[0]
    n = lax.broadcasted_iota(jnp.int32, (nr, nb), 1)
    cmask = n * BLOCK + (BLOCK - 1) <= t_pos
    sc = jnp.where(cmask, _dot_nt(qc_ref[0], kc), NEG)
    e = jnp.where(cmask, jnp.exp(sc - jnp.max(sc, axis=-1, keepdims=True)), 0.0)
    l = jnp.sum(e, axis=-1, keepdims=True)
    pc = e / jnp.where(l > 0.0, l, 1.0)
    oc_ref[0] = _dot(pc.astype(BF16), vc_ref[0])

    imp = pc[0:per_r]
    for r in range(1, GROUP):
        imp = imp + pc[r * per_r:(r + 1) * per_r]
    n1 = lax.broadcasted_iota(jnp.int32, (per_r, nb), 1)
    t1 = past + lax.broadcasted_iota(jnp.int32, (per_r, 1), 0) // N_GROUPS
    cur = t1 // BLOCK
    forced = (n1 == 0) | (n1 == cur) | (n1 == cur - 1)
    dead = (n1 * BLOCK > t1) | (n1 >= nbs)
    sel = _top_blocks(jnp.where(forced, jnp.inf, jnp.where(dead, -jnp.inf, imp)), min(N_SELECT, nbs))
    sel_ref[0] = jnp.concatenate([sel] * GROUP, axis=0).astype(BF16)

    qr = qr_ref[0]
    wb = win_ref.shape[-1]
    kw = win_ref[0, 0].reshape(N_GROUPS * HEAD_DIM, wb).astype(BF16)
    vw = win_ref[0, 1].reshape(N_GROUPS * HEAD_DIM, wb).astype(BF16)
    d_old = t_pos - (past - wb + lax.broadcasted_iota(jnp.int32, (nr, wb), 1))
    s_old = jnp.where((d_old >= 0) & (d_old < WINDOW), _dot(qr, kw), NEG)
    tn = lax.broadcasted_iota(jnp.int32, (nr, LANES), 1)
    d_new = t_pos - (past + tn)
    s_new = jnp.where((d_new >= 0) & (tn < dt), _dot(qr, kn_ref[0]), NEG)
    m = jnp.maximum(jnp.max(s_old, axis=-1, keepdims=True), jnp.max(s_new, axis=-1, keepdims=True))
    p_old = jnp.exp(s_old - m)
    p_new = jnp.exp(s_new - m)
    lw = jnp.sum(p_old, axis=-1, keepdims=True) + jnp.sum(p_new, axis=-1, keepdims=True)
    ow_ref[0] = (_dot_nt(p_old.astype(BF16), vw) + _dot_nt(p_new.astype(BF16), vn_ref[0])) / lw


def _nsa_sel_sample(qbd_c, qbd_r, kc, vc, win_t, kn_t, vn_t, past, dt, nbs):
    db, nr, gd = qbd_c.shape
    blk = lambda a: pl.BlockSpec((1,) + a.shape[1:], lambda i: (i,) + (0,) * (a.ndim - 1))
    args = (qbd_c, qbd_r, kc, vc, win_t, kn_t, vn_t)
    return pl.pallas_call(
        functools.partial(_nsa_sel_sample_kernel, past=past, dt=dt, nbs=nbs),
        grid=(db,),
        in_specs=[blk(a) for a in args],
        out_specs=[pl.BlockSpec((1, nr, gd), lambda i: (i, 0, 0)), pl.BlockSpec((1, nr, gd), lambda i: (i, 0, 0)),
                   pl.BlockSpec((1, nr, LANES), lambda i: (i, 0, 0))],
        out_shape=[jax.ShapeDtypeStruct((db, nr, gd), F32), jax.ShapeDtypeStruct((db, nr, gd), F32),
                   jax.ShapeDtypeStruct((db, nr, LANES), BF16)],
        compiler_params=_params("parallel"),
        name="nsa_select_window_sample",
    )(*args)


def _online_update(s, v_t, m_scr, l_scr, acc_scr):
    m_i = m_scr[...]
    m_n = jnp.maximum(m_i, jnp.max(s, axis=-1, keepdims=True))
    alpha = jnp.exp(m_i - m_n)
    p = jnp.exp(s - m_n)
    l_scr[...] = alpha * l_scr[...] + jnp.sum(p, axis=-1, keepdims=True)
    acc_scr[...] = alpha * acc_scr[...] + _dot_nt(p.astype(BF16), v_t)
    m_scr[...] = m_n


def _nsa_slc_sample_kernel(pt_ref, page_ref, q_ref, sel_ref, e_ref, en_ref, kn_ref, vn_ref, oc_ref, ow_ref, gt_ref,
                           o_ref, m_scr, l_scr, acc_scr, *, past, dt):
    del pt_ref
    p = pl.program_id(1)
    gd = N_GROUPS * HEAD_DIM

    @pl.when(p == 0)
    def _():
        m_scr[...] = jnp.full(m_scr.shape, NEG, F32)
        l_scr[...] = jnp.zeros_like(l_scr)
        acc_scr[...] = jnp.zeros_like(acc_scr)

    q = q_ref[0]
    sel = sel_ref[0]
    k_t = page_ref[0, 0].reshape(gd, PAGE).astype(BF16)
    v_t = page_ref[0, 1].reshape(gd, PAGE).astype(BF16)
    s = jnp.where(_dot(sel, e_ref[0]) > 0.5, _dot(q, k_t), NEG)
    _online_update(s, v_t, m_scr, l_scr, acc_scr)

    @pl.when(p == pl.num_programs(1) - 1)
    def _():
        nr = q.shape[0]
        row = lax.broadcasted_iota(jnp.int32, (nr, 1), 0)
        t_row = (row // N_GROUPS) % dt
        tn = lax.broadcasted_iota(jnp.int32, (nr, LANES), 1)
        ok = (_dot(sel, en_ref[...]) > 0.5) & (tn <= t_row) & (tn < dt)
        _online_update(jnp.where(ok, _dot(q, kn_ref[0]), NEG), vn_ref[0], m_scr, l_scr, acc_scr)
        gt = gt_ref[0]
        o_ref[0] = (gt[:, 0:1] * oc_ref[0] + gt[:, 1:2] * (acc_scr[...] / l_scr[...]) + gt[:, 2:3] * ow_ref[0])


def _nsa_slc_sample(pool_t, pt, qbd_r, sel, kn_t, vn_t, o_c, o_w, gates, past, dt, n_pages):
    db, nr, gd = qbd_r.shape
    per = PAGE // BLOCK
    key_blk = jnp.arange(n_pages * PAGE) // BLOCK
    e_tab = (jnp.arange(LANES)[None, :, None] == key_blk.reshape(n_pages, 1, PAGE)).astype(BF16)
    new_blk = jnp.where(jnp.arange(LANES) < dt, (past + jnp.arange(LANES)) // BLOCK, -1)
    e_new = (jnp.arange(LANES)[:, None] == new_blk[None, :]).astype(BF16)
    del per
    blk = lambda a: pl.BlockSpec((1,) + a.shape[1:], lambda i, p, pt: (i,) + (0,) * (a.ndim - 1))
    return pl.pallas_call(
        functools.partial(_nsa_slc_sample_kernel, past=past, dt=dt),
        grid_spec=pltpu.PrefetchScalarGridSpec(
            num_scalar_prefetch=1, grid=(db, n_pages),
            in_specs=[_page_spec((2, N_GROUPS, HEAD_DIM, PAGE), 1, n_pages), blk(qbd_r), blk(sel),
                      pl.BlockSpec((1, LANES, PAGE), lambda i, p, pt: (p, 0, 0)),
                      pl.BlockSpec(e_new.shape, lambda i, p, pt: (0, 0)),
                      blk(kn_t), blk(vn_t), blk(o_c), blk(o_w), blk(gates)],
            out_specs=pl.BlockSpec((1, nr, gd), lambda i, p, pt: (i, 0, 0)),
            scratch_shapes=[pltpu.VMEM((nr, 1), F32), pltpu.VMEM((nr, 1), F32), pltpu.VMEM((nr, gd), F32)]),
        out_shape=jax.ShapeDtypeStruct((db, nr, gd), F32),
        compiler_params=_params("parallel", "arbitrary"),
        name="nsa_selected_sample",
    )(pt, pool_t, qbd_r, sel, e_tab, e_new, kn_t, vn_t, o_c, o_w, gates)


def _fox_sample_kernel(pt_ref, page_ref, lf_ref, q_ref, kn_ref, vn_ref, lfn_ref, og_ref,
                       o_ref, m_scr, l_scr, acc_scr, c_scr, *, dt):
    del pt_ref
    p = pl.program_id(1)
    hd = N_HEADS * HEAD_DIM

    @pl.when(p == 0)
    def _():
        m_scr[...] = jnp.full(m_scr.shape, NEG, F32)
        l_scr[...] = jnp.zeros_like(l_scr)
        acc_scr[...] = jnp.zeros_like(acc_scr)
        c_scr[...] = jnp.zeros_like(c_scr)

    tri = (lax.broadcasted_iota(jnp.int32, (PAGE, PAGE), 0) <= lax.broadcasted_iota(jnp.int32, (PAGE, PAGE), 1)).astype(BF16)

    def cum(lf):
        hi, mid, lo = _split3(lf)
        return _dot(hi, tri) + _dot(mid, tri) + _dot(lo, tri) + c_scr[...]

    q = q_ref[0]
    c_page = cum(lf_ref[0])
    k_t = page_ref[0, 0].reshape(hd, PAGE).astype(BF16)
    v_t = page_ref[0, 1].reshape(hd, PAGE).astype(BF16)
    s = _dot(q, k_t) - jnp.concatenate([c_page] * dt, axis=0)
    _online_update(s, v_t, m_scr, l_scr, acc_scr)
    c_scr[...] = jnp.broadcast_to(c_page[:, PAGE - 1:PAGE], c_scr.shape)

    @pl.when(p == pl.num_programs(1) - 1)
    def _():
        nr = q.shape[0]
        t_row = lax.broadcasted_iota(jnp.int32, (nr, 1), 0) // N_HEADS
        tn = lax.broadcasted_iota(jnp.int32, (nr, LANES), 1)
        c_new = cum(lfn_ref[0])
        s_n = _dot(q, kn_ref[0]) - jnp.concatenate([c_new] * dt, axis=0)
        s_n = jnp.where((tn <= t_row) & (tn < dt), s_n, NEG)
        _online_update(s_n, vn_ref[0], m_scr, l_scr, acc_scr)
        o_ref[0] = acc_scr[...] / l_scr[...] * og_ref[0]


def _fox_sample(pool_t, lf_pool_t, pt, qbd, kn_t, vn_t, lfn_t, og_t, dt, n_pages):
    db, nr, hd = qbd.shape
    blk = lambda a: pl.BlockSpec((1,) + a.shape[1:], lambda i, p, pt: (i,) + (0,) * (a.ndim - 1))
    return pl.pallas_call(
        functools.partial(_fox_sample_kernel, dt=dt),
        grid_spec=pltpu.PrefetchScalarGridSpec(
            num_scalar_prefetch=1, grid=(db, n_pages),
            in_specs=[_page_spec((2, N_HEADS, HEAD_DIM, PAGE), 0, n_pages),
                      pl.BlockSpec((1, N_HEADS, PAGE), lambda i, p, pt: (pt[i * n_pages + p], 0, 0)),
                      blk(qbd), blk(kn_t), blk(vn_t), blk(lfn_t), blk(og_t)],
            out_specs=pl.BlockSpec((1, nr, hd), lambda i, p, pt: (i, 0, 0)),
            scratch_shapes=[pltpu.VMEM((nr, 1), F32), pltpu.VMEM((nr, 1), F32), pltpu.VMEM((nr, hd), F32),
                            pltpu.VMEM((N_HEADS, PAGE), F32)]),
        out_shape=jax.ShapeDtypeStruct((db, nr, hd), F32),
        compiler_params=_params("parallel", "arbitrary"),
        name="fox_attention_sample",
    )(pt, pool_t, lf_pool_t, qbd, kn_t, vn_t, lfn_t, og_t)


def _page_specs(block, slot_block, n_pages):
    nd = len(block)
    return [pl.BlockSpec((1,) + block, lambda i, pt, j=j: (pt[i * n_pages + j], slot_block) + (0,) * (nd - 1))
            for j in range(n_pages)]


def _softmax_chunks(scores):
    m = scores[0]
    for s in scores[1:]:
        m = jnp.maximum(m, s)
    m = jnp.max(m, axis=-1, keepdims=True)
    ps = [jnp.exp(s - m) for s in scores]
    tot = ps[0]
    for p in ps[1:]:
        tot = tot + p
    return ps, jnp.sum(tot, axis=-1, keepdims=True)


def _nsa_attn_sample_kernel(*refs, n_pages, past, dt, nbs):
    pages = refs[1:1 + n_pages]
    (qc_ref, qr_ref, kc_ref, vc_ref, win_ref, kwn_ref, vwn_ref, ksn_ref, vsn_ref, e_ref, en_ref, gt_ref,
     o_ref) = refs[1 + n_pages:]
    gd = N_GROUPS * HEAD_DIM
    nr = qc_ref.shape[1]
    per_r = dt * N_GROUPS
    row = lax.broadcasted_iota(jnp.int32, (nr, 1), 0)
    t_row = (row // N_GROUPS) % dt
    t_pos = past + t_row
    tn = lax.broadcasted_iota(jnp.int32, (nr, LANES), 1)

    kc = kc_ref[0]
    nb = kc.shape[0]
    n = lax.broadcasted_iota(jnp.int32, (nr, nb), 1)
    cmask = n * BLOCK + (BLOCK - 1) <= t_pos
    sc = jnp.where(cmask, _dot_nt(qc_ref[0], kc), NEG)
    e = jnp.where(cmask, jnp.exp(sc - jnp.max(sc, axis=-1, keepdims=True)), 0.0)
    l = jnp.sum(e, axis=-1, keepdims=True)
    pc = e / jnp.where(l > 0.0, l, 1.0)
    o_c = _dot(pc.astype(BF16), vc_ref[0])

    imp = pc[0:per_r]
    for r in range(1, GROUP):
        imp = imp + pc[r * per_r:(r + 1) * per_r]
    n1 = lax.broadcasted_iota(jnp.int32, (per_r, nb), 1)
    t1 = past + lax.broadcasted_iota(jnp.int32, (per_r, 1), 0) // N_GROUPS
    cur = t1 // BLOCK
    forced = (n1 == 0) | (n1 == cur) | (n1 == cur - 1)
    dead = (n1 * BLOCK > t1) | (n1 >= nbs)
    sel = _top_blocks(jnp.where(forced, jnp.inf, jnp.where(dead, -jnp.inf, imp)), min(N_SELECT, nbs))
    selb = jnp.concatenate([sel] * GROUP, axis=0).astype(BF16)

    qr = qr_ref[0]
    wb = win_ref.shape[-1]
    kw = win_ref[0, 0].reshape(gd, wb).astype(BF16)
    vw = win_ref[0, 1].reshape(gd, wb).astype(BF16)
    d_old = t_pos - (past - wb + lax.broadcasted_iota(jnp.int32, (nr, wb), 1))
    s_old = jnp.where((d_old >= 0) & (d_old < WINDOW), _dot(qr, kw), NEG)
    s_new = jnp.where((tn <= t_row) & (tn < dt), _dot(qr, kwn_ref[0]), NEG)
    m = jnp.maximum(jnp.max(s_old, axis=-1, keepdims=True), jnp.max(s_new, axis=-1, keepdims=True))
    p_old = jnp.exp(s_old - m)
    p_new = jnp.exp(s_new - m)
    lw = jnp.sum(p_old, axis=-1, keepdims=True) + jnp.sum(p_new, axis=-1, keepdims=True)
    o_w = (_dot_nt(p_old.astype(BF16), vw) + _dot_nt(p_new.astype(BF16), vwn_ref[0])) / lw

    scores = []
    for j in range(n_pages):
        k_t = pages[j][0, 0].reshape(gd, PAGE).astype(BF16)
        scores.append(jnp.where(_dot(selb, e_ref[j]) > 0.5, _dot(qr, k_t), NEG))
    ok = (_dot(selb, en_ref[...]) > 0.5) & (tn <= t_row) & (tn < dt)
    scores.append(jnp.where(ok, _dot(qr, ksn_ref[0]), NEG))
    ps, ls = _softmax_chunks(scores)
    acc = _dot_nt(ps[n_pages].astype(BF16), vsn_ref[0])
    for j in range(n_pages):
        acc = acc + _dot_nt(ps[j].astype(BF16), pages[j][0, 1].reshape(gd, PAGE).astype(BF16))
    gt = gt_ref[0]
    o_ref[0] = gt[:, 0:1] * o_c + gt[:, 1:2] * (acc / ls) + gt[:, 2:3] * o_w


def _nsa_attn_sample(pool_t, pt, qbd_c, qbd_r, kc, vc, win_t, kwn, vwn, ksn, vsn, gates, past, dt, nbs, n_pages):
    db, nr, gd = qbd_r.shape
    key_blk = jnp.arange(n_pages * PAGE) // BLOCK
    e_tab = (jnp.arange(LANES)[None, :, None] == key_blk.reshape(n_pages, 1, PAGE)).astype(BF16)
    new_blk = jnp.where(jnp.arange(LANES) < dt, (past + jnp.arange(LANES)) // BLOCK, -1)
    e_new = (jnp.arange(LANES)[:, None] == new_blk[None, :]).astype(BF16)
    blk = lambda a: pl.BlockSpec((1,) + a.shape[1:], lambda i, pt: (i,) + (0,) * (a.ndim - 1))
    full = lambda a: pl.BlockSpec(a.shape, lambda i, pt: (0,) * a.ndim)
    per_batch = (qbd_c, qbd_r, kc, vc, win_t, kwn, vwn, ksn, vsn)
    return pl.pallas_call(
        functools.partial(_nsa_attn_sample_kernel, n_pages=n_pages, past=past, dt=dt, nbs=nbs),
        grid_spec=pltpu.PrefetchScalarGridSpec(
            num_scalar_prefetch=1, grid=(db,),
            in_specs=(_page_specs((2, N_GROUPS, HEAD_DIM, PAGE), 1, n_pages) + [blk(a) for a in per_batch]
                      + [full(e_tab), full(e_new), blk(gates)]),
            out_specs=pl.BlockSpec((1, nr, gd), lambda i, pt: (i, 0, 0))),
        out_shape=jax.ShapeDtypeStruct((db, nr, gd), F32),
        compiler_params=_params("parallel"),
        name="nsa_attention_sample",
    )(pt, *([pool_t] * n_pages), *per_batch, e_tab, e_new, gates)


def _fox_attn_sample_kernel(*refs, n_pages, dt):
    pages = refs[1:1 + n_pages]
    lfs = refs[1 + n_pages:1 + 2 * n_pages]
    q_ref, kn_ref, vn_ref, lfn_ref, og_ref, o_ref = refs[1 + 2 * n_pages:]
    hd = N_HEADS * HEAD_DIM
    tri = (lax.broadcasted_iota(jnp.int32, (PAGE, PAGE), 0) <= lax.broadcasted_iota(jnp.int32, (PAGE, PAGE), 1)).astype(BF16)

    def local_cum(lf):
        hi, mid, lo = _split3(lf)
        return _dot(hi, tri) + _dot(mid, tri) + _dot(lo, tri)

    q = q_ref[0]
    nr = q.shape[0]
    scores = []
    prefix = jnp.zeros((N_HEADS, 1), F32)
    for j in range(n_pages):
        loc = local_cum(lfs[j][0])
        c_page = loc + prefix
        prefix = prefix + loc[:, PAGE - 1:PAGE]
        k_t = pages[j][0, 0].reshape(hd, PAGE).astype(BF16)
        scores.append(_dot(q, k_t) - jnp.concatenate([c_page] * dt, axis=0))
    c_new = local_cum(lfn_ref[0]) + prefix
    t_row = lax.broadcasted_iota(jnp.int32, (nr, 1), 0) // N_HEADS
    tn = lax.broadcasted_iota(jnp.int32, (nr, LANES), 1)
    s_n = _dot(q, kn_ref[0]) - jnp.concatenate([c_new] * dt, axis=0)
    scores.append(jnp.where((tn <= t_row) & (tn < dt), s_n, NEG))
    ps, ls = _softmax_chunks(scores)
    acc = _dot_nt(ps[n_pages].astype(BF16), vn_ref[0])
    for j in range(n_pages):
        acc = acc + _dot_nt(ps[j].astype(BF16), pages[j][0, 1].reshape(hd, PAGE).astype(BF16))
    o_ref[0] = acc / ls * og_ref[0]


def _fox_attn_sample(pool_t, lf_pool_t, pt, qbd, kn_t, vn_t, lfn_t, og_t, dt, n_pages):
    db, nr, hd = qbd.shape
    blk = lambda a: pl.BlockSpec((1,) + a.shape[1:], lambda i, pt: (i,) + (0,) * (a.ndim - 1))
    lf_specs = [pl.BlockSpec((1, N_HEADS, PAGE), lambda i, pt, j=j: (pt[i * n_pages + j], 0, 0))
                for j in range(n_pages)]
    per_batch = (qbd, kn_t, vn_t, lfn_t, og_t)
    return pl.pallas_call(
        functools.partial(_fox_attn_sample_kernel, n_pages=n_pages, dt=dt),
        grid_spec=pltpu.PrefetchScalarGridSpec(
            num_scalar_prefetch=1, grid=(db,),
            in_specs=_page_specs((2, N_HEADS, HEAD_DIM, PAGE), 0, n_pages) + lf_specs + [blk(a) for a in per_batch],
            out_specs=pl.BlockSpec((1, nr, hd), lambda i, pt: (i, 0, 0))),
        out_shape=jax.ShapeDtypeStruct((db, nr, hd), F32),
        compiler_params=_params("parallel"),
        name="fox_attention_sample",
    )(pt, *([pool_t] * n_pages), *([lf_pool_t] * n_pages), *per_batch)


def _per_batch(a, dt, db):
    return a.reshape(dt, db, -1).transpose(1, 0, 2)


def _new_keys_t(a, dt, db):
    a = _per_batch(a, dt, db).transpose(0, 2, 1)
    return jnp.pad(a, ((0, 0), (0, 0), (0, LANES - dt))).astype(a.dtype)


def _nsa_qbd(q, dt, db):
    q5 = q.reshape(dt, db, N_GROUPS, GROUP, HEAD_DIM).transpose(1, 3, 0, 2, 4)
    eye = jnp.eye(N_GROUPS, dtype=q.dtype)
    out = q5[:, :, :, :, None, :] * eye[None, None, None, :, :, None]
    return out.reshape(db, GROUP * dt * N_GROUPS, N_GROUPS * HEAD_DIM)


def _nsa_undiag(o, dt, db):
    o6 = o.reshape(db, GROUP, dt, N_GROUPS, N_GROUPS, HEAD_DIM)
    dg = jnp.diagonal(o6, axis1=3, axis2=4)
    return dg.transpose(2, 0, 4, 1, 3).reshape(dt * db, N_HEADS * HEAD_DIM)


def _fox_qbd(q, dt, db):
    q4 = q.reshape(dt, db, N_HEADS, HEAD_DIM).transpose(1, 0, 2, 3)
    eye = jnp.eye(N_HEADS, dtype=q.dtype)
    out = q4[:, :, :, None, :] * eye[None, None, :, :, None]
    return out.reshape(db, dt * N_HEADS, N_HEADS * HEAD_DIM)


def _fox_undiag(o, dt, db):
    o5 = o.reshape(db, dt, N_HEADS, N_HEADS, HEAD_DIM)
    dg = jnp.diagonal(o5, axis1=2, axis2=3)
    return dg.transpose(1, 0, 3, 2).reshape(dt * db, N_HEADS * HEAD_DIM)


def _prompt_mods(mod, b):
    return [m.reshape(b, 1, -1) for m in jnp.split(mod[:b], 6, axis=-1)]


def _sample_mods(mod, b, t):
    return [jnp.tile(m, (t, 1))[None] for m in jnp.split(mod[b:], 6, axis=-1)]


def kernel(x_prompt, x_sample, cache_nsa_kv, cache_nsa_win, cache_fox_kv, cache_fox_logf, state_ffn_conv, page_table, c_prompt, c_sample, w_ada, b_ada, norm_mix_g, norm_ffn_g, w_nsa_in, pe_cmp, w_cmp1, w_cmp2, w_nsa_out, w_fox_in, b_fox_f, fox_q_norm_g, fox_k_norm_g, w_fox_out, w_ffn_up, ffn_conv_w, ffn_conv_b, w_ffn_down, final_norm_g):
    b, s, d = x_prompt.shape
    db, dt, _ = x_sample.shape
    f_dim = w_ffn_down.shape[1]
    depth = w_ada.shape[0]
    tm = 256
    tpb = s // tm
    tm_f = 512
    tpb_f = s // tm_f

    n_pages = page_table.shape[1]
    past = n_pages * PAGE
    nbs = -(-(past + dt) // BLOCK)
    per = PAGE // BLOCK
    r_s = dt * db
    pt = page_table.reshape(-1).astype(jnp.int32)
    key_last = (0, 2, 3, 4, 1)

    c_all = jnp.concatenate([c_prompt, c_sample], axis=0)
    xp = x_prompt.reshape(b * s, d)
    xs = x_sample.transpose(1, 0, 2).reshape(r_s, d)
    tabs_p = _rope_tables(jnp.arange(s, dtype=jnp.int32))
    tabs_s = _rope_tables(past + jnp.arange(r_s, dtype=jnp.int32) // db)

    nsa_kv_p, nsa_win_p, fox_kv_p, fox_lf_p, conv_p = [], [], [], [], []
    nsa_kv_s, nsa_win_s, fox_kv_s, fox_lf_s, conv_s = [], [], [], [], []
    y_prompt = y_sample = None
    for i in range(depth):
        j = i // 2
        mod = _adaln(c_all, w_ada[i].astype(BF16), b_ada[i][None])
        mp = _prompt_mods(mod, b)
        ms = _sample_mods(mod, b, dt)
        g_mix = norm_mix_g[i][None]
        if i % 2 == 0:
            w_in = _nsa_weight(w_nsa_in[j])
            qc, qr, rows, win, dup, gates = _nsa_proj(xp, g_mix, mp[1], mp[0], w_in, tabs_p, "prompt", tm, tpb)
            cdup = _compress_prompt(rows, *_cmp_weights(pe_cmp[j], w_cmp1[j], w_cmp2[j]), b, s)
            attn_p = _nsa_attention_t(qc, qr, cdup, dup, gates, b, s)
            og_p = None
            w_out = w_nsa_out[j].astype(BF16)
            nsa_kv_p.append(rows.reshape(b, s, 4, N_GROUPS, HEAD_DIM))
            nsa_win_p.append(win.reshape(b, s, 2, N_GROUPS, HEAD_DIM)[:, s - min(WINDOW, s):])

            qc, qr, rows, win, _, gates = _nsa_proj(xs, g_mix, ms[1], ms[0], w_in, tabs_s, "sample", r_s, 1)
            nsa_kv_s.append(rows.reshape(dt, db, 4, N_GROUPS, HEAD_DIM).transpose(1, 0, 2, 3, 4))
            nsa_win_s.append(win.reshape(dt, db, 2, N_GROUPS, HEAD_DIM).transpose(1, 0, 2, 3, 4))
            pool_t = jnp.transpose(cache_nsa_kv[j], key_last)
            kc2 = _nsa_cmp_sample(pool_t, pt, pe_cmp[j], w_cmp1[j], w_cmp2[j], db, n_pages)
            kc = kc2.reshape(db, 2, N_GROUPS, n_pages, per, HEAD_DIM).transpose(0, 1, 3, 4, 2, 5)
            kc = kc.reshape(db, 2, n_pages * per, N_GROUPS * HEAD_DIM)
            kc = jnp.pad(kc, ((0, 0), (0, 0), (0, LANES - n_pages * per), (0, 0))).astype(BF16)
            qbd_c, qbd_r = _nsa_qbd(qc, dt, db), _nsa_qbd(qr, dt, db)
            gd = N_GROUPS * HEAD_DIM
            newt = lambda a: _new_keys_t(a.astype(BF16), dt, db)
            g_s = gates.reshape(dt, db, N_GROUPS, LANES)[..., :GROUP * 3].reshape(dt, db, N_GROUPS, GROUP, 3)
            g_s = g_s.transpose(1, 3, 0, 2, 4).reshape(db, GROUP * dt * N_GROUPS, 3)
            g_s = jnp.pad(g_s, ((0, 0), (0, 0), (0, LANES - 3)))
            o_s = _nsa_attn_sample(pool_t, pt, qbd_c, qbd_r, kc[:, 0], kc[:, 1],
                                   jnp.transpose(cache_nsa_win[j], key_last), newt(win[:, :gd]), newt(win[:, gd:]),
                                   newt(rows[:, 2 * gd:3 * gd]), newt(rows[:, 3 * gd:]), g_s, past, dt, nbs, n_pages)
            attn_s = _nsa_undiag(o_s, dt, db).astype(BF16)
        else:
            w_in = _fox_weight(w_fox_in[j])
            fox = lambda x, sc, sh, mode, t, n: _fox_proj(x, g_mix, sc, sh, w_in, fox_q_norm_g[j], fox_k_norm_g[j],
                                                         b_fox_f[j], mode, t, n)
            q, kf, kb, vf, vb, og, lf, c = fox(xp, mp[1], mp[0], "prompt", tm, tpb)
            attn_p = _fox_attention(q, kb, vb, c, og, b, s).reshape(b * s, d)
            og_p = None
            w_out = w_fox_out[j].astype(BF16)
            hs = (N_HEADS, HEAD_DIM)
            fox_kv_p.append(jnp.stack([kf.reshape((b, s) + hs), vf.reshape((b, s) + hs)], axis=2))
            fox_lf_p.append(lf[:, :N_HEADS].reshape(b, s, N_HEADS))

            q, kf, kb, vf, vb, og, lf, _ = fox(xs, ms[1], ms[0], "sample", r_s, 1)
            fox_kv_s.append(jnp.stack([kf.reshape((dt, db) + hs), vf.reshape((dt, db) + hs)], axis=2).transpose(1, 0, 2, 3, 4))
            fox_lf_s.append(lf[:, :N_HEADS].reshape(dt, db, N_HEADS).transpose(1, 0, 2))
            og_t = jnp.repeat(_per_batch(og, dt, db)[:, :, None, :], N_HEADS, axis=2).reshape(db, dt * N_HEADS, -1)
            o_full = _fox_attn_sample(jnp.transpose(cache_fox_kv[j], key_last),
                                      jnp.transpose(cache_fox_logf[j], (0, 2, 1)), pt, _fox_qbd(q, dt, db),
                                      _new_keys_t(kb, dt, db), _new_keys_t(vb, dt, db),
                                      _new_keys_t(lf[:, :N_HEADS], dt, db), og_t, dt, n_pages)
            attn_s = _fox_undiag(o_full, dt, db).astype(BF16)

        final = final_norm_g[None] if i == depth - 1 else None
        ffn_w = (norm_ffn_g[i][None], _ffn_cols(w_ffn_up[i].astype(BF16)), _ffn_cols(ffn_conv_w[i]),
                 _ffn_cols(ffn_conv_b[i][None]), w_ffn_down[i].astype(BF16))
        unperm = lambda a: _ffn_cols(a, inverse=True)

        res = _ffn(xp, attn_p, w_out, (mp[2], mp[4], mp[3], mp[5]), *ffn_w, "prompt", tm_f, tpb_f, 1, final_g=final,
                   og=og_p)
        xp = res[0]
        conv_p.append(unperm(res[1].reshape(b, tpb_f, 8, 2 * f_dim)[:, -1, 6:, :]))
        if final is not None:
            y_prompt = res[2].reshape(b, s, d)

        state = _ffn_cols(state_ffn_conv[i].transpose(1, 0, 2).reshape(2 * db, 2 * f_dim))
        res = _ffn(xs, attn_s, w_out, (ms[2], ms[4], ms[3], ms[5]), *ffn_w, "sample", r_s, 1, db, state=state,
                   final_g=final)
        xs = res[0]
        conv_s.append(unperm(res[1]).reshape(2, db, 2 * f_dim).transpose(1, 0, 2))
        if final is not None:
            y_sample = res[2].reshape(dt, db, d).transpose(1, 0, 2)

    return (y_prompt, y_sample, jnp.stack(nsa_kv_p), jnp.stack(nsa_kv_s), jnp.stack(nsa_win_p), jnp.stack(nsa_win_s),
            jnp.stack(fox_kv_p), jnp.stack(fox_kv_s), jnp.stack(fox_lf_p), jnp.stack(fox_lf_s),
            jnp.stack(conv_p), jnp.stack(conv_s))
```

```python
import functools

import jax
import jax.numpy as jnp
from jax import lax
from jax.experimental import pallas as pl
from jax.experimental.pallas import tpu as pltpu

F32 = jnp.float32
BF16 = jnp.bfloat16

HEAD_DIM = 64
N_HEADS = 16
N_GROUPS = 4
GROUP = N_HEADS // N_GROUPS
BLOCK = 64
N_SELECT = 16
WINDOW = 512
ROT_DIM = 16
ROPE_THETA = 500000.0
PAGE = 128
Q_BLOCK = 128
EPS = 1e-6
NEG = -1e30
SCALE = HEAD_DIM ** -0.5

LANES = 128
VMEM_LIMIT = 56 * 1024 * 1024


def _params(*sem, flags=None):
    return pltpu.CompilerParams(dimension_semantics=sem, vmem_limit_bytes=VMEM_LIMIT, flags=flags)


def _dot(a, b):
    return jnp.dot(a, b, preferred_element_type=F32)


def _dot_nt(a, b):
    return lax.dot_general(a, b, (((1,), (1,)), ((), ())), preferred_element_type=F32)


def _split3(x):
    hi = x.astype(BF16)
    r1 = x - hi.astype(F32)
    mid = r1.astype(BF16)
    lo = (r1 - mid.astype(F32)).astype(BF16)
    return hi, mid, lo


def _ada_kernel(c_ref, w_ref, b_ref, o_ref):
    c = c_ref[...]
    a = (c * jax.nn.sigmoid(c)).astype(BF16)
    o_ref[...] = _dot(a, w_ref[...]) + b_ref[...]


def _adaln(c, w, b):
    r, d = c.shape
    n = w.shape[1]
    tn = n // 4
    return pl.pallas_call(
        _ada_kernel,
        grid=(n // tn,),
        in_specs=[pl.BlockSpec((r, d), lambda j: (0, 0)),
                  pl.BlockSpec((d, tn), lambda j: (0, j)),
                  pl.BlockSpec((1, tn), lambda j: (0, j))],
        out_specs=pl.BlockSpec((r, tn), lambda j: (0, j)),
        out_shape=jax.ShapeDtypeStruct((r, n), F32),
        compiler_params=_params("arbitrary"),
        name="adaln",
    )(c, w, b)


def _norm_mod(x, g, scale, shift):
    ms = jnp.mean(x * x, axis=-1, keepdims=True)
    return (x * lax.rsqrt(ms + EPS) * g) * (1.0 + scale) + shift


def _rope_tables(pos):
    freqs = ROPE_THETA ** (-jnp.arange(0, ROT_DIM, 2, dtype=F32) / ROT_DIM)
    ang = pos.astype(F32)[:, None] * freqs[None, :]
    cos, sin = jnp.cos(ang), jnp.sin(ang)
    half = ROT_DIM // 2
    one = jnp.ones((pos.shape[0], HEAD_DIM - ROT_DIM), F32)
    zero = jnp.zeros_like(one)
    zh = jnp.zeros_like(cos)
    c = jnp.concatenate([cos, cos, one], axis=1)
    s_lo = jnp.concatenate([zh, sin, zero], axis=1)
    s_hi = jnp.concatenate([-sin, zh, zero], axis=1)
    rep = LANES // HEAD_DIM
    return jnp.tile(c, (1, rep)), jnp.tile(s_lo, (1, rep)), jnp.tile(s_hi, (1, rep))


def _rope(v, c, s_lo, s_hi):
    half = ROT_DIM // 2
    return v * c + pltpu.roll(v, half, 1) * s_lo + pltpu.roll(v, LANES - half, 1) * s_hi


def _row_specs(mode, tm, tpb, d):
    if mode == "prompt":
        mod = pl.BlockSpec((1, 1, d), lambda i: (i // tpb, 0, 0))
        tab = pl.BlockSpec((tm, LANES), lambda i: (i % tpb, 0))
    else:
        mod = pl.BlockSpec((1, tm, d), lambda i: (0, 0, 0))
        tab = pl.BlockSpec((tm, LANES), lambda i: (0, 0))
    return mod, tab


NSA_DUP = 4 * N_GROUPS * LANES
NSA_W_COLS = 1024 + 6 * 256 + N_GROUPS * LANES


def _nsa_proj_kernel(x_ref, g_ref, sc_ref, sh_ref, w_ref, tc_ref, tl_ref, th_ref,
                     qc_ref, qr_ref, rows_ref, win_ref, dup_ref, gates_ref):
    h = _norm_mod(x_ref[...], g_ref[...], sc_ref[0], sh_ref[0]).astype(BF16)
    z = _dot(h, w_ref[...])
    tc, tl, th = tc_ref[...], tl_ref[...], th_ref[...]
    lo = lax.broadcasted_iota(jnp.int32, (z.shape[0], LANES), 1) < HEAD_DIM

    def chunk(j):
        return z[:, j * LANES:(j + 1) * LANES]

    def put_dup(kind, pair, v):
        vr = pltpu.roll(v, HEAD_DIM, 1)
        base = (kind * N_GROUPS + 2 * pair) * LANES
        dup_ref[:, base:base + LANES] = jnp.where(lo, v, vr).astype(BF16)
        dup_ref[:, base + LANES:base + 2 * LANES] = jnp.where(lo, vr, v).astype(BF16)

    for j in range(8):
        v = chunk(j)
        sl = slice(j * LANES, (j + 1) * LANES)
        qc_ref[:, sl] = (v * SCALE).astype(BF16)
        qr_ref[:, sl] = (_rope(v, tc, tl, th) * SCALE).astype(BF16)
    for j in range(4):
        rows_ref[:, j * LANES:(j + 1) * LANES] = chunk(8 + j)
    for j in range(2):
        ks = _rope(chunk(12 + j), tc, tl, th)
        vs = chunk(14 + j)
        rows_ref[:, (4 + j) * LANES:(5 + j) * LANES] = ks
        rows_ref[:, (6 + j) * LANES:(7 + j) * LANES] = vs
        put_dup(0, j, ks)
        put_dup(1, j, vs)
    for j in range(2):
        kw = _rope(chunk(16 + j), tc, tl, th)
        vw = chunk(18 + j)
        win_ref[:, j * LANES:(j + 1) * LANES] = kw
        win_ref[:, (2 + j) * LANES:(3 + j) * LANES] = vw
        put_dup(2, j, kw)
        put_dup(3, j, vw)
    for j in range(N_GROUPS):
        gates_ref[:, j * LANES:(j + 1) * LANES] = jax.nn.sigmoid(chunk(20 + j))


def _nsa_proj(x, g, scale, shift, w, tabs, mode, tm, tpb):
    r, d = x.shape
    mod, tab = _row_specs(mode, tm, tpb, d)
    row = lambda n: pl.BlockSpec((tm, n), lambda i: (i, 0))
    outs = [(1024, BF16), (1024, BF16), (1024, F32), (512, F32), (NSA_DUP, BF16), (N_GROUPS * LANES, F32)]
    return pl.pallas_call(
        _nsa_proj_kernel,
        grid=(r // tm,),
        in_specs=[row(d), pl.BlockSpec((1, d), lambda i: (0, 0)), mod, mod,
                  pl.BlockSpec(w.shape, lambda i: (0, 0)), tab, tab, tab],
        out_specs=[row(n) for n, _ in outs],
        out_shape=[jax.ShapeDtypeStruct((r, n), t) for n, t in outs],
        compiler_params=_params("parallel"),
        name="nsa_proj",
    )(x, g, scale, shift, w, *tabs)


def _nsa_weight(w_in):
    d = w_in.shape[0]
    main = w_in[:, :1024 + 6 * 256]
    gates = w_in[:, 1024 + 6 * 256:].reshape(d, N_GROUPS, GROUP * 3)
    gates = jnp.pad(gates, ((0, 0), (0, 0), (0, LANES - GROUP * 3))).reshape(d, N_GROUPS * LANES)
    return jnp.concatenate([main, gates], axis=1).astype(BF16)


def _cmp_kernel(x_ref, pe_ref, w1_ref, w2_ref, o_ref, acc_ref):
    lc = pl.program_id(2)

    @pl.when(lc == 0)
    def _():
        acc_ref[...] = jnp.zeros_like(acc_ref)

    acc = acc_ref[...]
    for l in range(x_ref.shape[1]):
        xl = (x_ref[0, l] + pe_ref[0, l:l + 1, :]).astype(BF16)
        acc = acc + _dot(xl, w1_ref[0, l])
    acc_ref[...] = acc

    @pl.when(lc == pl.num_programs(2) - 1)
    def _():
        hid = acc * jax.nn.sigmoid(acc)
        o_ref[0, 0] = _dot(hid.astype(BF16), w2_ref[0]).astype(BF16)


def _block_diag(w, n):
    eye = jnp.eye(n, dtype=w.dtype)
    out = jnp.einsum("ij,...ab->...iajb", eye, w)
    return out.reshape(w.shape[:-2] + (n * w.shape[-2], n * w.shape[-1]))


def _cmp_weights(pe, w1, w2):
    pe_t = jnp.tile(pe, (1, 1, N_GROUPS))
    w1_bd = _block_diag(w1, N_GROUPS).astype(BF16)
    w2_dup = jnp.concatenate([w2, w2], axis=-1)
    w2_bd = _block_diag(w2_dup, N_GROUPS).astype(BF16)
    return pe_t, w1_bd, w2_bd


def _compress_prompt(rows, pe_t, w1_bd, w2_bd, b, s):
    nb = s // BLOCK
    gd = N_GROUPS * HEAD_DIM
    xt = rows.reshape(b, nb, BLOCK, -1)[..., :2 * gd].transpose(0, 2, 1, 3)
    lstep = 8
    return pl.pallas_call(
        _cmp_kernel,
        grid=(b, 2, BLOCK // lstep),
        in_specs=[pl.BlockSpec((1, lstep, nb, gd), lambda i, kv, lc: (i, lc, 0, kv)),
                  pl.BlockSpec((1, lstep, gd), lambda i, kv, lc: (kv, lc, 0)),
                  pl.BlockSpec((1, lstep, gd, gd), lambda i, kv, lc: (kv, lc, 0, 0)),
                  pl.BlockSpec((1, gd, N_GROUPS * LANES), lambda i, kv, lc: (kv, 0, 0))],
        out_specs=pl.BlockSpec((1, 1, nb, N_GROUPS * LANES), lambda i, kv, lc: (i, kv, 0, 0)),
        out_shape=jax.ShapeDtypeStruct((b, 2, nb, N_GROUPS * LANES), BF16),
        scratch_shapes=[pltpu.VMEM((nb, gd), F32)],
        compiler_params=_params("parallel", "parallel", "arbitrary"),
        name="nsa_compress",
    )(xt, pe_t, w1_bd, w2_bd)


def _softmax_rows(s):
    m = jnp.max(s, axis=-1, keepdims=True)
    p = jnp.exp(s - m)
    return p, jnp.sum(p, axis=-1, keepdims=True)


def _top_rows(v, n_sel):
    nb = v.shape[0]
    n = lax.broadcasted_iota(jnp.int32, v.shape, 0)
    sel = jnp.zeros(v.shape, F32)
    for _ in range(n_sel):
        mx = jnp.max(v, axis=0, keepdims=True)
        idx = jnp.min(jnp.where(v == mx, n, nb), axis=0, keepdims=True)
        hit = n == idx
        sel = jnp.where(hit, 1.0, sel)
        v = jnp.where(hit, -jnp.inf, v)
    return sel


def _top_blocks(v, n_sel):
    rows, nb = v.shape
    pad = -rows % LANES
    if pad:
        v = jnp.concatenate([v, jnp.zeros((pad, nb), v.dtype)], axis=0)
    return _top_rows(v.T, n_sel).T[:rows]


def _stack_heads(ref):
    q = ref.shape[1]
    lo = lax.broadcasted_iota(jnp.int32, (q, LANES), 1) < HEAD_DIM
    parts = []
    for r in range(GROUP):
        pair = ref[0, :, (r // 2) * LANES:(r // 2 + 1) * LANES]
        parts.append(jnp.where(lo if r % 2 == 0 else jnp.logical_not(lo), pair, jnp.zeros_like(pair)))
    return jnp.concatenate(parts, axis=0)


def _nsa_attn_kernel(qc_ref, qr_ref, kc_ref, vc_ref, ks_ref, vs_ref, kw_ref, vw_ref, e_ref, gt_ref, o_ref,
                     *, tk, wlen):
    nq = Q_BLOCK
    s0 = pl.program_id(2) * nq
    t_q = s0 + lax.broadcasted_iota(jnp.int32, (nq, 1), 0)
    rep = lambda a: jnp.concatenate([a] * GROUP, axis=0)

    qc = _stack_heads(qc_ref)
    kc = kc_ref[0, 0]
    nb = kc.shape[0]
    n = lax.broadcasted_iota(jnp.int32, (nq, nb), 1)
    cmask = rep(n * BLOCK + (BLOCK - 1) <= t_q)
    sc = jnp.where(cmask, _dot_nt(qc, kc), NEG)
    e = jnp.where(cmask, jnp.exp(sc - jnp.max(sc, axis=-1, keepdims=True)), 0.0)
    l = jnp.sum(e, axis=-1, keepdims=True)
    pc = e / jnp.where(l > 0.0, l, 1.0)
    o_c = _dot(pc.astype(BF16), vc_ref[0, 0])

    imp = pc[0:nq] + pc[nq:2 * nq] + pc[2 * nq:3 * nq] + pc[3 * nq:4 * nq]
    cur = t_q // BLOCK
    forced = (n == 0) | (n == cur) | (n == cur - 1)
    future = n * BLOCK > t_q
    sel = _top_blocks(jnp.where(forced, jnp.inf, jnp.where(future, -jnp.inf, imp)), min(N_SELECT, nb))
    selb = sel.astype(BF16)

    qr = _stack_heads(qr_ref)
    w0 = pl.multiple_of(jnp.maximum(s0 + nq - wlen, 0), nq)
    kpos = w0 + lax.broadcasted_iota(jnp.int32, (nq, wlen), 1)
    dpos = t_q - kpos
    wmask = rep((dpos >= 0) & (dpos < WINDOW))
    sw = jnp.where(wmask, _dot_nt(qr, kw_ref[0, pl.ds(w0, wlen), :]), NEG)
    pw, lw = _softmax_rows(sw)
    o_w = _dot(pw.astype(BF16), vw_ref[0, pl.ds(w0, wlen), :]) / lw

    col = lax.broadcasted_iota(jnp.int32, (nq, tk), 1)
    half = GROUP * nq // 2
    q_halves = (qr[:half], qr[half:])

    def body(j, carry):
        k0 = pl.multiple_of(j * tk, tk)
        k = ks_ref[0, pl.ds(k0, tk), :]
        v = vs_ref[0, pl.ds(k0, tk), :]
        ok = (_dot(selb, e_ref[j]) > 0.5) & (k0 + col <= t_q)
        ok2 = jnp.concatenate([ok, ok], axis=0)
        out = []
        for q_h, (m_i, l_i, acc) in zip(q_halves, carry):
            s = jnp.where(ok2, _dot_nt(q_h, k), NEG)
            m_n = jnp.maximum(m_i, jnp.max(s, axis=-1, keepdims=True))
            alpha = jnp.exp(m_i - m_n)
            p = jnp.exp(s - m_n)
            l_n = alpha * l_i + jnp.sum(p, axis=-1, keepdims=True)
            out.append((m_n, l_n, alpha * acc + _dot(p.astype(BF16), v)))
        return tuple(out)

    init = (jnp.full((half, 1), NEG, F32), jnp.zeros((half, 1), F32), jnp.zeros((half, LANES), F32))
    (_, l_a, acc_a), (_, l_b, acc_b) = lax.fori_loop(0, (s0 + nq + tk - 1) // tk, body, (init, init))
    o_s = jnp.concatenate([acc_a / l_a, acc_b / l_b], axis=0)

    gt = gt_ref[0]
    lo = lax.broadcasted_iota(jnp.int32, (nq, LANES), 1) < HEAD_DIM
    outs = []
    for r in range(GROUP):
        sl = slice(r * nq, (r + 1) * nq)
        outs.append(gt[:, 3 * r:3 * r + 1] * o_c[sl] + gt[:, 3 * r + 1:3 * r + 2] * o_s[sl]
                    + gt[:, 3 * r + 2:3 * r + 3] * o_w[sl])
    for pr in range(GROUP // 2):
        o_ref[0, :, pr * LANES:(pr + 1) * LANES] = jnp.where(lo, outs[2 * pr], outs[2 * pr + 1]).astype(BF16)


def _nsa_attention(qc, qr, cdup, dup, gates, b, s):
    nb = s // BLOCK
    tk = min(512, s)
    wlen = min(WINDOW + Q_BLOCK, s)
    blk = (jnp.arange(s) // BLOCK).reshape(s // tk, 1, tk)
    expand = (jnp.arange(nb)[None, :, None] == blk).astype(BF16)
    g4 = N_GROUPS
    qspec = pl.BlockSpec((1, Q_BLOCK, 2 * LANES), lambda i, g, q: (i, q, g))
    dspec = lambda kind: pl.BlockSpec((1, s, LANES), lambda i, g, q: (i, 0, kind * g4 + g))
    return pl.pallas_call(
        functools.partial(_nsa_attn_kernel, tk=tk, wlen=wlen),
        grid=(b, N_GROUPS, s // Q_BLOCK),
        in_specs=[qspec, qspec,
                  pl.BlockSpec((1, 1, nb, LANES), lambda i, g, q: (i, 0, 0, g)),
                  pl.BlockSpec((1, 1, nb, LANES), lambda i, g, q: (i, 1, 0, g)),
                  dspec(0), dspec(1), dspec(2), dspec(3),
                  pl.BlockSpec(expand.shape, lambda i, g, q: (0, 0, 0)),
                  pl.BlockSpec((1, Q_BLOCK, LANES), lambda i, g, q: (i, q, g))],
        out_specs=qspec,
        out_shape=jax.ShapeDtypeStruct((b, s, 1024), BF16),
        compiler_params=_params("parallel", "parallel", "arbitrary"),
        name="nsa_attention",
    )(qc.reshape(b, s, -1), qr.reshape(b, s, -1), cdup, cdup,
      dup.reshape(b, s, -1), dup.reshape(b, s, -1), dup.reshape(b, s, -1), dup.reshape(b, s, -1),
      expand, gates.reshape(b, s, -1))


NSA_TK = 1024


def _nsa_attn_t_kernel(qc_ref, qr_ref, kc_ref, vc_ref, ks_ref, vs_ref, kw_ref, vw_ref, e_ref, gt_ref, o_ref,
                       *, tk, wlen):
    nq = Q_BLOCK
    ncol = GROUP * nq
    s0 = pl.program_id(2) * nq
    t_q = s0 + lax.broadcasted_iota(jnp.int32, (1, nq), 1)
    t_col = jnp.concatenate([t_q] * GROUP, axis=1)
    qc, qr = qc_ref[0, 0], qr_ref[0, 0]

    kc = kc_ref[0, 0]
    nb = kc.shape[0]
    cmask = lax.broadcasted_iota(jnp.int32, (nb, 1), 0) * BLOCK + (BLOCK - 1) <= t_col
    sc = jnp.where(cmask, _dot(kc, qc), NEG)
    e = jnp.where(cmask, jnp.exp(sc - jnp.max(sc, axis=0, keepdims=True)), 0.0)
    l = jnp.sum(e, axis=0, keepdims=True)
    pc = e / jnp.where(l > 0.0, l, 1.0)
    o_c = _dot(vc_ref[0, 0], pc.astype(BF16))

    imp = pc[:, 0:nq]
    for r in range(1, GROUP):
        imp = imp + pc[:, r * nq:(r + 1) * nq]
    n = lax.broadcasted_iota(jnp.int32, (nb, nq), 0)
    cur = t_q // BLOCK
    forced = (n == 0) | (n == cur) | (n == cur - 1)
    future = n * BLOCK > t_q
    sel = _top_rows(jnp.where(forced, jnp.inf, jnp.where(future, -jnp.inf, imp)), min(N_SELECT, nb))
    selb = sel.astype(BF16)

    w0 = pl.multiple_of(jnp.maximum(s0 + nq - wlen, 0), nq)
    dpos = t_col - (w0 + lax.broadcasted_iota(jnp.int32, (wlen, 1), 0))
    sw = jnp.where((dpos >= 0) & (dpos < WINDOW), _dot(kw_ref[0, pl.ds(w0, wlen), :], qr), NEG)
    pw = jnp.exp(sw - jnp.max(sw, axis=0, keepdims=True))
    lw = jnp.sum(pw, axis=0, keepdims=True)
    pwb = pw.astype(BF16)
    c0 = w0 // nq
    o_w = _dot(vw_ref[0, c0], pwb[0:nq])
    for c in range(1, wlen // nq):
        o_w = o_w + _dot(vw_ref[0, c0 + c], pwb[c * nq:(c + 1) * nq])
    o_w = o_w / lw

    sub = lax.broadcasted_iota(jnp.int32, (tk, 1), 0)

    def body(j, carry):
        m_i, l_i, acc = carry
        k0 = pl.multiple_of(j * tk, tk)
        ok = (_dot(e_ref[j], selb) > 0.5) & (k0 + sub <= t_q)
        s = jnp.where(jnp.concatenate([ok] * GROUP, axis=1), _dot(ks_ref[0, pl.ds(k0, tk), :], qr), NEG)
        m_n = jnp.maximum(m_i, jnp.max(s, axis=0, keepdims=True))
        alpha = jnp.exp(m_i - m_n)
        p = jnp.exp(s - m_n)
        l_n = alpha * l_i + jnp.sum(p, axis=0, keepdims=True)
        return m_n, l_n, alpha * acc + _dot(vs_ref[0, j], p.astype(BF16))

    init = (jnp.full((1, ncol), NEG, F32), jnp.zeros((1, ncol), F32), jnp.zeros((LANES, ncol), F32))
    _, l_s, acc_s = lax.fori_loop(0, (s0 + nq + tk - 1) // tk, body, init)

    gt = gt_ref[0, 0]
    out = gt[0:1] * o_c + gt[1:2] * (acc_s / l_s) + gt[2:3] * o_w
    o_ref[0, 0] = out[0:HEAD_DIM].astype(BF16)


def _nsa_attention_t(qc, qr, cdup, dup, gates, b, s):
    nb = s // BLOCK
    tk = min(NSA_TK, s)
    nt = s // tk
    wlen = min(WINDOW + Q_BLOCK, s)
    nqb = s // Q_BLOCK
    g4 = N_GROUPS
    ncol = GROUP * Q_BLOCK

    def q_t(q):
        q6 = q.reshape(b, nqb, Q_BLOCK, g4, GROUP, HEAD_DIM).transpose(0, 3, 5, 1, 4, 2)
        q6 = q6.reshape(b, g4, HEAD_DIM, nqb * ncol)
        return jnp.pad(q6, ((0, 0), (0, 0), (0, LANES - HEAD_DIM), (0, 0)))

    def v_t(kind, rows):
        v = dup.reshape(b, s, -1)[:, :, kind * g4 * LANES:(kind + 1) * g4 * LANES]
        return v.reshape(b, s // rows, rows, g4 * LANES).transpose(0, 1, 3, 2)

    vc_t = cdup[:, 1].reshape(b, nb, g4, LANES).transpose(0, 2, 3, 1)
    vs_t, vw_t = v_t(1, tk), v_t(3, Q_BLOCK)
    blk = (jnp.arange(s) // BLOCK).reshape(nt, tk, 1)
    expand = (jnp.arange(nb)[None, None, :] == blk).astype(BF16)
    g_t = gates.reshape(b, nqb, Q_BLOCK, g4, LANES)[..., :GROUP * 3].reshape(b, nqb, Q_BLOCK, g4, GROUP, 3)
    g_t = g_t.transpose(0, 3, 5, 1, 4, 2).reshape(b, g4, 3, nqb * ncol)
    g_t = jnp.pad(g_t, ((0, 0), (0, 0), (0, 5), (0, 0)))

    qspec = pl.BlockSpec((1, 1, LANES, ncol), lambda i, g, q: (i, g, 0, q))
    dspec = lambda kind: pl.BlockSpec((1, s, LANES), lambda i, g, q: (i, 0, kind * g4 + g))
    o_t = pl.pallas_call(
        functools.partial(_nsa_attn_t_kernel, tk=tk, wlen=wlen),
        grid=(b, g4, nqb),
        in_specs=[qspec, qspec,
                  pl.BlockSpec((1, 1, nb, LANES), lambda i, g, q: (i, 0, 0, g)),
                  pl.BlockSpec((1, 1, LANES, nb), lambda i, g, q: (i, g, 0, 0)),
                  dspec(0), pl.BlockSpec((1, nt, LANES, tk), lambda i, g, q: (i, 0, g, 0)),
                  dspec(2), pl.BlockSpec((1, nqb, LANES, Q_BLOCK), lambda i, g, q: (i, 0, g, 0)),
                  pl.BlockSpec(expand.shape, lambda i, g, q: (0, 0, 0)),
                  pl.BlockSpec((1, 1, 8, ncol), lambda i, g, q: (i, g, 0, q))],
        out_specs=pl.BlockSpec((1, 1, HEAD_DIM, ncol), lambda i, g, q: (i, g, 0, q)),
        out_shape=jax.ShapeDtypeStruct((b, g4, HEAD_DIM, nqb * ncol), BF16),
        compiler_params=_params("parallel", "parallel", "arbitrary"),
        name="nsa_attention",
    )(q_t(qc), q_t(qr), cdup, vc_t, dup.reshape(b, s, -1), vs_t, dup.reshape(b, s, -1), vw_t, expand, g_t)
    o6 = o_t.reshape(b, g4, HEAD_DIM, nqb, GROUP, Q_BLOCK).transpose(0, 3, 5, 1, 4, 2)
    return o6.reshape(b * s, g4 * GROUP * HEAD_DIM)


FFN_CHUNK = 256


def _ffn_kernel(*refs, u, tpb, chain, final, gated, nf):
    (x_ref, a_ref, wo_ref, gm_ref, g_ref, sc_ref, sh_ref, gf_ref, wu_ref, cw_ref, cb_ref, wd_ref) = refs[:12]
    k = 12
    og_ref = st_ref = gfin_ref = y_ref = carry_scr = None
    if gated:
        og_ref = refs[k]
        k += 1
    if not chain:
        st_ref = refs[k]
        k += 1
    if final:
        gfin_ref = refs[k]
        k += 1
    xo_ref, tail_ref = refs[k], refs[k + 1]
    k += 2
    if final:
        y_ref = refs[k]
        k += 1
    ext_scr = refs[k]
    if chain:
        carry_scr = refs[k + 1]

    fc = FFN_CHUNK
    tm = x_ref.shape[0]
    base = ext_scr.shape[0] - tm
    tail = tail_ref.shape[0]
    a = (a_ref[...] * og_ref[...]).astype(BF16) if gated else a_ref[...]
    x1 = x_ref[...] + gm_ref[0] * _dot(a, wo_ref[...])
    h = _norm_mod(x1, g_ref[...], sc_ref[0], sh_ref[0]).astype(BF16)
    if chain:
        first = (pl.program_id(0) % tpb) == 0
    acc = jnp.zeros((tm, x_ref.shape[1]), F32)
    f_dim = wd_ref.shape[0]

    def conv_half(f, half):
        cs = slice(half * f_dim + f * fc, half * f_dim + (f + 1) * fc)
        hs = slice(half * fc, (half + 1) * fc)
        up = _dot(h, wu_ref[:, cs])
        if chain:
            ext_scr[0:base, hs] = jnp.where(first, 0.0, carry_scr[f, :, hs])
            carry_scr[f, :, hs] = up[tm - base:, :]
        else:
            ext_scr[0:base, hs] = st_ref[:, cs]
        ext_scr[base:, hs] = up
        tail_ref[:, cs] = up[tm - tail:, :]
        cw = cw_ref[:, cs]
        return (cb_ref[:, cs] + cw[0:1] * ext_scr[base - 2 * u:base - 2 * u + tm, hs]
                + cw[1:2] * ext_scr[base - u:base - u + tm, hs] + cw[2:3] * up)

    for f in range(nf):
        a, g = conv_half(f, 0), conv_half(f, 1)
        act = (g * jax.nn.sigmoid(g) * a).astype(BF16)
        acc = acc + _dot(act, wd_ref[f * fc:(f + 1) * fc, :])
    xn = x1 + gf_ref[0] * acc
    xo_ref[...] = xn
    if final:
        ms = jnp.mean(xn * xn, axis=-1, keepdims=True)
        y_ref[...] = xn * lax.rsqrt(ms + EPS) * gfin_ref[...]


def _ffn(x, attn, w_out, mods, g, wu, cw, cb, wd, mode, tm, tpb, u, state=None, final_g=None, og=None):
    r, d = x.shape
    f2 = wu.shape[1]
    nf = f2 // (2 * FFN_CHUNK)
    chain = state is None
    final = final_g is not None
    gated = og is not None
    base = 8 if chain else 2 * u
    tail = 8 if chain else 2 * u
    mod, _ = _row_specs(mode, tm, tpb, d)
    row = lambda n: pl.BlockSpec((tm, n), lambda i: (i, 0))
    full = lambda a: pl.BlockSpec(a.shape, lambda i: (0,) * a.ndim, pipeline_mode=pl.Buffered(1))
    args = [x, attn, w_out, mods[0], g, mods[1], mods[2], mods[3], wu, cw, cb, wd]
    specs = [row(d), row(d), full(w_out), mod, full(g), mod, mod, mod, full(wu), full(cw), full(cb), full(wd)]
    if gated:
        args.append(og)
        specs.append(row(d))
    if not chain:
        args.append(state)
        specs.append(full(state))
    if final:
        args.append(final_g)
        specs.append(full(final_g))
    out_specs = [row(d), pl.BlockSpec((tail, f2), lambda i: (i, 0))]
    out_shape = [jax.ShapeDtypeStruct((r, d), F32), jax.ShapeDtypeStruct((r // tm * tail, f2), F32)]
    if final:
        out_specs.append(row(d))
        out_shape.append(jax.ShapeDtypeStruct((r, d), F32))
    scratch = [pltpu.VMEM((base + tm, 2 * FFN_CHUNK), F32)]
    if chain:
        scratch.append(pltpu.VMEM((nf, base, 2 * FFN_CHUNK), F32))
    return pl.pallas_call(
        functools.partial(_ffn_kernel, u=u, tpb=tpb, chain=chain, final=final, gated=gated, nf=nf),
        grid=(r // tm,),
        in_specs=specs, out_specs=out_specs, out_shape=out_shape, scratch_shapes=scratch,
        compiler_params=_params("arbitrary"),
        name="out_proj_ffn",
    )(*args)


FOX_W_COLS = 4 * 1024 + LANES


def _fox_proj_kernel(x_ref, g_ref, sc_ref, sh_ref, w_ref, ind_ref, indt_ref, gq_ref, gk_ref, bf_ref,
                     q_ref, kf_ref, kb_ref, vf_ref, vb_ref, og_ref, lf_ref, c_ref, carry_scr, *, tpb):
    h = _norm_mod(x_ref[...], g_ref[...], sc_ref[0], sh_ref[0]).astype(BF16)
    z = _dot(h, w_ref[...])
    tm = z.shape[0]
    ind, indt = ind_ref[...], indt_ref[...]

    def head_norm(zc, gain):
        sq = zc * zc
        hi = sq.astype(BF16)
        lo = (sq - hi.astype(F32)).astype(BF16)
        ms = (_dot(hi, ind) + _dot(lo, ind)) * (1.0 / HEAD_DIM)
        rinv = lax.rsqrt(ms + EPS)
        rh = rinv.astype(BF16)
        rl = (rinv - rh.astype(F32)).astype(BF16)
        return zc * (_dot(rh, indt) + _dot(rl, indt)) * gain

    q_ref[...] = (head_norm(z[:, 0:1024], gq_ref[...]) * SCALE).astype(BF16)
    kn = head_norm(z[:, 1024:2048], gk_ref[...])
    kf_ref[...] = kn
    kb_ref[...] = kn.astype(BF16)
    v = z[:, 2048:3072]
    vf_ref[...] = v
    vb_ref[...] = v.astype(BF16)
    og_ref[...] = jax.nn.sigmoid(z[:, 3072:4096])
    zf = z[:, 4096:4096 + LANES] + bf_ref[...]
    lf = jnp.minimum(zf, 0.0) - jnp.log1p(jnp.exp(-jnp.abs(zf)))
    lf = jnp.where(lax.broadcasted_iota(jnp.int32, lf.shape, 1) < N_HEADS, lf, 0.0)
    lf_ref[...] = lf

    @pl.when(pl.program_id(0) % tpb == 0)
    def _():
        carry_scr[...] = jnp.zeros_like(carry_scr)

    tri = (lax.broadcasted_iota(jnp.int32, (tm, tm), 0) >= lax.broadcasted_iota(jnp.int32, (tm, tm), 1)).astype(BF16)
    hi, mid, lo = _split3(lf)
    c = _dot(tri, hi) + _dot(tri, mid) + _dot(tri, lo) + carry_scr[0:1, :]
    c_ref[...] = c
    carry_scr[0:1, :] = c[tm - 1:tm, :]


def _fox_weight(w_in):
    d = w_in.shape[0]
    return jnp.pad(w_in, ((0, 0), (0, FOX_W_COLS - w_in.shape[1]))).astype(BF16)


def _fox_proj(x, g, scale, shift, w, gq, gk, bf, mode, tm, tpb):
    r, d = x.shape
    mod, _ = _row_specs(mode, tm, tpb, d)
    row = lambda n: pl.BlockSpec((tm, n), lambda i: (i, 0))
    full = lambda a: pl.BlockSpec(a.shape, lambda i: (0,) * a.ndim)
    head_of = jnp.arange(1024) // HEAD_DIM
    ind = (head_of[:, None] == jnp.arange(LANES)[None, :]).astype(BF16)
    gq_t = jnp.tile(gq, N_HEADS)[None]
    gk_t = jnp.tile(gk, N_HEADS)[None]
    bf_p = jnp.pad(bf, (0, LANES - bf.shape[0]))[None]
    outs = [(1024, BF16), (1024, F32), (1024, BF16), (1024, F32), (1024, BF16), (1024, F32), (LANES, F32), (LANES, F32)]
    return pl.pallas_call(
        functools.partial(_fox_proj_kernel, tpb=tpb),
        grid=(r // tm,),
        in_specs=[row(d), full(g), mod, mod, full(w), full(ind), full(ind.T), full(gq_t), full(gk_t), full(bf_p)],
        out_specs=[row(n) for n, _ in outs],
        out_shape=[jax.ShapeDtypeStruct((r, n), t) for n, t in outs],
        scratch_shapes=[pltpu.VMEM((8, LANES), F32)],
        compiler_params=_params("arbitrary"),
        name="fox_proj",
    )(x, g, scale, shift, w, ind, ind.T, gq_t, gk_t, bf_p)


def _fox_attn_kernel(q_ref, k_ref, v_ref, nc_ref, og_ref, o_ref, *, t, tk, nt):
    qi = pl.program_id(2)
    q = q_ref[0]
    lo = lax.broadcasted_iota(jnp.int32, (t, LANES), 1) < HEAD_DIM
    t_q = qi * t + lax.broadcasted_iota(jnp.int32, (t, 1), 0)
    col = lax.broadcasted_iota(jnp.int32, (t, tk), 1)
    q_heads = (jnp.where(lo, q, jnp.zeros_like(q)), jnp.where(lo, jnp.zeros_like(q), q))

    def tile(j, carry, masked):
        k0 = pl.multiple_of(j * tk, tk)
        k = k_ref[0, pl.ds(k0, tk), :]
        v = v_ref[0, pl.ds(k0, tk), :]
        out = []
        for h2, (m_i, l_i, acc) in enumerate(carry):
            s = _dot_nt(q_heads[h2], k) + nc_ref[0, 0, h2 * nt + j]
            if masked:
                s = jnp.where(k0 + col <= t_q, s, NEG)
            m_n = jnp.maximum(m_i, jnp.max(s, axis=-1, keepdims=True))
            alpha = jnp.exp(m_i - m_n)
            p = jnp.exp(s - m_n)
            l_n = alpha * l_i + jnp.sum(p, axis=-1, keepdims=True)
            out.append((m_n, l_n, alpha * acc + _dot(p.astype(BF16), v)))
        return tuple(out)

    init = (jnp.full((t, 1), NEG, F32), jnp.zeros((t, 1), F32), jnp.zeros((t, LANES), F32))
    jd = (qi * t) // tk
    carry = lax.fori_loop(0, jd, lambda j, c: tile(j, c, False), (init, init))
    for dj in range(max(1, t // tk)):
        carry = tile(jd + dj, carry, True)
    (_, l_0, acc_0), (_, l_1, acc_1) = carry
    o_ref[0] = (jnp.where(lo, acc_0 / l_0, acc_1 / l_1) * og_ref[0]).astype(BF16)


FOX_TQ = 1024
FOX_TK = 1024


def _fox_attention(q, kb, vb, c, og, b, s):
    t, tk = min(FOX_TQ, s), min(FOX_TK, s)
    nt = s // tk
    hp = N_HEADS // 2
    negc = -c[:, :N_HEADS].reshape(b, nt, tk, hp, 2).transpose(0, 3, 4, 1, 2).reshape(b, hp, 2 * nt, 1, tk)
    qspec = pl.BlockSpec((1, t, LANES), lambda i, p, qi: (i, qi, p))
    kspec = pl.BlockSpec((1, s, LANES), lambda i, p, qi: (i, 0, p))
    return pl.pallas_call(
        functools.partial(_fox_attn_kernel, t=t, tk=tk, nt=nt),
        grid=(b, hp, s // t),
        in_specs=[qspec, kspec, kspec,
                  pl.BlockSpec((1, 1, 2 * nt, 1, tk), lambda i, p, qi: (i, p, 0, 0, 0)), qspec],
        out_specs=qspec,
        out_shape=jax.ShapeDtypeStruct((b, s, 1024), BF16),
        compiler_params=_params("parallel", "parallel", "arbitrary"),
        name="fox_attention",
    )(q.reshape(b, s, -1), kb.reshape(b, s, -1), vb.reshape(b, s, -1), negc, og.reshape(b, s, -1))


def _fox_attn_t_kernel(q_ref, k_ref, v_ref, o_ref, *, tq, tk):
    qi = pl.program_id(2)
    t_q = qi * tq + lax.broadcasted_iota(jnp.int32, (1, tq), 1)
    sub = lax.broadcasted_iota(jnp.int32, (tk, 1), 0)
    jd = (qi * tq) // tk

    def tile(j, carry, masked):
        k0 = pl.multiple_of(j * tk, tk)
        out = []
        for h, (m_i, l_i, acc) in enumerate(carry):
            s = _dot(k_ref[0, h, pl.ds(k0, tk), :], q_ref[0, h])
            if masked:
                s = jnp.where(k0 + sub <= t_q, s, NEG)
            m_n = jnp.maximum(m_i, jnp.max(s, axis=0, keepdims=True))
            alpha = jnp.exp(m_i - m_n)
            p = jnp.exp(s - m_n)
            l_n = alpha * l_i + jnp.sum(p, axis=0, keepdims=True)
            out.append((m_n, l_n, alpha * acc + _dot(v_ref[0, h, j], p.astype(BF16))))
        return tuple(out)

    init = (jnp.full((1, tq), NEG, F32), jnp.zeros((1, tq), F32), jnp.zeros((HEAD_DIM, tq), F32))
    carry = lax.fori_loop(0, jd, lambda j, c: tile(j, c, False), (init, init))
    for h, (_, l_f, acc_f) in enumerate(tile(jd, carry, True)):
        o_ref[0, h] = acc_f / l_f


def _fox_attention_t(q, kb, vb, c, b, s):
    tq, tk = min(256, s), min(512, s)
    nt = s // tk
    hs = (b, s, N_HEADS, HEAD_DIM)
    negc = _split3(-c[:, :N_HEADS].reshape(b, s, N_HEADS))
    fill = jnp.zeros(hs[:3] + (LANES - HEAD_DIM - 3,), BF16)
    k_aug = jnp.concatenate([kb.reshape(hs)] + [t[..., None] for t in negc] + [fill], axis=-1).transpose(0, 2, 1, 3)
    q_aug = jnp.concatenate([q.reshape(hs), jnp.ones(hs[:3] + (3,), BF16), fill], axis=-1).transpose(0, 2, 3, 1)
    v_t = vb.reshape(b, nt, tk, N_HEADS, HEAD_DIM).transpose(0, 3, 1, 4, 2)
    return pl.pallas_call(
        functools.partial(_fox_attn_t_kernel, tq=tq, tk=tk),
        grid=(b, N_HEADS // 2, s // tq),
        in_specs=[pl.BlockSpec((1, 2, LANES, tq), lambda i, p, qi: (i, p, 0, qi)),
                  pl.BlockSpec((1, 2, s, LANES), lambda i, p, qi: (i, p, 0, 0)),
                  pl.BlockSpec((1, 2, nt, HEAD_DIM, tk), lambda i, p, qi: (i, p, 0, 0, 0))],
        out_specs=pl.BlockSpec((1, 2, HEAD_DIM, tq), lambda i, p, qi: (i, p, 0, qi)),
        out_shape=jax.ShapeDtypeStruct((b, N_HEADS, HEAD_DIM, s), F32),
        compiler_params=_params("parallel", "parallel", "arbitrary"),
        name="fox_attention",
    )(q_aug, k_aug, v_t)


def _page_spec(shape, slot_block, n_pages):
    nd = len(shape)
    return pl.BlockSpec((1,) + shape, lambda i, p, pt: (pt[i * n_pages + p], slot_block) + (0,) * (nd - 1))


def _nsa_cmp_sample_kernel(*refs, n_pages):
    pt_ref = refs[0]
    pages = refs[1:1 + n_pages]
    pe_ref, w1_ref, w2_ref, o_ref, x_scr = refs[1 + n_pages:]
    del pt_ref
    rows = n_pages * N_GROUPS
    for j in range(n_pages):
        for slot in range(2):
            for g in range(N_GROUPS):
                r0 = (g * n_pages + j) * HEAD_DIM
                x_scr[slot, r0:r0 + HEAD_DIM, :] = pages[j][0, slot, g]
    for slot in range(2):
        acc = jnp.zeros((rows, LANES), F32)
        for dd in range(HEAD_DIM):
            xl = x_scr[slot, pl.ds(dd, rows, stride=HEAD_DIM), :] + pe_ref[slot, dd:dd + 1, :]
            acc = acc + _dot(xl.astype(BF16), w1_ref[slot, dd])
        hid = acc * jax.nn.sigmoid(acc)
        o_ref[0, slot] = _dot(hid.astype(BF16), w2_ref[slot])


def _nsa_cmp_sample(pool_t, pt, pe, w1, w2, db, n_pages):
    per = PAGE // BLOCK
    pe_t = jnp.tile(pe.transpose(0, 2, 1), (1, 1, per))
    w1_d = _block_diag(w1.transpose(0, 2, 1, 3), per).astype(BF16)
    w2_d = _block_diag(w2, per).astype(BF16)
    rows = n_pages * N_GROUPS
    page = lambda j: pl.BlockSpec((1, 2, N_GROUPS, HEAD_DIM, PAGE),
                                  lambda i, pt, j=j: (pt[i * n_pages + j], 0, 0, 0, 0))
    full = lambda a: pl.BlockSpec(a.shape, lambda i, pt: (0,) * a.ndim)
    return pl.pallas_call(
        functools.partial(_nsa_cmp_sample_kernel, n_pages=n_pages),
        grid_spec=pltpu.PrefetchScalarGridSpec(
            num_scalar_prefetch=1, grid=(db,),
            in_specs=[page(j) for j in range(n_pages)] + [full(pe_t), full(w1_d), full(w2_d)],
            out_specs=pl.BlockSpec((1, 2, rows, per * HEAD_DIM), lambda i, pt: (i, 0, 0, 0)),
            scratch_shapes=[pltpu.VMEM((2, rows * HEAD_DIM, PAGE), F32)]),
        out_shape=jax.ShapeDtypeStruct((db, 2, rows, per * HEAD_DIM), F32),
        compiler_params=_params("arbitrary"),
        name="nsa_compress_sample",
    )(pt, *([pool_t] * n_pages), pe_t, w1_d, w2_d)


def _nsa_sel_sample_kernel(qc_ref, qr_ref, kc_ref, vc_ref, win_ref, kn_ref, vn_ref,
                           oc_ref, ow_ref, sel_ref, *, past, dt, nbs):
    nr = qc_ref.shape[1]
    per_r = dt * N_GROUPS
    row = lax.broadcasted_iota(jnp.int32, (nr, 1), 0)
    t_pos = past + (row // N_GROUPS) % dt

    kc = kc_ref[0]
    nb = kc.shape[0]
    n = lax.broadcasted_iota(jnp.int32, (nr, nb), 1)
    cmask = n * BLOCK + (BLOCK - 1) <= t_pos
    sc = jnp.where(cmask, _dot_nt(qc_ref[0], kc), NEG)
    e = jnp.where(cmask, jnp.exp(sc - jnp.max(sc, axis=-1, keepdims=True)), 0.0)
    l = jnp.sum(e, axis=-1, keepdims=True)
    pc = e / jnp.where(l > 0.0, l, 1.0)
    oc_ref[0] = _dot(pc.astype(BF16), vc_ref[0])

    imp = pc[0:per_r]
    for r in range(1, GROUP):
        imp = imp + pc[r * per_r:(r + 1) * per_r]
    n1 = lax.broadcasted_iota(jnp.int32, (per_r, nb), 1)
    t1 = past + lax.broadcasted_iota(jnp.int32, (per_r, 1), 0) // N_GROUPS
    cur = t1 // BLOCK
    forced = (n1 == 0) | (n1 == cur) | (n1 == cur - 1)
    dead = (n1 * BLOCK > t1) | (n1 >= nbs)
    sel = _top_blocks(jnp.where(forced, jnp.inf, jnp.where(dead, -jnp.inf, imp)), min(N_SELECT, nbs))
    sel_ref[0] = jnp.concatenate([sel] * GROUP, axis=0).astype(BF16)

    qr = qr_ref[0]
    wb = win_ref.shape[-1]
    kw = win_ref[0, 0].reshape(N_GROUPS * HEAD_DIM, wb).astype(BF16)
    vw = win_ref[0, 1].reshape(N_GROUPS * HEAD_DIM, wb).astype(BF16)
    d_old = t_pos - (past - wb + lax.broadcasted_iota(jnp.int32, (nr, wb), 1))
    s_old = jnp.where((d_old >= 0) & (d_old < WINDOW), _dot(qr, kw), NEG)
    tn = lax.broadcasted_iota(jnp.int32, (nr, LANES), 1)
    d_new = t_pos - (past + tn)
    s_new = jnp.where((d_new >= 0) & (tn < dt), _dot(qr, kn_ref[0]), NEG)
    m = jnp.maximum(jnp.max(s_old, axis=-1, keepdims=True), jnp.max(s_new, axis=-1, keepdims=True))
    p_old = jnp.exp(s_old - m)
    p_new = jnp.exp(s_new - m)
    lw = jnp.sum(p_old, axis=-1, keepdims=True) + jnp.sum(p_new, axis=-1, keepdims=True)
    ow_ref[0] = (_dot_nt(p_old.astype(BF16), vw) + _dot_nt(p_new.astype(BF16), vn_ref[0])) / lw


def _nsa_sel_sample(qbd_c, qbd_r, kc, vc, win_t, kn_t, vn_t, past, dt, nbs):
    db, nr, gd = qbd_c.shape
    blk = lambda a: pl.BlockSpec((1,) + a.shape[1:], lambda i: (i,) + (0,) * (a.ndim - 1))
    args = (qbd_c, qbd_r, kc, vc, win_t, kn_t, vn_t)
    return pl.pallas_call(
        functools.partial(_nsa_sel_sample_kernel, past=past, dt=dt, nbs=nbs),
        grid=(db,),
        in_specs=[blk(a) for a in args],
        out_specs=[pl.BlockSpec((1, nr, gd), lambda i: (i, 0, 0)), pl.BlockSpec((1, nr, gd), lambda i: (i, 0, 0)),
                   pl.BlockSpec((1, nr, LANES), lambda i: (i, 0, 0))],
        out_shape=[jax.ShapeDtypeStruct((db, nr, gd), F32), jax.ShapeDtypeStruct((db, nr, gd), F32),
                   jax.ShapeDtypeStruct((db, nr, LANES), BF16)],
        compiler_params=_params("parallel"),
        name="nsa_select_window_sample",
    )(*args)


def _online_update(s, v_t, m_scr, l_scr, acc_scr):
    m_i = m_scr[...]
    m_n = jnp.maximum(m_i, jnp.max(s, axis=-1, keepdims=True))
    alpha = jnp.exp(m_i - m_n)
    p = jnp.exp(s - m_n)
    l_scr[...] = alpha * l_scr[...] + jnp.sum(p, axis=-1, keepdims=True)
    acc_scr[...] = alpha * acc_scr[...] + _dot_nt(p.astype(BF16), v_t)
    m_scr[...] = m_n


def _nsa_slc_sample_kernel(pt_ref, page_ref, q_ref, sel_ref, e_ref, en_ref, kn_ref, vn_ref, oc_ref, ow_ref, gt_ref,
                           o_ref, m_scr, l_scr, acc_scr, *, past, dt):
    del pt_ref
    p = pl.program_id(1)
    gd = N_GROUPS * HEAD_DIM

    @pl.when(p == 0)
    def _():
        m_scr[...] = jnp.full(m_scr.shape, NEG, F32)
        l_scr[...] = jnp.zeros_like(l_scr)
        acc_scr[...] = jnp.zeros_like(acc_scr)

    q = q_ref[0]
    sel = sel_ref[0]
    k_t = page_ref[0, 0].reshape(gd, PAGE).astype(BF16)
    v_t = page_ref[0, 1].reshape(gd, PAGE).astype(BF16)
    s = jnp.where(_dot(sel, e_ref[0]) > 0.5, _dot(q, k_t), NEG)
    _online_update(s, v_t, m_scr, l_scr, acc_scr)

    @pl.when(p == pl.num_programs(1) - 1)
    def _():
        nr = q.shape[0]
        row = lax.broadcasted_iota(jnp.int32, (nr, 1), 0)
        t_row = (row // N_GROUPS) % dt
        tn = lax.broadcasted_iota(jnp.int32, (nr, LANES), 1)
        ok = (_dot(sel, en_ref[...]) > 0.5) & (tn <= t_row) & (tn < dt)
        _online_update(jnp.where(ok, _dot(q, kn_ref[0]), NEG), vn_ref[0], m_scr, l_scr, acc_scr)
        gt = gt_ref[0]
        o_ref[0] = (gt[:, 0:1] * oc_ref[0] + gt[:, 1:2] * (acc_scr[...] / l_scr[...]) + gt[:, 2:3] * ow_ref[0])


def _nsa_slc_sample(pool_t, pt, qbd_r, sel, kn_t, vn_t, o_c, o_w, gates, past, dt, n_pages):
    db, nr, gd = qbd_r.shape
    per = PAGE // BLOCK
    key_blk = jnp.arange(n_pages * PAGE) // BLOCK
    e_tab = (jnp.arange(LANES)[None, :, None] == key_blk.reshape(n_pages, 1, PAGE)).astype(BF16)
    new_blk = jnp.where(jnp.arange(LANES) < dt, (past + jnp.arange(LANES)) // BLOCK, -1)
    e_new = (jnp.arange(LANES)[:, None] == new_blk[None, :]).astype(BF16)
    del per
    blk = lambda a: pl.BlockSpec((1,) + a.shape[1:], lambda i, p, pt: (i,) + (0,) * (a.ndim - 1))
    return pl.pallas_call(
        functools.partial(_nsa_slc_sample_kernel, past=past, dt=dt),
        grid_spec=pltpu.PrefetchScalarGridSpec(
            num_scalar_prefetch=1, grid=(db, n_pages),
            in_specs=[_page_spec((2, N_GROUPS, HEAD_DIM, PAGE), 1, n_pages), blk(qbd_r), blk(sel),
                      pl.BlockSpec((1, LANES, PAGE), lambda i, p, pt: (p, 0, 0)),
                      pl.BlockSpec(e_new.shape, lambda i, p, pt: (0, 0)),
                      blk(kn_t), blk(vn_t), blk(o_c), blk(o_w), blk(gates)],
            out_specs=pl.BlockSpec((1, nr, gd), lambda i, p, pt: (i, 0, 0)),
            scratch_shapes=[pltpu.VMEM((nr, 1), F32), pltpu.VMEM((nr, 1), F32), pltpu.VMEM((nr, gd), F32)]),
        out_shape=jax.ShapeDtypeStruct((db, nr, gd), F32),
        compiler_params=_params("parallel", "arbitrary"),
        name="nsa_selected_sample",
    )(pt, pool_t, qbd_r, sel, e_tab, e_new, kn_t, vn_t, o_c, o_w, gates)


def _fox_sample_kernel(pt_ref, page_ref, lf_ref, q_ref, kn_ref, vn_ref, lfn_ref, og_ref,
                       o_ref, m_scr, l_scr, acc_scr, c_scr, *, dt):
    del pt_ref
    p = pl.program_id(1)
    hd = N_HEADS * HEAD_DIM

    @pl.when(p == 0)
    def _():
        m_scr[...] = jnp.full(m_scr.shape, NEG, F32)
        l_scr[...] = jnp.zeros_like(l_scr)
        acc_scr[...] = jnp.zeros_like(acc_scr)
        c_scr[...] = jnp.zeros_like(c_scr)

    tri = (lax.broadcasted_iota(jnp.int32, (PAGE, PAGE), 0) <= lax.broadcasted_iota(jnp.int32, (PAGE, PAGE), 1)).astype(BF16)

    def cum(lf):
        hi, mid, lo = _split3(lf)
        return _dot(hi, tri) + _dot(mid, tri) + _dot(lo, tri) + c_scr[...]

    q = q_ref[0]
    c_page = cum(lf_ref[0])
    k_t = page_ref[0, 0].reshape(hd, PAGE).astype(BF16)
    v_t = page_ref[0, 1].reshape(hd, PAGE).astype(BF16)
    s = _dot(q, k_t) - jnp.concatenate([c_page] * dt, axis=0)
    _online_update(s, v_t, m_scr, l_scr, acc_scr)
    c_scr[...] = jnp.broadcast_to(c_page[:, PAGE - 1:PAGE], c_scr.shape)

    @pl.when(p == pl.num_programs(1) - 1)
    def _():
        nr = q.shape[0]
        t_row = lax.broadcasted_iota(jnp.int32, (nr, 1), 0) // N_HEADS
        tn = lax.broadcasted_iota(jnp.int32, (nr, LANES), 1)
        c_new = cum(lfn_ref[0])
        s_n = _dot(q, kn_ref[0]) - jnp.concatenate([c_new] * dt, axis=0)
        s_n = jnp.where((tn <= t_row) & (tn < dt), s_n, NEG)
        _online_update(s_n, vn_ref[0], m_scr, l_scr, acc_scr)
        o_ref[0] = acc_scr[...] / l_scr[...] * og_ref[0]


def _fox_sample(pool_t, lf_pool_t, pt, qbd, kn_t, vn_t, lfn_t, og_t, dt, n_pages):
    db, nr, hd = qbd.shape
    blk = lambda a: pl.BlockSpec((1,) + a.shape[1:], lambda i, p, pt: (i,) + (0,) * (a.ndim - 1))
    return pl.pallas_call(
        functools.partial(_fox_sample_kernel, dt=dt),
        grid_spec=pltpu.PrefetchScalarGridSpec(
            num_scalar_prefetch=1, grid=(db, n_pages),
            in_specs=[_page_spec((2, N_HEADS, HEAD_DIM, PAGE), 0, n_pages),
                      pl.BlockSpec((1, N_HEADS, PAGE), lambda i, p, pt: (pt[i * n_pages + p], 0, 0)),
                      blk(qbd), blk(kn_t), blk(vn_t), blk(lfn_t), blk(og_t)],
            out_specs=pl.BlockSpec((1, nr, hd), lambda i, p, pt: (i, 0, 0)),
            scratch_shapes=[pltpu.VMEM((nr, 1), F32), pltpu.VMEM((nr, 1), F32), pltpu.VMEM((nr, hd), F32),
                            pltpu.VMEM((N_HEADS, PAGE), F32)]),
        out_shape=jax.ShapeDtypeStruct((db, nr, hd), F32),
        compiler_params=_params("parallel", "arbitrary"),
        name="fox_attention_sample",
    )(pt, pool_t, lf_pool_t, qbd, kn_t, vn_t, lfn_t, og_t)


def _page_specs(block, slot_block, n_pages):
    nd = len(block)
    return [pl.BlockSpec((1,) + block, lambda i, pt, j=j: (pt[i * n_pages + j], slot_block) + (0,) * (nd - 1))
            for j in range(n_pages)]


def _softmax_chunks(scores):
    m = scores[0]
    for s in scores[1:]:
        m = jnp.maximum(m, s)
    m = jnp.max(m, axis=-1, keepdims=True)
    ps = [jnp.exp(s - m) for s in scores]
    tot = ps[0]
    for p in ps[1:]:
        tot = tot + p
    return ps, jnp.sum(tot, axis=-1, keepdims=True)


def _nsa_attn_sample_kernel(*refs, n_pages, past, dt, nbs):
    pages = refs[1:1 + n_pages]
    (qc_ref, qr_ref, kc_ref, vc_ref, win_ref, kwn_ref, vwn_ref, ksn_ref, vsn_ref, e_ref, en_ref, gt_ref,
     o_ref) = refs[1 + n_pages:]
    gd = N_GROUPS * HEAD_DIM
    nr = qc_ref.shape[1]
    per_r = dt * N_GROUPS
    row = lax.broadcasted_iota(jnp.int32, (nr, 1), 0)
    t_row = (row // N_GROUPS) % dt
    t_pos = past + t_row
    tn = lax.broadcasted_iota(jnp.int32, (nr, LANES), 1)

    kc = kc_ref[0]
    nb = kc.shape[0]
    n = lax.broadcasted_iota(jnp.int32, (nr, nb), 1)
    cmask = n * BLOCK + (BLOCK - 1) <= t_pos
    sc = jnp.where(cmask, _dot_nt(qc_ref[0], kc), NEG)
    e = jnp.where(cmask, jnp.exp(sc - jnp.max(sc, axis=-1, keepdims=True)), 0.0)
    l = jnp.sum(e, axis=-1, keepdims=True)
    pc = e / jnp.where(l > 0.0, l, 1.0)
    o_c = _dot(pc.astype(BF16), vc_ref[0])

    imp = pc[0:per_r]
    for r in range(1, GROUP):
        imp = imp + pc[r * per_r:(r + 1) * per_r]
    n1 = lax.broadcasted_iota(jnp.int32, (per_r, nb), 1)
    t1 = past + lax.broadcasted_iota(jnp.int32, (per_r, 1), 0) // N_GROUPS
    cur = t1 // BLOCK
    forced = (n1 == 0) | (n1 == cur) | (n1 == cur - 1)
    dead = (n1 * BLOCK > t1) | (n1 >= nbs)
    sel = _top_blocks(jnp.where(forced, jnp.inf, jnp.where(dead, -jnp.inf, imp)), min(N_SELECT, nbs))
    selb = jnp.concatenate([sel] * GROUP, axis=0).astype(BF16)

    qr = qr_ref[0]
    wb = win_ref.shape[-1]
    kw = win_ref[0, 0].reshape(gd, wb).astype(BF16)
    vw = win_ref[0, 1].reshape(gd, wb).astype(BF16)
    d_old = t_pos - (past - wb + lax.broadcasted_iota(jnp.int32, (nr, wb), 1))
    s_old = jnp.where((d_old >= 0) & (d_old < WINDOW), _dot(qr, kw), NEG)
    s_new = jnp.where((tn <= t_row) & (tn < dt), _dot(qr, kwn_ref[0]), NEG)
    m = jnp.maximum(jnp.max(s_old, axis=-1, keepdims=True), jnp.max(s_new, axis=-1, keepdims=True))
    p_old = jnp.exp(s_old - m)
    p_new = jnp.exp(s_new - m)
    lw = jnp.sum(p_old, axis=-1, keepdims=True) + jnp.sum(p_new, axis=-1, keepdims=True)
    o_w = (_dot_nt(p_old.astype(BF16), vw) + _dot_nt(p_new.astype(BF16), vwn_ref[0])) / lw

    scores = []
    for j in range(n_pages):
        k_t = pages[j][0, 0].reshape(gd, PAGE).astype(BF16)
        scores.append(jnp.where(_dot(selb, e_ref[j]) > 0.5, _dot(qr, k_t), NEG))
    ok = (_dot(selb, en_ref[...]) > 0.5) & (tn <= t_row) & (tn < dt)
    scores.append(jnp.where(ok, _dot(qr, ksn_ref[0]), NEG))
    ps, ls = _softmax_chunks(scores)
    acc = _dot_nt(ps[n_pages].astype(BF16), vsn_ref[0])
    for j in range(n_pages):
        acc = acc + _dot_nt(ps[j].astype(BF16), pages[j][0, 1].reshape(gd, PAGE).astype(BF16))
    gt = gt_ref[0]
    o_ref[0] = gt[:, 0:1] * o_c + gt[:, 1:2] * (acc / ls) + gt[:, 2:3] * o_w


def _nsa_attn_sample(pool_t, pt, qbd_c, qbd_r, kc, vc, win_t, kwn, vwn, ksn, vsn, gates, past, dt, nbs, n_pages):
    db, nr, gd = qbd_r.shape
    key_blk = jnp.arange(n_pages * PAGE) // BLOCK
    e_tab = (jnp.arange(LANES)[None, :, None] == key_blk.reshape(n_pages, 1, PAGE)).astype(BF16)
    new_blk = jnp.where(jnp.arange(LANES) < dt, (past + jnp.arange(LANES)) // BLOCK, -1)
    e_new = (jnp.arange(LANES)[:, None] == new_blk[None, :]).astype(BF16)
    blk = lambda a: pl.BlockSpec((1,) + a.shape[1:], lambda i, pt: (i,) + (0,) * (a.ndim - 1))
    full = lambda a: pl.BlockSpec(a.shape, lambda i, pt: (0,) * a.ndim)
    per_batch = (qbd_c, qbd_r, kc, vc, win_t, kwn, vwn, ksn, vsn)
    return pl.pallas_call(
        functools.partial(_nsa_attn_sample_kernel, n_pages=n_pages, past=past, dt=dt, nbs=nbs),
        grid_spec=pltpu.PrefetchScalarGridSpec(
            num_scalar_prefetch=1, grid=(db,),
            in_specs=(_page_specs((2, N_GROUPS, HEAD_DIM, PAGE), 1, n_pages) + [blk(a) for a in per_batch]
                      + [full(e_tab), full(e_new), blk(gates)]),
            out_specs=pl.BlockSpec((1, nr, gd), lambda i, pt: (i, 0, 0))),
        out_shape=jax.ShapeDtypeStruct((db, nr, gd), F32),
        compiler_params=_params("parallel"),
        name="nsa_attention_sample",
    )(pt, *([pool_t] * n_pages), *per_batch, e_tab, e_new, gates)


def _fox_attn_sample_kernel(*refs, n_pages, dt):
    pages = refs[1:1 + n_pages]
    lfs = refs[1 + n_pages:1 + 2 * n_pages]
    q_ref, kn_ref, vn_ref, lfn_ref, og_ref, o_ref = refs[1 + 2 * n_pages:]
    hd = N_HEADS * HEAD_DIM
    tri = (lax.broadcasted_iota(jnp.int32, (PAGE, PAGE), 0) <= lax.broadcasted_iota(jnp.int32, (PAGE, PAGE), 1)).astype(BF16)

    def local_cum(lf):
        hi, mid, lo = _split3(lf)
        return _dot(hi, tri) + _dot(mid, tri) + _dot(lo, tri)

    q = q_ref[0]
    nr = q.shape[0]
    scores = []
    prefix = jnp.zeros((N_HEADS, 1), F32)
    for j in range(n_pages):
        loc = local_cum(lfs[j][0])
        c_page = loc + prefix
        prefix = prefix + loc[:, PAGE - 1:PAGE]
        k_t = pages[j][0, 0].reshape(hd, PAGE).astype(BF16)
        scores.append(_dot(q, k_t) - jnp.concatenate([c_page] * dt, axis=0))
    c_new = local_cum(lfn_ref[0]) + prefix
    t_row = lax.broadcasted_iota(jnp.int32, (nr, 1), 0) // N_HEADS
    tn = lax.broadcasted_iota(jnp.int32, (nr, LANES), 1)
    s_n = _dot(q, kn_ref[0]) - jnp.concatenate([c_new] * dt, axis=0)
    scores.append(jnp.where((tn <= t_row) & (tn < dt), s_n, NEG))
    ps, ls = _softmax_chunks(scores)
    acc = _dot_nt(ps[n_pages].astype(BF16), vn_ref[0])
    for j in range(n_pages):
        acc = acc + _dot_nt(ps[j].astype(BF16), pages[j][0, 1].reshape(hd, PAGE).astype(BF16))
    o_ref[0] = acc / ls * og_ref[0]


def _fox_attn_sample(pool_t, lf_pool_t, pt, qbd, kn_t, vn_t, lfn_t, og_t, dt, n_pages):
    db, nr, hd = qbd.shape
    blk = lambda a: pl.BlockSpec((1,) + a.shape[1:], lambda i, pt: (i,) + (0,) * (a.ndim - 1))
    lf_specs = [pl.BlockSpec((1, N_HEADS, PAGE), lambda i, pt, j=j: (pt[i * n_pages + j], 0, 0))
                for j in range(n_pages)]
    per_batch = (qbd, kn_t, vn_t, lfn_t, og_t)
    return pl.pallas_call(
        functools.partial(_fox_attn_sample_kernel, n_pages=n_pages, dt=dt),
        grid_spec=pltpu.PrefetchScalarGridSpec(
            num_scalar_prefetch=1, grid=(db,),
            in_specs=_page_specs((2, N_HEADS, HEAD_DIM, PAGE), 0, n_pages) + lf_specs + [blk(a) for a in per_batch],
            out_specs=pl.BlockSpec((1, nr, hd), lambda i, pt: (i, 0, 0))),
        out_shape=jax.ShapeDtypeStruct((db, nr, hd), F32),
        compiler_params=_params("parallel"),
        name="fox_attention_sample",
    )(pt, *([pool_t] * n_pages), *([lf_pool_t] * n_pages), *per_batch)


def _per_batch(a, dt, db):
    return a.reshape(dt, db, -1).transpose(1, 0, 2)


def _new_keys_t(a, dt, db):
    a = _per_batch(a, dt, db).transpose(0, 2, 1)
    return jnp.pad(a, ((0, 0), (0, 0), (0, LANES - dt))).astype(a.dtype)


def _nsa_qbd(q, dt, db):
    q5 = q.reshape(dt, db, N_GROUPS, GROUP, HEAD_DIM).transpose(1, 3, 0, 2, 4)
    eye = jnp.eye(N_GROUPS, dtype=q.dtype)
    out = q5[:, :, :, :, None, :] * eye[None, None, None, :, :, None]
    return out.reshape(db, GROUP * dt * N_GROUPS, N_GROUPS * HEAD_DIM)


def _nsa_undiag(o, dt, db):
    o6 = o.reshape(db, GROUP, dt, N_GROUPS, N_GROUPS, HEAD_DIM)
    dg = jnp.diagonal(o6, axis1=3, axis2=4)
    return dg.transpose(2, 0, 4, 1, 3).reshape(dt * db, N_HEADS * HEAD_DIM)


def _fox_qbd(q, dt, db):
    q4 = q.reshape(dt, db, N_HEADS, HEAD_DIM).transpose(1, 0, 2, 3)
    eye = jnp.eye(N_HEADS, dtype=q.dtype)
    out = q4[:, :, :, None, :] * eye[None, None, :, :, None]
    return out.reshape(db, dt * N_HEADS, N_HEADS * HEAD_DIM)


def _fox_undiag(o, dt, db):
    o5 = o.reshape(db, dt, N_HEADS, N_HEADS, HEAD_DIM)
    dg = jnp.diagonal(o5, axis1=2, axis2=3)
    return dg.transpose(1, 0, 3, 2).reshape(dt * db, N_HEADS * HEAD_DIM)


def _prompt_mods(mod, b):
    return [m.reshape(b, 1, -1) for m in jnp.split(mod[:b], 6, axis=-1)]


def _sample_mods(mod, b, t):
    return [jnp.tile(m, (t, 1))[None] for m in jnp.split(mod[b:], 6, axis=-1)]


def kernel(x_prompt, x_sample, cache_nsa_kv, cache_nsa_win, cache_fox_kv, cache_fox_logf, state_ffn_conv, page_table, c_prompt, c_sample, w_ada, b_ada, norm_mix_g, norm_ffn_g, w_nsa_in, pe_cmp, w_cmp1, w_cmp2, w_nsa_out, w_fox_in, b_fox_f, fox_q_norm_g, fox_k_norm_g, w_fox_out, w_ffn_up, ffn_conv_w, ffn_conv_b, w_ffn_down, final_norm_g):
    b, s, d = x_prompt.shape
    db, dt, _ = x_sample.shape
    f_dim = w_ffn_down.shape[1]
    depth = w_ada.shape[0]
    tm = 256
    tpb = s // tm
    tm_f = 512
    tpb_f = s // tm_f

    n_pages = page_table.shape[1]
    past = n_pages * PAGE
    nbs = -(-(past + dt) // BLOCK)
    per = PAGE // BLOCK
    r_s = dt * db
    pt = page_table.reshape(-1).astype(jnp.int32)
    key_last = (0, 2, 3, 4, 1)

    c_all = jnp.concatenate([c_prompt, c_sample], axis=0)
    xp = x_prompt.reshape(b * s, d)
    xs = x_sample.transpose(1, 0, 2).reshape(r_s, d)
    tabs_p = _rope_tables(jnp.arange(s, dtype=jnp.int32))
    tabs_s = _rope_tables(past + jnp.arange(r_s, dtype=jnp.int32) // db)

    nsa_kv_p, nsa_win_p, fox_kv_p, fox_lf_p, conv_p = [], [], [], [], []
    nsa_kv_s, nsa_win_s, fox_kv_s, fox_lf_s, conv_s = [], [], [], [], []
    y_prompt = y_sample = None
    for i in range(depth):
        j = i // 2
        mod = _adaln(c_all, w_ada[i].astype(BF16), b_ada[i][None])
        mp = _prompt_mods(mod, b)
        ms = _sample_mods(mod, b, dt)
        g_mix = norm_mix_g[i][None]
        if i % 2 == 0:
            w_in = _nsa_weight(w_nsa_in[j])
            qc, qr, rows, win, dup, gates = _nsa_proj(xp, g_mix, mp[1], mp[0], w_in, tabs_p, "prompt", tm, tpb)
            cdup = _compress_prompt(rows, *_cmp_weights(pe_cmp[j], w_cmp1[j], w_cmp2[j]), b, s)
            attn_p = _nsa_attention_t(qc, qr, cdup, dup, gates, b, s)
            og_p = None
            w_out = w_nsa_out[j].astype(BF16)
            seq_last = lambda a, lead: a.transpose(0, 2, 1).reshape((b,) + lead + (a.shape[1],)).transpose(0, 4, 1, 2, 3)
            nsa_kv_p.append(seq_last(rows.reshape(b, s, -1), (4, N_GROUPS, HEAD_DIM)))
            wl = min(WINDOW, s)
            nsa_win_p.append(seq_last(win.reshape(b, s, -1)[:, s - wl:], (2, N_GROUPS, HEAD_DIM)))

            qc, qr, rows, win, _, gates = _nsa_proj(xs, g_mix, ms[1], ms[0], w_in, tabs_s, "sample", r_s, 1)
            nsa_kv_s.append(rows.reshape(dt, db, 4, N_GROUPS, HEAD_DIM).transpose(1, 0, 2, 3, 4))
            nsa_win_s.append(win.reshape(dt, db, 2, N_GROUPS, HEAD_DIM).transpose(1, 0, 2, 3, 4))
            pool_t = jnp.transpose(cache_nsa_kv[j], key_last)
            kc2 = _nsa_cmp_sample(pool_t, pt, pe_cmp[j], w_cmp1[j], w_cmp2[j], db, n_pages)
            kc = kc2.reshape(db, 2, N_GROUPS, n_pages, per, HEAD_DIM).transpose(0, 1, 3, 4, 2, 5)
            kc = kc.reshape(db, 2, n_pages * per, N_GROUPS * HEAD_DIM)
            kc = jnp.pad(kc, ((0, 0), (0, 0), (0, LANES - n_pages * per), (0, 0))).astype(BF16)
            qbd_c, qbd_r = _nsa_qbd(qc, dt, db), _nsa_qbd(qr, dt, db)
            gd = N_GROUPS * HEAD_DIM
            newt = lambda a: _new_keys_t(a.astype(BF16), dt, db)
            g_s = gates.reshape(dt, db, N_GROUPS, LANES)[..., :GROUP * 3].reshape(dt, db, N_GROUPS, GROUP, 3)
            g_s = g_s.transpose(1, 3, 0, 2, 4).reshape(db, GROUP * dt * N_GROUPS, 3)
            g_s = jnp.pad(g_s, ((0, 0), (0, 0), (0, LANES - 3)))
            o_s = _nsa_attn_sample(pool_t, pt, qbd_c, qbd_r, kc[:, 0], kc[:, 1],
                                   jnp.transpose(cache_nsa_win[j], key_last), newt(win[:, :gd]), newt(win[:, gd:]),
                                   newt(rows[:, 2 * gd:3 * gd]), newt(rows[:, 3 * gd:]), g_s, past, dt, nbs, n_pages)
            attn_s = _nsa_undiag(o_s, dt, db).astype(BF16)
        else:
            w_in = _fox_weight(w_fox_in[j])
            fox = lambda x, sc, sh, mode, t, n: _fox_proj(x, g_mix, sc, sh, w_in, fox_q_norm_g[j], fox_k_norm_g[j],
                                                         b_fox_f[j], mode, t, n)
            q, kf, kb, vf, vb, og, lf, c = fox(xp, mp[1], mp[0], "prompt", tm, tpb)
            attn_p = _fox_attention(q, kb, vb, c, og, b, s).reshape(b * s, d)
            og_p = None
            w_out = w_fox_out[j].astype(BF16)
            hs = (N_HEADS, HEAD_DIM)
            kv_t = jnp.stack([kf.reshape(b, s, -1).transpose(0, 2, 1), vf.reshape(b, s, -1).transpose(0, 2, 1)], axis=1)
            fox_kv_p.append(kv_t.reshape((b, 2) + hs + (s,)).transpose(0, 4, 1, 2, 3))
            fox_lf_p.append(lf[:, :N_HEADS].reshape(b, s, N_HEADS))

            q, kf, kb, vf, vb, og, lf, _ = fox(xs, ms[1], ms[0], "sample", r_s, 1)
            fox_kv_s.append(jnp.stack([kf.reshape((dt, db) + hs), vf.reshape((dt, db) + hs)], axis=2).transpose(1, 0, 2, 3, 4))
            fox_lf_s.append(lf[:, :N_HEADS].reshape(dt, db, N_HEADS).transpose(1, 0, 2))
            og_t = jnp.repeat(_per_batch(og, dt, db)[:, :, None, :], N_HEADS, axis=2).reshape(db, dt * N_HEADS, -1)
            o_full = _fox_attn_sample(jnp.transpose(cache_fox_kv[j], key_last),
                                      jnp.transpose(cache_fox_logf[j], (0, 2, 1)), pt, _fox_qbd(q, dt, db),
                                      _new_keys_t(kb, dt, db), _new_keys_t(vb, dt, db),
                                      _new_keys_t(lf[:, :N_HEADS], dt, db), og_t, dt, n_pages)
            attn_s = _fox_undiag(o_full, dt, db).astype(BF16)

        final = final_norm_g[None] if i == depth - 1 else None
        ffn_w = (norm_ffn_g[i][None], w_ffn_up[i].astype(BF16), ffn_conv_w[i], ffn_conv_b[i][None],
                 w_ffn_down[i].astype(BF16))

        res = _ffn(xp, attn_p, w_out, (mp[2], mp[4], mp[3], mp[5]), *ffn_w, "prompt", tm_f, tpb_f, 1, final_g=final,
                   og=og_p)
        xp = res[0]
        conv_p.append(res[1].reshape(b, tpb_f, 8, 2 * f_dim)[:, -1, 6:, :])
        if final is not None:
            y_prompt = res[2].reshape(b, s, d)

        state = state_ffn_conv[i].transpose(1, 0, 2).reshape(2 * db, 2 * f_dim)
        res = _ffn(xs, attn_s, w_out, (ms[2], ms[4], ms[3], ms[5]), *ffn_w, "sample", r_s, 1, db, state=state,
                   final_g=final)
        xs = res[0]
        conv_s.append(res[1].reshape(2, db, 2 * f_dim).transpose(1, 0, 2))
        if final is not None:
            y_sample = res[2].reshape(dt, db, d).transpose(1, 0, 2)

    return (y_prompt, y_sample, jnp.stack(nsa_kv_p), jnp.stack(nsa_kv_s), jnp.stack(nsa_win_p), jnp.stack(nsa_win_s),
            jnp.stack(fox_kv_p), jnp.stack(fox_kv_s), jnp.stack(fox_lf_p), jnp.stack(fox_lf_s),
            jnp.stack(conv_p), jnp.stack(conv_s))
```

```python
import functools

import jax
import jax.numpy as jnp
from jax import lax
from jax.experimental import pallas as pl
from jax.experimental.pallas import tpu as pltpu

F32 = jnp.float32
BF16 = jnp.bfloat16

HEAD_DIM = 64
N_HEADS = 16
N_GROUPS = 4
GROUP = N_HEADS // N_GROUPS
BLOCK = 64
N_SELECT = 16
WINDOW = 512
ROT_DIM = 16
ROPE_THETA = 500000.0
PAGE = 128
Q_BLOCK = 128
EPS = 1e-6
NEG = -1e30
SCALE = HEAD_DIM ** -0.5
LOG2E = 1.4426950408889634
QSCALE = SCALE * LOG2E

LANES = 128
VMEM_LIMIT = 56 * 1024 * 1024


def _params(*sem, flags=None):
    return pltpu.CompilerParams(dimension_semantics=sem, vmem_limit_bytes=VMEM_LIMIT, flags=flags)


def _dot(a, b):
    return jnp.dot(a, b, preferred_element_type=F32)


def _dot_nt(a, b):
    return lax.dot_general(a, b, (((1,), (1,)), ((), ())), preferred_element_type=F32)


def _split3(x):
    hi = x.astype(BF16)
    r1 = x - hi.astype(F32)
    mid = r1.astype(BF16)
    lo = (r1 - mid.astype(F32)).astype(BF16)
    return hi, mid, lo


def _ada_kernel(c_ref, w_ref, b_ref, o_ref):
    c = c_ref[...]
    a = (c * jax.nn.sigmoid(c)).astype(BF16)
    o_ref[...] = _dot(a, w_ref[...]) + b_ref[...]


def _adaln(c, w, b):
    r, d = c.shape
    n = w.shape[1]
    tn = n // 4
    return pl.pallas_call(
        _ada_kernel,
        grid=(n // tn,),
        in_specs=[pl.BlockSpec((r, d), lambda j: (0, 0)),
                  pl.BlockSpec((d, tn), lambda j: (0, j)),
                  pl.BlockSpec((1, tn), lambda j: (0, j))],
        out_specs=pl.BlockSpec((r, tn), lambda j: (0, j)),
        out_shape=jax.ShapeDtypeStruct((r, n), F32),
        compiler_params=_params("arbitrary"),
        name="adaln",
    )(c, w, b)


def _norm_mod(x, g, scale, shift):
    ms = jnp.mean(x * x, axis=-1, keepdims=True)
    return (x * lax.rsqrt(ms + EPS) * g) * (1.0 + scale) + shift


def _rope_tables(pos):
    freqs = ROPE_THETA ** (-jnp.arange(0, ROT_DIM, 2, dtype=F32) / ROT_DIM)
    ang = pos.astype(F32)[:, None] * freqs[None, :]
    cos, sin = jnp.cos(ang), jnp.sin(ang)
    half = ROT_DIM // 2
    one = jnp.ones((pos.shape[0], HEAD_DIM - ROT_DIM), F32)
    zero = jnp.zeros_like(one)
    zh = jnp.zeros_like(cos)
    c = jnp.concatenate([cos, cos, one], axis=1)
    s_lo = jnp.concatenate([zh, sin, zero], axis=1)
    s_hi = jnp.concatenate([-sin, zh, zero], axis=1)
    rep = LANES // HEAD_DIM
    return jnp.tile(c, (1, rep)), jnp.tile(s_lo, (1, rep)), jnp.tile(s_hi, (1, rep))


def _rope(v, c, s_lo, s_hi):
    half = ROT_DIM // 2
    return v * c + pltpu.roll(v, half, 1) * s_lo + pltpu.roll(v, LANES - half, 1) * s_hi


def _row_specs(mode, tm, tpb, d):
    if mode == "prompt":
        mod = pl.BlockSpec((1, 1, d), lambda i: (i // tpb, 0, 0))
        tab = pl.BlockSpec((tm, LANES), lambda i: (i % tpb, 0))
    else:
        mod = pl.BlockSpec((1, tm, d), lambda i: (0, 0, 0))
        tab = pl.BlockSpec((tm, LANES), lambda i: (0, 0))
    return mod, tab


NSA_DUP = 4 * N_GROUPS * LANES
NSA_W_COLS = 1024 + 6 * 256 + N_GROUPS * LANES


def _nsa_proj_kernel(x_ref, g_ref, sc_ref, sh_ref, w_ref, tc_ref, tl_ref, th_ref,
                     qc_ref, qr_ref, rows_ref, win_ref, dup_ref, gates_ref):
    h = _norm_mod(x_ref[...], g_ref[...], sc_ref[0], sh_ref[0]).astype(BF16)
    z = _dot(h, w_ref[...])
    tc, tl, th = tc_ref[...], tl_ref[...], th_ref[...]
    lo = lax.broadcasted_iota(jnp.int32, (z.shape[0], LANES), 1) < HEAD_DIM

    def chunk(j):
        return z[:, j * LANES:(j + 1) * LANES]

    def put_dup(kind, pair, v):
        vr = pltpu.roll(v, HEAD_DIM, 1)
        base = (kind * N_GROUPS + 2 * pair) * LANES
        dup_ref[:, base:base + LANES] = jnp.where(lo, v, vr).astype(BF16)
        dup_ref[:, base + LANES:base + 2 * LANES] = jnp.where(lo, vr, v).astype(BF16)

    for j in range(8):
        v = chunk(j)
        sl = slice(j * LANES, (j + 1) * LANES)
        qc_ref[:, sl] = (v * QSCALE).astype(BF16)
        qr_ref[:, sl] = (_rope(v, tc, tl, th) * QSCALE).astype(BF16)
    for j in range(4):
        rows_ref[:, j * LANES:(j + 1) * LANES] = chunk(8 + j)
    for j in range(2):
        ks = _rope(chunk(12 + j), tc, tl, th)
        vs = chunk(14 + j)
        rows_ref[:, (4 + j) * LANES:(5 + j) * LANES] = ks
        rows_ref[:, (6 + j) * LANES:(7 + j) * LANES] = vs
        put_dup(0, j, ks)
        put_dup(1, j, vs)
    for j in range(2):
        kw = _rope(chunk(16 + j), tc, tl, th)
        vw = chunk(18 + j)
        win_ref[:, j * LANES:(j + 1) * LANES] = kw
        win_ref[:, (2 + j) * LANES:(3 + j) * LANES] = vw
        put_dup(2, j, kw)
        put_dup(3, j, vw)
    for j in range(N_GROUPS):
        gates_ref[:, j * LANES:(j + 1) * LANES] = jax.nn.sigmoid(chunk(20 + j))


def _nsa_proj(x, g, scale, shift, w, tabs, mode, tm, tpb):
    r, d = x.shape
    mod, tab = _row_specs(mode, tm, tpb, d)
    row = lambda n: pl.BlockSpec((tm, n), lambda i: (i, 0))
    outs = [(1024, BF16), (1024, BF16), (1024, F32), (512, F32), (NSA_DUP, BF16), (N_GROUPS * LANES, F32)]
    return pl.pallas_call(
        _nsa_proj_kernel,
        grid=(r // tm,),
        in_specs=[row(d), pl.BlockSpec((1, d), lambda i: (0, 0)), mod, mod,
                  pl.BlockSpec(w.shape, lambda i: (0, 0)), tab, tab, tab],
        out_specs=[row(n) for n, _ in outs],
        out_shape=[jax.ShapeDtypeStruct((r, n), t) for n, t in outs],
        compiler_params=_params("parallel"),
        name="nsa_proj",
    )(x, g, scale, shift, w, *tabs)


def _nsa_weight(w_in):
    d = w_in.shape[0]
    main = w_in[:, :1024 + 6 * 256]
    gates = w_in[:, 1024 + 6 * 256:].reshape(d, N_GROUPS, GROUP * 3)
    gates = jnp.pad(gates, ((0, 0), (0, 0), (0, LANES - GROUP * 3))).reshape(d, N_GROUPS * LANES)
    return jnp.concatenate([main, gates], axis=1).astype(BF16)


def _cmp_kernel(x_ref, pe_ref, w1_ref, w2_ref, o_ref, acc_ref):
    lc = pl.program_id(2)

    @pl.when(lc == 0)
    def _():
        acc_ref[...] = jnp.zeros_like(acc_ref)

    acc = acc_ref[...]
    for l in range(x_ref.shape[1]):
        xl = (x_ref[0, l] + pe_ref[0, l:l + 1, :]).astype(BF16)
        acc = acc + _dot(xl, w1_ref[0, l])
    acc_ref[...] = acc

    @pl.when(lc == pl.num_programs(2) - 1)
    def _():
        hid = acc * jax.nn.sigmoid(acc)
        o_ref[0, 0] = _dot(hid.astype(BF16), w2_ref[0]).astype(BF16)


def _block_diag(w, n):
    eye = jnp.eye(n, dtype=w.dtype)
    out = jnp.einsum("ij,...ab->...iajb", eye, w)
    return out.reshape(w.shape[:-2] + (n * w.shape[-2], n * w.shape[-1]))


def _cmp_weights(pe, w1, w2):
    pe_t = jnp.tile(pe, (1, 1, N_GROUPS))
    w1_bd = _block_diag(w1, N_GROUPS).astype(BF16)
    w2_dup = jnp.concatenate([w2, w2], axis=-1)
    w2_bd = _block_diag(w2_dup, N_GROUPS).astype(BF16)
    return pe_t, w1_bd, w2_bd


def _compress_prompt(rows, pe_t, w1_bd, w2_bd, b, s):
    nb = s // BLOCK
    gd = N_GROUPS * HEAD_DIM
    xt = rows.reshape(b, nb, BLOCK, -1)[..., :2 * gd].transpose(0, 2, 1, 3)
    lstep = 8
    return pl.pallas_call(
        _cmp_kernel,
        grid=(b, 2, BLOCK // lstep),
        in_specs=[pl.BlockSpec((1, lstep, nb, gd), lambda i, kv, lc: (i, lc, 0, kv)),
                  pl.BlockSpec((1, lstep, gd), lambda i, kv, lc: (kv, lc, 0)),
                  pl.BlockSpec((1, lstep, gd, gd), lambda i, kv, lc: (kv, lc, 0, 0)),
                  pl.BlockSpec((1, gd, N_GROUPS * LANES), lambda i, kv, lc: (kv, 0, 0))],
        out_specs=pl.BlockSpec((1, 1, nb, N_GROUPS * LANES), lambda i, kv, lc: (i, kv, 0, 0)),
        out_shape=jax.ShapeDtypeStruct((b, 2, nb, N_GROUPS * LANES), BF16),
        scratch_shapes=[pltpu.VMEM((nb, gd), F32)],
        compiler_params=_params("parallel", "parallel", "arbitrary"),
        name="nsa_compress",
    )(xt, pe_t, w1_bd, w2_bd)


def _top_rows(v, n_sel):
    nb = v.shape[0]
    n = lax.broadcasted_iota(jnp.int32, v.shape, 0)
    sel = jnp.zeros(v.shape, F32)
    for _ in range(n_sel):
        mx = jnp.max(v, axis=0, keepdims=True)
        idx = jnp.min(jnp.where(v == mx, n, nb), axis=0, keepdims=True)
        hit = n == idx
        sel = jnp.where(hit, 1.0, sel)
        v = jnp.where(hit, -jnp.inf, v)
    return sel


def _top_blocks(v, n_sel):
    rows, nb = v.shape
    pad = -rows % LANES
    if pad:
        v = jnp.concatenate([v, jnp.zeros((pad, nb), v.dtype)], axis=0)
    return _top_rows(v.T, n_sel).T[:rows]


NSA_TK = 1024


def _nsa_attn_t_kernel(qc_ref, qr_ref, kc_ref, vc_ref, ks_ref, vs_ref, kw_ref, vw_ref, e_ref, gt_ref, o_ref,
                       *, tk, wlen):
    nq = Q_BLOCK
    ncol = GROUP * nq
    s0 = pl.program_id(2) * nq
    t_q = s0 + lax.broadcasted_iota(jnp.int32, (1, nq), 1)
    t_col = jnp.concatenate([t_q] * GROUP, axis=1)
    qc, qr = qc_ref[0, 0], qr_ref[0, 0]

    kc = kc_ref[0, 0]
    nb = kc.shape[0]
    cmask = lax.broadcasted_iota(jnp.int32, (nb, 1), 0) * BLOCK + (BLOCK - 1) <= t_col
    sc = jnp.where(cmask, _dot(kc, qc), NEG)
    e = jnp.where(cmask, jnp.exp2(sc - jnp.max(sc, axis=0, keepdims=True)), 0.0)
    l = jnp.sum(e, axis=0, keepdims=True)
    pc = e / jnp.where(l > 0.0, l, 1.0)
    o_c = _dot(vc_ref[0, 0], pc.astype(BF16))

    imp = pc[:, 0:nq]
    for r in range(1, GROUP):
        imp = imp + pc[:, r * nq:(r + 1) * nq]
    n = lax.broadcasted_iota(jnp.int32, (nb, nq), 0)
    cur = t_q // BLOCK
    forced = (n == 0) | (n == cur) | (n == cur - 1)
    future = n * BLOCK > t_q
    sel = _top_rows(jnp.where(forced, jnp.inf, jnp.where(future, -jnp.inf, imp)), min(N_SELECT, nb))
    selb = sel.astype(BF16)

    w0 = pl.multiple_of(jnp.maximum(s0 + nq - wlen, 0), nq)
    dpos = t_col - (w0 + lax.broadcasted_iota(jnp.int32, (wlen, 1), 0))
    sw = jnp.where((dpos >= 0) & (dpos < WINDOW), _dot(kw_ref[0, pl.ds(w0, wlen), :], qr), NEG)
    pw = jnp.exp2(sw - jnp.max(sw, axis=0, keepdims=True))
    lw = jnp.sum(pw, axis=0, keepdims=True)
    pwb = pw.astype(BF16)
    c0 = w0 // nq
    o_w = _dot(vw_ref[0, c0], pwb[0:nq])
    for c in range(1, wlen // nq):
        o_w = o_w + _dot(vw_ref[0, c0 + c], pwb[c * nq:(c + 1) * nq])
    o_w = o_w / lw

    sub = lax.broadcasted_iota(jnp.int32, (tk, 1), 0)

    def body(j, carry):
        m_i, l_i, acc = carry
        k0 = pl.multiple_of(j * tk, tk)
        ok = (_dot(e_ref[j], selb) > 0.5) & (k0 + sub <= t_q)
        s = jnp.where(jnp.concatenate([ok] * GROUP, axis=1), _dot(ks_ref[0, pl.ds(k0, tk), :], qr), NEG)
        m_n = jnp.maximum(m_i, jnp.max(s, axis=0, keepdims=True))
        alpha = jnp.exp2(m_i - m_n)
        p = jnp.exp2(s - m_n)
        l_n = alpha * l_i + jnp.sum(p, axis=0, keepdims=True)
        return m_n, l_n, alpha * acc + _dot(vs_ref[0, j], p.astype(BF16))

    init = (jnp.full((1, ncol), NEG, F32), jnp.zeros((1, ncol), F32), jnp.zeros((LANES, ncol), F32))
    _, l_s, acc_s = lax.fori_loop(0, (s0 + nq + tk - 1) // tk, body, init)

    gt = gt_ref[0, 0]
    out = gt[0:1] * o_c + gt[1:2] * (acc_s / l_s) + gt[2:3] * o_w
    o_ref[0, 0] = out[0:HEAD_DIM].astype(BF16)


def _nsa_attention_t(qc, qr, cdup, dup, gates, b, s):
    nb = s // BLOCK
    tk = min(NSA_TK, s)
    nt = s // tk
    wlen = min(WINDOW + Q_BLOCK, s)
    nqb = s // Q_BLOCK
    g4 = N_GROUPS
    ncol = GROUP * Q_BLOCK

    def q_t(q):
        q6 = q.reshape(b, nqb, Q_BLOCK, g4, GROUP, HEAD_DIM).transpose(0, 3, 5, 1, 4, 2)
        q6 = q6.reshape(b, g4, HEAD_DIM, nqb * ncol)
        return jnp.pad(q6, ((0, 0), (0, 0), (0, LANES - HEAD_DIM), (0, 0)))

    def v_t(kind, rows):
        v = dup.reshape(b, s, -1)[:, :, kind * g4 * LANES:(kind + 1) * g4 * LANES]
        return v.reshape(b, s // rows, rows, g4 * LANES).transpose(0, 1, 3, 2)

    vc_t = cdup[:, 1].reshape(b, nb, g4, LANES).transpose(0, 2, 3, 1)
    vs_t, vw_t = v_t(1, tk), v_t(3, Q_BLOCK)
    blk = (jnp.arange(s) // BLOCK).reshape(nt, tk, 1)
    expand = (jnp.arange(nb)[None, None, :] == blk).astype(BF16)
    g_t = gates.reshape(b, nqb, Q_BLOCK, g4, LANES)[..., :GROUP * 3].reshape(b, nqb, Q_BLOCK, g4, GROUP, 3)
    g_t = g_t.transpose(0, 3, 5, 1, 4, 2).reshape(b, g4, 3, nqb * ncol)
    g_t = jnp.pad(g_t, ((0, 0), (0, 0), (0, 5), (0, 0)))

    qspec = pl.BlockSpec((1, 1, LANES, ncol), lambda i, g, q: (i, g, 0, q))
    dspec = lambda kind: pl.BlockSpec((1, s, LANES), lambda i, g, q: (i, 0, kind * g4 + g))
    o_t = pl.pallas_call(
        functools.partial(_nsa_attn_t_kernel, tk=tk, wlen=wlen),
        grid=(b, g4, nqb),
        in_specs=[qspec, qspec,
                  pl.BlockSpec((1, 1, nb, LANES), lambda i, g, q: (i, 0, 0, g)),
                  pl.BlockSpec((1, 1, LANES, nb), lambda i, g, q: (i, g, 0, 0)),
                  dspec(0), pl.BlockSpec((1, nt, LANES, tk), lambda i, g, q: (i, 0, g, 0)),
                  dspec(2), pl.BlockSpec((1, nqb, LANES, Q_BLOCK), lambda i, g, q: (i, 0, g, 0)),
                  pl.BlockSpec(expand.shape, lambda i, g, q: (0, 0, 0)),
                  pl.BlockSpec((1, 1, 8, ncol), lambda i, g, q: (i, g, 0, q))],
        out_specs=pl.BlockSpec((1, 1, HEAD_DIM, ncol), lambda i, g, q: (i, g, 0, q)),
        out_shape=jax.ShapeDtypeStruct((b, g4, HEAD_DIM, nqb * ncol), BF16),
        compiler_params=_params("parallel", "parallel", "arbitrary"),
        name="nsa_attention",
    )(q_t(qc), q_t(qr), cdup, vc_t, dup.reshape(b, s, -1), vs_t, dup.reshape(b, s, -1), vw_t, expand, g_t)
    o6 = o_t.reshape(b, g4, HEAD_DIM, nqb, GROUP, Q_BLOCK).transpose(0, 3, 5, 1, 4, 2)
    return o6.reshape(b * s, g4 * GROUP * HEAD_DIM)


FFN_CHUNK = 256


def _ffn_kernel(*refs, u, tpb, chain, final, nf):
    (x_ref, a_ref, wo_ref, gm_ref, g_ref, sc_ref, sh_ref, gf_ref, wu_ref, cw_ref, cb_ref, wd_ref) = refs[:12]
    k = 12
    st_ref = gfin_ref = y_ref = carry_scr = None
    if not chain:
        st_ref = refs[k]
        k += 1
    if final:
        gfin_ref = refs[k]
        k += 1
    xo_ref, tail_ref = refs[k], refs[k + 1]
    k += 2
    if final:
        y_ref = refs[k]
        k += 1
    ext_scr = refs[k]
    if chain:
        carry_scr = refs[k + 1]

    fc = FFN_CHUNK
    tm = x_ref.shape[0]
    base = ext_scr.shape[0] - tm
    tail = tail_ref.shape[0]
    x1 = x_ref[...] + gm_ref[0] * _dot(a_ref[...], wo_ref[...])
    h = _norm_mod(x1, g_ref[...], sc_ref[0], sh_ref[0]).astype(BF16)
    if chain:
        first = (pl.program_id(0) % tpb) == 0
    acc = jnp.zeros((tm, x_ref.shape[1]), F32)
    f_dim = wd_ref.shape[0]

    def conv_half(f, half):
        cs = slice(half * f_dim + f * fc, half * f_dim + (f + 1) * fc)
        hs = slice(half * fc, (half + 1) * fc)
        up = _dot(h, wu_ref[:, cs])
        if chain:
            ext_scr[0:base, hs] = jnp.where(first, 0.0, carry_scr[f, :, hs])
            carry_scr[f, :, hs] = up[tm - base:, :]
        else:
            ext_scr[0:base, hs] = st_ref[:, cs]
        ext_scr[base:, hs] = up
        tail_ref[:, cs] = up[tm - tail:, :]
        cw = cw_ref[:, cs]
        return (cb_ref[:, cs] + cw[0:1] * ext_scr[base - 2 * u:base - 2 * u + tm, hs]
                + cw[1:2] * ext_scr[base - u:base - u + tm, hs] + cw[2:3] * up)

    for f in range(nf):
        a, g = conv_half(f, 0), conv_half(f, 1)
        act = (g * jax.nn.sigmoid(g) * a).astype(BF16)
        acc = acc + _dot(act, wd_ref[f * fc:(f + 1) * fc, :])
    xn = x1 + gf_ref[0] * acc
    xo_ref[...] = xn
    if final:
        ms = jnp.mean(xn * xn, axis=-1, keepdims=True)
        y_ref[...] = xn * lax.rsqrt(ms + EPS) * gfin_ref[...]


def _ffn(x, attn, w_out, mods, g, wu, cw, cb, wd, mode, tm, tpb, u, state=None, final_g=None):
    r, d = x.shape
    f2 = wu.shape[1]
    nf = f2 // (2 * FFN_CHUNK)
    chain = state is None
    final = final_g is not None
    base = 8 if chain else 2 * u
    tail = 8 if chain else 2 * u
    mod, _ = _row_specs(mode, tm, tpb, d)
    row = lambda n: pl.BlockSpec((tm, n), lambda i: (i, 0))
    full = lambda a: pl.BlockSpec(a.shape, lambda i: (0,) * a.ndim, pipeline_mode=pl.Buffered(1))
    args = [x, attn, w_out, mods[0], g, mods[1], mods[2], mods[3], wu, cw, cb, wd]
    specs = [row(d), row(d), full(w_out), mod, full(g), mod, mod, mod, full(wu), full(cw), full(cb), full(wd)]
    if not chain:
        args.append(state)
        specs.append(full(state))
    if final:
        args.append(final_g)
        specs.append(full(final_g))
    out_specs = [row(d), pl.BlockSpec((tail, f2), lambda i: (i, 0))]
    out_shape = [jax.ShapeDtypeStruct((r, d), F32), jax.ShapeDtypeStruct((r // tm * tail, f2), F32)]
    if final:
        out_specs.append(row(d))
        out_shape.append(jax.ShapeDtypeStruct((r, d), F32))
    scratch = [pltpu.VMEM((base + tm, 2 * FFN_CHUNK), F32)]
    if chain:
        scratch.append(pltpu.VMEM((nf, base, 2 * FFN_CHUNK), F32))
    return pl.pallas_call(
        functools.partial(_ffn_kernel, u=u, tpb=tpb, chain=chain, final=final, nf=nf),
        grid=(r // tm,),
        in_specs=specs, out_specs=out_specs, out_shape=out_shape, scratch_shapes=scratch,
        compiler_params=_params("arbitrary"),
        name="out_proj_ffn",
    )(*args)


FOX_W_COLS = 4 * 1024 + LANES


def _fox_proj_kernel(x_ref, g_ref, sc_ref, sh_ref, w_ref, ind_ref, indt_ref, gq_ref, gk_ref, bf_ref,
                     q_ref, kf_ref, kb_ref, vf_ref, vb_ref, og_ref, lf_ref, c_ref, carry_scr, *, tpb):
    h = _norm_mod(x_ref[...], g_ref[...], sc_ref[0], sh_ref[0]).astype(BF16)
    z = _dot(h, w_ref[...])
    tm = z.shape[0]
    ind, indt = ind_ref[...], indt_ref[...]

    def head_norm(zc, gain):
        sq = zc * zc
        hi = sq.astype(BF16)
        lo = (sq - hi.astype(F32)).astype(BF16)
        ms = (_dot(hi, ind) + _dot(lo, ind)) * (1.0 / HEAD_DIM)
        rinv = lax.rsqrt(ms + EPS)
        rh = rinv.astype(BF16)
        rl = (rinv - rh.astype(F32)).astype(BF16)
        return zc * (_dot(rh, indt) + _dot(rl, indt)) * gain

    q_ref[...] = (head_norm(z[:, 0:1024], gq_ref[...]) * QSCALE).astype(BF16)
    kn = head_norm(z[:, 1024:2048], gk_ref[...])
    kf_ref[...] = kn
    kb_ref[...] = kn.astype(BF16)
    v = z[:, 2048:3072]
    vf_ref[...] = v
    vb_ref[...] = v.astype(BF16)
    og_ref[...] = jax.nn.sigmoid(z[:, 3072:4096])
    zf = z[:, 4096:4096 + LANES] + bf_ref[...]
    lf = jnp.minimum(zf, 0.0) - jnp.log1p(jnp.exp(-jnp.abs(zf)))
    lf = jnp.where(lax.broadcasted_iota(jnp.int32, lf.shape, 1) < N_HEADS, lf, 0.0)
    lf_ref[...] = lf

    @pl.when(pl.program_id(0) % tpb == 0)
    def _():
        carry_scr[...] = jnp.zeros_like(carry_scr)

    tri = (lax.broadcasted_iota(jnp.int32, (tm, tm), 0) >= lax.broadcasted_iota(jnp.int32, (tm, tm), 1)).astype(BF16)
    hi, mid, lo = _split3(lf)
    c = _dot(tri, hi) + _dot(tri, mid) + _dot(tri, lo) + carry_scr[0:1, :]
    c_ref[...] = c
    carry_scr[0:1, :] = c[tm - 1:tm, :]


def _fox_weight(w_in):
    d = w_in.shape[0]
    return jnp.pad(w_in, ((0, 0), (0, FOX_W_COLS - w_in.shape[1]))).astype(BF16)


def _fox_proj(x, g, scale, shift, w, gq, gk, bf, mode, tm, tpb):
    r, d = x.shape
    mod, _ = _row_specs(mode, tm, tpb, d)
    row = lambda n: pl.BlockSpec((tm, n), lambda i: (i, 0))
    full = lambda a: pl.BlockSpec(a.shape, lambda i: (0,) * a.ndim)
    head_of = jnp.arange(1024) // HEAD_DIM
    ind = (head_of[:, None] == jnp.arange(LANES)[None, :]).astype(BF16)
    gq_t = jnp.tile(gq, N_HEADS)[None]
    gk_t = jnp.tile(gk, N_HEADS)[None]
    bf_p = jnp.pad(bf, (0, LANES - bf.shape[0]))[None]
    outs = [(1024, BF16), (1024, F32), (1024, BF16), (1024, F32), (1024, BF16), (1024, F32), (LANES, F32), (LANES, F32)]
    return pl.pallas_call(
        functools.partial(_fox_proj_kernel, tpb=tpb),
        grid=(r // tm,),
        in_specs=[row(d), full(g), mod, mod, full(w), full(ind), full(ind.T), full(gq_t), full(gk_t), full(bf_p)],
        out_specs=[row(n) for n, _ in outs],
        out_shape=[jax.ShapeDtypeStruct((r, n), t) for n, t in outs],
        scratch_shapes=[pltpu.VMEM((8, LANES), F32)],
        compiler_params=_params("arbitrary"),
        name="fox_proj",
    )(x, g, scale, shift, w, ind, ind.T, gq_t, gk_t, bf_p)


def _fox_attn_kernel(q_ref, k_ref, v_ref, nc_ref, og_ref, o_ref, *, t, tk, nt):
    qi = pl.program_id(2)
    q = q_ref[0]
    lo = lax.broadcasted_iota(jnp.int32, (t, LANES), 1) < HEAD_DIM
    t_q = qi * t + lax.broadcasted_iota(jnp.int32, (t, 1), 0)
    col = lax.broadcasted_iota(jnp.int32, (t, tk), 1)
    q_heads = (jnp.where(lo, q, jnp.zeros_like(q)), jnp.where(lo, jnp.zeros_like(q), q))

    def tile(j, carry, masked):
        k0 = pl.multiple_of(j * tk, tk)
        k = k_ref[0, pl.ds(k0, tk), :]
        v = v_ref[0, pl.ds(k0, tk), :]
        out = []
        for h2, (m_i, l_i, acc) in enumerate(carry):
            s = _dot_nt(q_heads[h2], k) + nc_ref[0, 0, h2 * nt + j] * LOG2E
            if masked:
                s = jnp.where(k0 + col <= t_q, s, NEG)
            m_n = jnp.maximum(m_i, jnp.max(s, axis=-1, keepdims=True))
            alpha = jnp.exp2(m_i - m_n)
            p = jnp.exp2(s - m_n)
            l_n = alpha * l_i + jnp.sum(p, axis=-1, keepdims=True)
            out.append((m_n, l_n, alpha * acc + _dot(p.astype(BF16), v)))
        return tuple(out)

    init = (jnp.full((t, 1), NEG, F32), jnp.zeros((t, 1), F32), jnp.zeros((t, LANES), F32))
    jd = (qi * t) // tk
    carry = lax.fori_loop(0, jd, lambda j, c: tile(j, c, False), (init, init))
    for dj in range(max(1, t // tk)):
        carry = tile(jd + dj, carry, True)
    (_, l_0, acc_0), (_, l_1, acc_1) = carry
    o_ref[0] = (jnp.where(lo, acc_0 / l_0, acc_1 / l_1) * og_ref[0]).astype(BF16)


FOX_TQ = 1024
FOX_TK = 1024


def _fox_attention(q, kb, vb, c, og, b, s):
    t, tk = min(FOX_TQ, s), min(FOX_TK, s)
    nt = s // tk
    hp = N_HEADS // 2
    negc = -c[:, :N_HEADS].reshape(b, nt, tk, hp, 2).transpose(0, 3, 4, 1, 2).reshape(b, hp, 2 * nt, 1, tk)
    qspec = pl.BlockSpec((1, t, LANES), lambda i, p, qi: (i, qi, p))
    kspec = pl.BlockSpec((1, s, LANES), lambda i, p, qi: (i, 0, p))
    return pl.pallas_call(
        functools.partial(_fox_attn_kernel, t=t, tk=tk, nt=nt),
        grid=(b, hp, s // t),
        in_specs=[qspec, kspec, kspec,
                  pl.BlockSpec((1, 1, 2 * nt, 1, tk), lambda i, p, qi: (i, p, 0, 0, 0)), qspec],
        out_specs=qspec,
        out_shape=jax.ShapeDtypeStruct((b, s, 1024), BF16),
        compiler_params=_params("parallel", "parallel", "arbitrary"),
        name="fox_attention",
    )(q.reshape(b, s, -1), kb.reshape(b, s, -1), vb.reshape(b, s, -1), negc, og.reshape(b, s, -1))


def _nsa_cmp_sample_kernel(*refs, n_pages):
    pt_ref = refs[0]
    pages = refs[1:1 + n_pages]
    pe_ref, w1_ref, w2_ref, o_ref, x_scr = refs[1 + n_pages:]
    del pt_ref
    rows = n_pages * N_GROUPS
    for j in range(n_pages):
        for slot in range(2):
            for g in range(N_GROUPS):
                r0 = (g * n_pages + j) * HEAD_DIM
                x_scr[slot, r0:r0 + HEAD_DIM, :] = pages[j][0, slot, g]
    for slot in range(2):
        acc = jnp.zeros((rows, LANES), F32)
        for dd in range(HEAD_DIM):
            xl = x_scr[slot, pl.ds(dd, rows, stride=HEAD_DIM), :] + pe_ref[slot, dd:dd + 1, :]
            acc = acc + _dot(xl.astype(BF16), w1_ref[slot, dd])
        hid = acc * jax.nn.sigmoid(acc)
        o_ref[0, slot] = _dot(hid.astype(BF16), w2_ref[slot])


def _nsa_cmp_sample(pool_t, pt, pe, w1, w2, db, n_pages):
    per = PAGE // BLOCK
    pe_t = jnp.tile(pe.transpose(0, 2, 1), (1, 1, per))
    w1_d = _block_diag(w1.transpose(0, 2, 1, 3), per).astype(BF16)
    w2_d = _block_diag(w2, per).astype(BF16)
    rows = n_pages * N_GROUPS
    page = lambda j: pl.BlockSpec((1, 2, N_GROUPS, HEAD_DIM, PAGE),
                                  lambda i, pt, j=j: (pt[i * n_pages + j], 0, 0, 0, 0))
    full = lambda a: pl.BlockSpec(a.shape, lambda i, pt: (0,) * a.ndim)
    return pl.pallas_call(
        functools.partial(_nsa_cmp_sample_kernel, n_pages=n_pages),
        grid_spec=pltpu.PrefetchScalarGridSpec(
            num_scalar_prefetch=1, grid=(db,),
            in_specs=[page(j) for j in range(n_pages)] + [full(pe_t), full(w1_d), full(w2_d)],
            out_specs=pl.BlockSpec((1, 2, rows, per * HEAD_DIM), lambda i, pt: (i, 0, 0, 0)),
            scratch_shapes=[pltpu.VMEM((2, rows * HEAD_DIM, PAGE), F32)]),
        out_shape=jax.ShapeDtypeStruct((db, 2, rows, per * HEAD_DIM), F32),
        compiler_params=_params("arbitrary"),
        name="nsa_compress_sample",
    )(pt, *([pool_t] * n_pages), pe_t, w1_d, w2_d)


def _page_specs(block, slot_block, n_pages):
    nd = len(block)
    return [pl.BlockSpec((1,) + block, lambda i, pt, j=j: (pt[i * n_pages + j], slot_block) + (0,) * (nd - 1))
            for j in range(n_pages)]


def _softmax_chunks(scores):
    m = scores[0]
    for s in scores[1:]:
        m = jnp.maximum(m, s)
    m = jnp.max(m, axis=-1, keepdims=True)
    ps = [jnp.exp2(s - m) for s in scores]
    tot = ps[0]
    for p in ps[1:]:
        tot = tot + p
    return ps, jnp.sum(tot, axis=-1, keepdims=True)


def _nsa_attn_sample_kernel(*refs, n_pages, past, dt, nbs):
    pages = refs[1:1 + n_pages]
    (qc_ref, qr_ref, kc_ref, vc_ref, win_ref, kwn_ref, vwn_ref, ksn_ref, vsn_ref, e_ref, en_ref, gt_ref,
     o_ref) = refs[1 + n_pages:]
    gd = N_GROUPS * HEAD_DIM
    nr = qc_ref.shape[1]
    per_r = dt * N_GROUPS
    row = lax.broadcasted_iota(jnp.int32, (nr, 1), 0)
    t_row = (row // N_GROUPS) % dt
    t_pos = past + t_row
    tn = lax.broadcasted_iota(jnp.int32, (nr, LANES), 1)

    kc = kc_ref[0]
    nb = kc.shape[0]
    n = lax.broadcasted_iota(jnp.int32, (nr, nb), 1)
    cmask = n * BLOCK + (BLOCK - 1) <= t_pos
    sc = jnp.where(cmask, _dot_nt(qc_ref[0], kc), NEG)
    e = jnp.where(cmask, jnp.exp2(sc - jnp.max(sc, axis=-1, keepdims=True)), 0.0)
    l = jnp.sum(e, axis=-1, keepdims=True)
    pc = e / jnp.where(l > 0.0, l, 1.0)
    o_c = _dot(pc.astype(BF16), vc_ref[0])

    imp = pc[0:per_r]
    for r in range(1, GROUP):
        imp = imp + pc[r * per_r:(r + 1) * per_r]
    n1 = lax.broadcasted_iota(jnp.int32, (per_r, nb), 1)
    t1 = past + lax.broadcasted_iota(jnp.int32, (per_r, 1), 0) // N_GROUPS
    cur = t1 // BLOCK
    forced = (n1 == 0) | (n1 == cur) | (n1 == cur - 1)
    dead = (n1 * BLOCK > t1) | (n1 >= nbs)
    sel = _top_blocks(jnp.where(forced, jnp.inf, jnp.where(dead, -jnp.inf, imp)), min(N_SELECT, nbs))
    selb = jnp.concatenate([sel] * GROUP, axis=0).astype(BF16)

    qr = qr_ref[0]
    wb = win_ref.shape[-1]
    kw = win_ref[0, 0].reshape(gd, wb).astype(BF16)
    vw = win_ref[0, 1].reshape(gd, wb).astype(BF16)
    d_old = t_pos - (past - wb + lax.broadcasted_iota(jnp.int32, (nr, wb), 1))
    s_old = jnp.where((d_old >= 0) & (d_old < WINDOW), _dot(qr, kw), NEG)
    s_new = jnp.where((tn <= t_row) & (tn < dt), _dot(qr, kwn_ref[0]), NEG)
    m = jnp.maximum(jnp.max(s_old, axis=-1, keepdims=True), jnp.max(s_new, axis=-1, keepdims=True))
    p_old = jnp.exp2(s_old - m)
    p_new = jnp.exp2(s_new - m)
    lw = jnp.sum(p_old, axis=-1, keepdims=True) + jnp.sum(p_new, axis=-1, keepdims=True)
    o_w = (_dot_nt(p_old.astype(BF16), vw) + _dot_nt(p_new.astype(BF16), vwn_ref[0])) / lw

    scores = []
    for j in range(n_pages):
        k_t = pages[j][0, 0].reshape(gd, PAGE).astype(BF16)
        scores.append(jnp.where(_dot(selb, e_ref[j]) > 0.5, _dot(qr, k_t), NEG))
    ok = (_dot(selb, en_ref[...]) > 0.5) & (tn <= t_row) & (tn < dt)
    scores.append(jnp.where(ok, _dot(qr, ksn_ref[0]), NEG))
    ps, ls = _softmax_chunks(scores)
    acc = _dot_nt(ps[n_pages].astype(BF16), vsn_ref[0])
    for j in range(n_pages):
        acc = acc + _dot_nt(ps[j].astype(BF16), pages[j][0, 1].reshape(gd, PAGE).astype(BF16))
    gt = gt_ref[0]
    o_ref[0] = gt[:, 0:1] * o_c + gt[:, 1:2] * (acc / ls) + gt[:, 2:3] * o_w


def _nsa_attn_sample(pool_t, pt, qbd_c, qbd_r, kc, vc, win_t, kwn, vwn, ksn, vsn, gates, past, dt, nbs, n_pages):
    db, nr, gd = qbd_r.shape
    key_blk = jnp.arange(n_pages * PAGE) // BLOCK
    e_tab = (jnp.arange(LANES)[None, :, None] == key_blk.reshape(n_pages, 1, PAGE)).astype(BF16)
    new_blk = jnp.where(jnp.arange(LANES) < dt, (past + jnp.arange(LANES)) // BLOCK, -1)
    e_new = (jnp.arange(LANES)[:, None] == new_blk[None, :]).astype(BF16)
    blk = lambda a: pl.BlockSpec((1,) + a.shape[1:], lambda i, pt: (i,) + (0,) * (a.ndim - 1))
    full = lambda a: pl.BlockSpec(a.shape, lambda i, pt: (0,) * a.ndim)
    per_batch = (qbd_c, qbd_r, kc, vc, win_t, kwn, vwn, ksn, vsn)
    return pl.pallas_call(
        functools.partial(_nsa_attn_sample_kernel, n_pages=n_pages, past=past, dt=dt, nbs=nbs),
        grid_spec=pltpu.PrefetchScalarGridSpec(
            num_scalar_prefetch=1, grid=(db,),
            in_specs=(_page_specs((2, N_GROUPS, HEAD_DIM, PAGE), 1, n_pages) + [blk(a) for a in per_batch]
                      + [full(e_tab), full(e_new), blk(gates)]),
            out_specs=pl.BlockSpec((1, nr, gd), lambda i, pt: (i, 0, 0))),
        out_shape=jax.ShapeDtypeStruct((db, nr, gd), F32),
        compiler_params=_params("parallel"),
        name="nsa_attention_sample",
    )(pt, *([pool_t] * n_pages), *per_batch, e_tab, e_new, gates)


def _fox_attn_sample_kernel(*refs, n_pages, dt):
    pages = refs[1:1 + n_pages]
    lfs = refs[1 + n_pages:1 + 2 * n_pages]
    q_ref, kn_ref, vn_ref, lfn_ref, og_ref, o_ref = refs[1 + 2 * n_pages:]
    hd = N_HEADS * HEAD_DIM
    tri = (lax.broadcasted_iota(jnp.int32, (PAGE, PAGE), 0) <= lax.broadcasted_iota(jnp.int32, (PAGE, PAGE), 1)).astype(BF16)

    def local_cum(lf):
        hi, mid, lo = _split3(lf)
        return _dot(hi, tri) + _dot(mid, tri) + _dot(lo, tri)

    q = q_ref[0]
    nr = q.shape[0]
    scores = []
    prefix = jnp.zeros((N_HEADS, 1), F32)
    for j in range(n_pages):
        loc = local_cum(lfs[j][0])
        c_page = (loc + prefix) * LOG2E
        prefix = prefix + loc[:, PAGE - 1:PAGE]
        k_t = pages[j][0, 0].reshape(hd, PAGE).astype(BF16)
        scores.append(_dot(q, k_t) - jnp.concatenate([c_page] * dt, axis=0))
    c_new = (local_cum(lfn_ref[0]) + prefix) * LOG2E
    t_row = lax.broadcasted_iota(jnp.int32, (nr, 1), 0) // N_HEADS
    tn = lax.broadcasted_iota(jnp.int32, (nr, LANES), 1)
    s_n = _dot(q, kn_ref[0]) - jnp.concatenate([c_new] * dt, axis=0)
    scores.append(jnp.where((tn <= t_row) & (tn < dt), s_n, NEG))
    ps, ls = _softmax_chunks(scores)
    acc = _dot_nt(ps[n_pages].astype(BF16), vn_ref[0])
    for j in range(n_pages):
        acc = acc + _dot_nt(ps[j].astype(BF16), pages[j][0, 1].reshape(hd, PAGE).astype(BF16))
    o_ref[0] = acc / ls * og_ref[0]


def _fox_attn_sample(pool_t, lf_pool_t, pt, qbd, kn_t, vn_t, lfn_t, og_t, dt, n_pages):
    db, nr, hd = qbd.shape
    blk = lambda a: pl.BlockSpec((1,) + a.shape[1:], lambda i, pt: (i,) + (0,) * (a.ndim - 1))
    lf_specs = [pl.BlockSpec((1, N_HEADS, PAGE), lambda i, pt, j=j: (pt[i * n_pages + j], 0, 0))
                for j in range(n_pages)]
    per_batch = (qbd, kn_t, vn_t, lfn_t, og_t)
    return pl.pallas_call(
        functools.partial(_fox_attn_sample_kernel, n_pages=n_pages, dt=dt),
        grid_spec=pltpu.PrefetchScalarGridSpec(
            num_scalar_prefetch=1, grid=(db,),
            in_specs=_page_specs((2, N_HEADS, HEAD_DIM, PAGE), 0, n_pages) + lf_specs + [blk(a) for a in per_batch],
            out_specs=pl.BlockSpec((1, nr, hd), lambda i, pt: (i, 0, 0))),
        out_shape=jax.ShapeDtypeStruct((db, nr, hd), F32),
        compiler_params=_params("parallel"),
        name="fox_attention_sample",
    )(pt, *([pool_t] * n_pages), *([lf_pool_t] * n_pages), *per_batch)


def _per_batch(a, dt, db):
    return a.reshape(dt, db, -1).transpose(1, 0, 2)


def _new_keys_t(a, dt, db):
    a = _per_batch(a, dt, db).transpose(0, 2, 1)
    return jnp.pad(a, ((0, 0), (0, 0), (0, LANES - dt))).astype(a.dtype)


def _nsa_qbd(q, dt, db):
    q5 = q.reshape(dt, db, N_GROUPS, GROUP, HEAD_DIM).transpose(1, 3, 0, 2, 4)
    eye = jnp.eye(N_GROUPS, dtype=q.dtype)
    out = q5[:, :, :, :, None, :] * eye[None, None, None, :, :, None]
    return out.reshape(db, GROUP * dt * N_GROUPS, N_GROUPS * HEAD_DIM)


def _nsa_undiag(o, dt, db):
    o6 = o.reshape(db, GROUP, dt, N_GROUPS, N_GROUPS, HEAD_DIM)
    dg = jnp.diagonal(o6, axis1=3, axis2=4)
    return dg.transpose(2, 0, 4, 1, 3).reshape(dt * db, N_HEADS * HEAD_DIM)


def _fox_qbd(q, dt, db):
    q4 = q.reshape(dt, db, N_HEADS, HEAD_DIM).transpose(1, 0, 2, 3)
    eye = jnp.eye(N_HEADS, dtype=q.dtype)
    out = q4[:, :, :, None, :] * eye[None, None, :, :, None]
    return out.reshape(db, dt * N_HEADS, N_HEADS * HEAD_DIM)


def _fox_undiag(o, dt, db):
    o5 = o.reshape(db, dt, N_HEADS, N_HEADS, HEAD_DIM)
    dg = jnp.diagonal(o5, axis1=2, axis2=3)
    return dg.transpose(1, 0, 3, 2).reshape(dt * db, N_HEADS * HEAD_DIM)


def _prompt_mods(mod, b):
    return [m.reshape(b, 1, -1) for m in jnp.split(mod[:b], 6, axis=-1)]


def _sample_mods(mod, b, t):
    return [jnp.tile(m, (t, 1))[None] for m in jnp.split(mod[b:], 6, axis=-1)]


def kernel(x_prompt, x_sample, cache_nsa_kv, cache_nsa_win, cache_fox_kv, cache_fox_logf, state_ffn_conv, page_table, c_prompt, c_sample, w_ada, b_ada, norm_mix_g, norm_ffn_g, w_nsa_in, pe_cmp, w_cmp1, w_cmp2, w_nsa_out, w_fox_in, b_fox_f, fox_q_norm_g, fox_k_norm_g, w_fox_out, w_ffn_up, ffn_conv_w, ffn_conv_b, w_ffn_down, final_norm_g):
    b, s, d = x_prompt.shape
    db, dt, _ = x_sample.shape
    f_dim = w_ffn_down.shape[1]
    depth = w_ada.shape[0]
    tm = 256
    tpb = s // tm
    tm_f = 512
    tpb_f = s // tm_f

    n_pages = page_table.shape[1]
    past = n_pages * PAGE
    nbs = -(-(past + dt) // BLOCK)
    per = PAGE // BLOCK
    r_s = dt * db
    pt = page_table.reshape(-1).astype(jnp.int32)
    key_last = (0, 2, 3, 4, 1)

    c_all = jnp.concatenate([c_prompt, c_sample], axis=0)
    xp = x_prompt.reshape(b * s, d)
    xs = x_sample.transpose(1, 0, 2).reshape(r_s, d)
    tabs_p = _rope_tables(jnp.arange(s, dtype=jnp.int32))
    tabs_s = _rope_tables(past + jnp.arange(r_s, dtype=jnp.int32) // db)

    nsa_kv_p, nsa_win_p, fox_kv_p, fox_lf_p, conv_p = [], [], [], [], []
    nsa_kv_s, nsa_win_s, fox_kv_s, fox_lf_s, conv_s = [], [], [], [], []
    y_prompt = y_sample = None
    for i in range(depth):
        j = i // 2
        mod = _adaln(c_all, w_ada[i].astype(BF16), b_ada[i][None])
        mp = _prompt_mods(mod, b)
        ms = _sample_mods(mod, b, dt)
        g_mix = norm_mix_g[i][None]
        if i % 2 == 0:
            w_in = _nsa_weight(w_nsa_in[j])
            qc, qr, rows, win, dup, gates = _nsa_proj(xp, g_mix, mp[1], mp[0], w_in, tabs_p, "prompt", tm, tpb)
            cdup = _compress_prompt(rows, *_cmp_weights(pe_cmp[j], w_cmp1[j], w_cmp2[j]), b, s)
            attn_p = _nsa_attention_t(qc, qr, cdup, dup, gates, b, s)
            w_out = w_nsa_out[j].astype(BF16)
            seq_last = lambda a, lead: a.transpose(0, 2, 1).reshape((b,) + lead + (a.shape[1],)).transpose(0, 4, 1, 2, 3)
            nsa_kv_p.append(seq_last(rows.reshape(b, s, -1), (4, N_GROUPS, HEAD_DIM)))
            wl = min(WINDOW, s)
            nsa_win_p.append(seq_last(win.reshape(b, s, -1)[:, s - wl:], (2, N_GROUPS, HEAD_DIM)))

            qc, qr, rows, win, _, gates = _nsa_proj(xs, g_mix, ms[1], ms[0], w_in, tabs_s, "sample", r_s, 1)
            nsa_kv_s.append(rows.reshape(dt, db, 4, N_GROUPS, HEAD_DIM).transpose(1, 0, 2, 3, 4))
            nsa_win_s.append(win.reshape(dt, db, 2, N_GROUPS, HEAD_DIM).transpose(1, 0, 2, 3, 4))
            pool_t = jnp.transpose(cache_nsa_kv[j], key_last)
            kc2 = _nsa_cmp_sample(pool_t, pt, pe_cmp[j], w_cmp1[j], w_cmp2[j], db, n_pages)
            kc = kc2.reshape(db, 2, N_GROUPS, n_pages, per, HEAD_DIM).transpose(0, 1, 3, 4, 2, 5)
            kc = kc.reshape(db, 2, n_pages * per, N_GROUPS * HEAD_DIM)
            kc = jnp.pad(kc, ((0, 0), (0, 0), (0, LANES - n_pages * per), (0, 0))).astype(BF16)
            qbd_c, qbd_r = _nsa_qbd(qc, dt, db), _nsa_qbd(qr, dt, db)
            gd = N_GROUPS * HEAD_DIM
            newt = lambda a: _new_keys_t(a.astype(BF16), dt, db)
            g_s = gates.reshape(dt, db, N_GROUPS, LANES)[..., :GROUP * 3].reshape(dt, db, N_GROUPS, GROUP, 3)
            g_s = g_s.transpose(1, 3, 0, 2, 4).reshape(db, GROUP * dt * N_GROUPS, 3)
            g_s = jnp.pad(g_s, ((0, 0), (0, 0), (0, LANES - 3)))
            o_s = _nsa_attn_sample(pool_t, pt, qbd_c, qbd_r, kc[:, 0], kc[:, 1],
                                   jnp.transpose(cache_nsa_win[j], key_last), newt(win[:, :gd]), newt(win[:, gd:]),
                                   newt(rows[:, 2 * gd:3 * gd]), newt(rows[:, 3 * gd:]), g_s, past, dt, nbs, n_pages)
            attn_s = _nsa_undiag(o_s, dt, db).astype(BF16)
        else:
            w_in = _fox_weight(w_fox_in[j])
            fox = lambda x, sc, sh, mode, t, n: _fox_proj(x, g_mix, sc, sh, w_in, fox_q_norm_g[j], fox_k_norm_g[j],
                                                         b_fox_f[j], mode, t, n)
            q, kf, kb, vf, vb, og, lf, c = fox(xp, mp[1], mp[0], "prompt", tm, tpb)
            attn_p = _fox_attention(q, kb, vb, c, og, b, s).reshape(b * s, d)
            w_out = w_fox_out[j].astype(BF16)
            hs = (N_HEADS, HEAD_DIM)
            kv_t = jnp.stack([kf.reshape(b, s, -1).transpose(0, 2, 1), vf.reshape(b, s, -1).transpose(0, 2, 1)], axis=1)
            fox_kv_p.append(kv_t.reshape((b, 2) + hs + (s,)).transpose(0, 4, 1, 2, 3))
            fox_lf_p.append(lf[:, :N_HEADS].reshape(b, s, N_HEADS))

            q, kf, kb, vf, vb, og, lf, _ = fox(xs, ms[1], ms[0], "sample", r_s, 1)
            fox_kv_s.append(jnp.stack([kf.reshape((dt, db) + hs), vf.reshape((dt, db) + hs)], axis=2).transpose(1, 0, 2, 3, 4))
            fox_lf_s.append(lf[:, :N_HEADS].reshape(dt, db, N_HEADS).transpose(1, 0, 2))
            og_t = jnp.repeat(_per_batch(og, dt, db)[:, :, None, :], N_HEADS, axis=2).reshape(db, dt * N_HEADS, -1)
            o_full = _fox_attn_sample(jnp.transpose(cache_fox_kv[j], key_last),
                                      jnp.transpose(cache_fox_logf[j], (0, 2, 1)), pt, _fox_qbd(q, dt, db),
                                      _new_keys_t(kb, dt, db), _new_keys_t(vb, dt, db),
                                      _new_keys_t(lf[:, :N_HEADS], dt, db), og_t, dt, n_pages)
            attn_s = _fox_undiag(o_full, dt, db).astype(BF16)

        final = final_norm_g[None] if i == depth - 1 else None
        ffn_w = (norm_ffn_g[i][None], w_ffn_up[i].astype(BF16), ffn_conv_w[i], ffn_conv_b[i][None],
                 w_ffn_down[i].astype(BF16))

        res = _ffn(xp, attn_p, w_out, (mp[2], mp[4], mp[3], mp[5]), *ffn_w, "prompt", tm_f, tpb_f, 1, final_g=final)
        xp = res[0]
        conv_p.append(res[1].reshape(b, tpb_f, 8, 2 * f_dim)[:, -1, 6:, :])
        if final is not None:
            y_prompt = res[2].reshape(b, s, d)

        state = state_ffn_conv[i].transpose(1, 0, 2).reshape(2 * db, 2 * f_dim)
        res = _ffn(xs, attn_s, w_out, (ms[2], ms[4], ms[3], ms[5]), *ffn_w, "sample", r_s, 1, db, state=state,
                   final_g=final)
        xs = res[0]
        conv_s.append(res[1].reshape(2, db, 2 * f_dim).transpose(1, 0, 2))
        if final is not None:
            y_sample = res[2].reshape(dt, db, d).transpose(1, 0, 2)

    return (y_prompt, y_sample, jnp.stack(nsa_kv_p), jnp.stack(nsa_kv_s), jnp.stack(nsa_win_p), jnp.stack(nsa_win_s),
            jnp.stack(fox_kv_p), jnp.stack(fox_kv_s), jnp.stack(fox_lf_p), jnp.stack(fox_lf_s),
            jnp.stack(conv_p), jnp.stack(conv_s))
```

```python
import functools

import jax
import jax.numpy as jnp
from jax import lax
from jax.experimental import pallas as pl
from jax.experimental.pallas import tpu as pltpu

F32 = jnp.float32
BF16 = jnp.bfloat16

HEAD_DIM = 64
N_HEADS = 16
N_GROUPS = 4
GROUP = N_HEADS // N_GROUPS
BLOCK = 64
N_SELECT = 16
WINDOW = 512
ROT_DIM = 16
ROPE_THETA = 500000.0
PAGE = 128
Q_BLOCK = 128
EPS = 1e-6
NEG = -1e30
SCALE = HEAD_DIM ** -0.5
LOG2E = 1.4426950408889634
QSCALE = SCALE * LOG2E

LANES = 128
VMEM_LIMIT = 56 * 1024 * 1024


def _params(*sem, flags=None):
    return pltpu.CompilerParams(dimension_semantics=sem, vmem_limit_bytes=VMEM_LIMIT, flags=flags)


def _dot(a, b):
    return jnp.dot(a, b, preferred_element_type=F32)


def _dot_nt(a, b):
    return lax.dot_general(a, b, (((1,), (1,)), ((), ())), preferred_element_type=F32)


def _split3(x):
    hi = x.astype(BF16)
    r1 = x - hi.astype(F32)
    mid = r1.astype(BF16)
    lo = (r1 - mid.astype(F32)).astype(BF16)
    return hi, mid, lo


def _ada_kernel(c_ref, w_ref, b_ref, o_ref):
    c = c_ref[...]
    a = (c * jax.nn.sigmoid(c)).astype(BF16)
    o_ref[...] = _dot(a, w_ref[...]) + b_ref[...]


def _adaln(c, w, b):
    r, d = c.shape
    n = w.shape[1]
    tn = n // 4
    return pl.pallas_call(
        _ada_kernel,
        grid=(n // tn,),
        in_specs=[pl.BlockSpec((r, d), lambda j: (0, 0)),
                  pl.BlockSpec((d, tn), lambda j: (0, j)),
                  pl.BlockSpec((1, tn), lambda j: (0, j))],
        out_specs=pl.BlockSpec((r, tn), lambda j: (0, j)),
        out_shape=jax.ShapeDtypeStruct((r, n), F32),
        compiler_params=_params("arbitrary"),
        name="adaln",
    )(c, w, b)


def _norm_mod(x, g, scale, shift):
    ms = jnp.mean(x * x, axis=-1, keepdims=True)
    return (x * lax.rsqrt(ms + EPS) * g) * (1.0 + scale) + shift


def _rope_tables(pos):
    freqs = ROPE_THETA ** (-jnp.arange(0, ROT_DIM, 2, dtype=F32) / ROT_DIM)
    ang = pos.astype(F32)[:, None] * freqs[None, :]
    cos, sin = jnp.cos(ang), jnp.sin(ang)
    half = ROT_DIM // 2
    one = jnp.ones((pos.shape[0], HEAD_DIM - ROT_DIM), F32)
    zero = jnp.zeros_like(one)
    zh = jnp.zeros_like(cos)
    c = jnp.concatenate([cos, cos, one], axis=1)
    s_lo = jnp.concatenate([zh, sin, zero], axis=1)
    s_hi = jnp.concatenate([-sin, zh, zero], axis=1)
    rep = LANES // HEAD_DIM
    return jnp.tile(c, (1, rep)), jnp.tile(s_lo, (1, rep)), jnp.tile(s_hi, (1, rep))


def _rope(v, c, s_lo, s_hi):
    half = ROT_DIM // 2
    return v * c + pltpu.roll(v, half, 1) * s_lo + pltpu.roll(v, LANES - half, 1) * s_hi


def _row_specs(mode, tm, tpb, d):
    if mode == "prompt":
        mod = pl.BlockSpec((1, 1, d), lambda i: (i // tpb, 0, 0))
        tab = pl.BlockSpec((tm, LANES), lambda i: (i % tpb, 0))
    else:
        mod = pl.BlockSpec((1, tm, d), lambda i: (0, 0, 0))
        tab = pl.BlockSpec((tm, LANES), lambda i: (0, 0))
    return mod, tab


NSA_DUP = 4 * N_GROUPS * LANES
NSA_W_COLS = 1024 + 6 * 256 + N_GROUPS * LANES


def _nsa_proj_kernel(x_ref, g_ref, sc_ref, sh_ref, w_ref, tc_ref, tl_ref, th_ref,
                     qc_ref, qr_ref, rows_ref, win_ref, dup_ref, gates_ref):
    h = _norm_mod(x_ref[...], g_ref[...], sc_ref[0], sh_ref[0]).astype(BF16)
    z = _dot(h, w_ref[...])
    tc, tl, th = tc_ref[...], tl_ref[...], th_ref[...]
    lo = lax.broadcasted_iota(jnp.int32, (z.shape[0], LANES), 1) < HEAD_DIM

    def chunk(j):
        return z[:, j * LANES:(j + 1) * LANES]

    def put_dup(kind, pair, v):
        vr = pltpu.roll(v, HEAD_DIM, 1)
        base = (kind * N_GROUPS + 2 * pair) * LANES
        dup_ref[:, base:base + LANES] = jnp.where(lo, v, vr).astype(BF16)
        dup_ref[:, base + LANES:base + 2 * LANES] = jnp.where(lo, vr, v).astype(BF16)

    for j in range(8):
        v = chunk(j)
        sl = slice(j * LANES, (j + 1) * LANES)
        qc_ref[:, sl] = (v * QSCALE).astype(BF16)
        qr_ref[:, sl] = (_rope(v, tc, tl, th) * QSCALE).astype(BF16)
    for j in range(4):
        rows_ref[:, j * LANES:(j + 1) * LANES] = chunk(8 + j)
    for j in range(2):
        ks = _rope(chunk(12 + j), tc, tl, th)
        vs = chunk(14 + j)
        rows_ref[:, (4 + j) * LANES:(5 + j) * LANES] = ks
        rows_ref[:, (6 + j) * LANES:(7 + j) * LANES] = vs
        put_dup(0, j, ks)
        put_dup(1, j, vs)
    for j in range(2):
        kw = _rope(chunk(16 + j), tc, tl, th)
        vw = chunk(18 + j)
        win_ref[:, j * LANES:(j + 1) * LANES] = kw
        win_ref[:, (2 + j) * LANES:(3 + j) * LANES] = vw
        put_dup(2, j, kw)
        put_dup(3, j, vw)
    for j in range(N_GROUPS):
        gates_ref[:, j * LANES:(j + 1) * LANES] = jax.nn.sigmoid(chunk(20 + j))


def _nsa_proj(x, g, scale, shift, w, tabs, mode, tm, tpb):
    r, d = x.shape
    mod, tab = _row_specs(mode, tm, tpb, d)
    row = lambda n: pl.BlockSpec((tm, n), lambda i: (i, 0))
    outs = [(1024, BF16), (1024, BF16), (1024, F32), (512, F32), (NSA_DUP, BF16), (N_GROUPS * LANES, F32)]
    return pl.pallas_call(
        _nsa_proj_kernel,
        grid=(r // tm,),
        in_specs=[row(d), pl.BlockSpec((1, d), lambda i: (0, 0)), mod, mod,
                  pl.BlockSpec(w.shape, lambda i: (0, 0)), tab, tab, tab],
        out_specs=[row(n) for n, _ in outs],
        out_shape=[jax.ShapeDtypeStruct((r, n), t) for n, t in outs],
        compiler_params=_params("parallel"),
        name="nsa_proj",
    )(x, g, scale, shift, w, *tabs)


def _nsa_weight(w_in):
    d = w_in.shape[0]
    main = w_in[:, :1024 + 6 * 256]
    gates = w_in[:, 1024 + 6 * 256:].reshape(d, N_GROUPS, GROUP * 3)
    gates = jnp.pad(gates, ((0, 0), (0, 0), (0, LANES - GROUP * 3))).reshape(d, N_GROUPS * LANES)
    return jnp.concatenate([main, gates], axis=1).astype(BF16)


def _cmp_kernel(x_ref, pe_ref, w1_ref, w2_ref, o_ref, acc_ref):
    lc = pl.program_id(2)

    @pl.when(lc == 0)
    def _():
        acc_ref[...] = jnp.zeros_like(acc_ref)

    acc = acc_ref[...]
    for l in range(x_ref.shape[1]):
        xl = (x_ref[0, l] + pe_ref[0, l:l + 1, :]).astype(BF16)
        acc = acc + _dot(xl, w1_ref[0, l])
    acc_ref[...] = acc

    @pl.when(lc == pl.num_programs(2) - 1)
    def _():
        hid = acc * jax.nn.sigmoid(acc)
        o_ref[0, 0] = _dot(hid.astype(BF16), w2_ref[0]).astype(BF16)


def _block_diag(w, n):
    eye = jnp.eye(n, dtype=w.dtype)
    out = jnp.einsum("ij,...ab->...iajb", eye, w)
    return out.reshape(w.shape[:-2] + (n * w.shape[-2], n * w.shape[-1]))


def _cmp_weights(pe, w1, w2):
    pe_t = jnp.tile(pe, (1, 1, N_GROUPS))
    w1_bd = _block_diag(w1, N_GROUPS).astype(BF16)
    w2_dup = jnp.concatenate([w2, w2], axis=-1)
    w2_bd = _block_diag(w2_dup, N_GROUPS).astype(BF16)
    return pe_t, w1_bd, w2_bd


def _compress_prompt(rows, pe_t, w1_bd, w2_bd, b, s):
    nb = s // BLOCK
    gd = N_GROUPS * HEAD_DIM
    xt = rows.reshape(b, nb, BLOCK, -1)[..., :2 * gd].transpose(0, 2, 1, 3)
    lstep = 8
    return pl.pallas_call(
        _cmp_kernel,
        grid=(b, 2, BLOCK // lstep),
        in_specs=[pl.BlockSpec((1, lstep, nb, gd), lambda i, kv, lc: (i, lc, 0, kv)),
                  pl.BlockSpec((1, lstep, gd), lambda i, kv, lc: (kv, lc, 0)),
                  pl.BlockSpec((1, lstep, gd, gd), lambda i, kv, lc: (kv, lc, 0, 0)),
                  pl.BlockSpec((1, gd, N_GROUPS * LANES), lambda i, kv, lc: (kv, 0, 0))],
        out_specs=pl.BlockSpec((1, 1, nb, N_GROUPS * LANES), lambda i, kv, lc: (i, kv, 0, 0)),
        out_shape=jax.ShapeDtypeStruct((b, 2, nb, N_GROUPS * LANES), BF16),
        scratch_shapes=[pltpu.VMEM((nb, gd), F32)],
        compiler_params=_params("parallel", "parallel", "arbitrary"),
        name="nsa_compress",
    )(xt, pe_t, w1_bd, w2_bd)


def _top_rows(v, n_sel):
    nb = v.shape[0]
    n = lax.broadcasted_iota(jnp.int32, v.shape, 0)
    sel = jnp.zeros(v.shape, F32)
    for _ in range(n_sel):
        mx = jnp.max(v, axis=0, keepdims=True)
        idx = jnp.min(jnp.where(v == mx, n, nb), axis=0, keepdims=True)
        hit = n == idx
        sel = jnp.where(hit, 1.0, sel)
        v = jnp.where(hit, -jnp.inf, v)
    return sel


def _top_blocks(v, n_sel):
    rows, nb = v.shape
    pad = -rows % LANES
    if pad:
        v = jnp.concatenate([v, jnp.zeros((pad, nb), v.dtype)], axis=0)
    return _top_rows(v.T, n_sel).T[:rows]


NSA_TK = 1024


def _nsa_attn_t_kernel(qc_ref, qr_ref, kc_ref, vc_ref, ks_ref, vs_ref, kw_ref, vw_ref, e_ref, gt_ref, o_ref,
                       sel_scr, *, tk, wlen):
    nq = Q_BLOCK
    ncol = GROUP * nq
    s0 = pl.program_id(2) * nq
    t_q = s0 + lax.broadcasted_iota(jnp.int32, (1, nq), 1)
    t_col = jnp.concatenate([t_q] * GROUP, axis=1)
    qc, qr = qc_ref[0, 0], qr_ref[0, 0]

    kc = kc_ref[0, 0]
    nb = kc.shape[0]
    cmask = lax.broadcasted_iota(jnp.int32, (nb, 1), 0) * BLOCK + (BLOCK - 1) <= t_col
    sc = jnp.where(cmask, _dot(kc, qc), NEG)
    e = jnp.where(cmask, jnp.exp2(sc - jnp.max(sc, axis=0, keepdims=True)), 0.0)
    l = jnp.sum(e, axis=0, keepdims=True)
    pc = e / jnp.where(l > 0.0, l, 1.0)
    o_c = _dot(vc_ref[0, 0, 0:HEAD_DIM, :], pc.astype(BF16))

    imp = pc[:, 0:nq]
    for r in range(1, GROUP):
        imp = imp + pc[:, r * nq:(r + 1) * nq]
    n = lax.broadcasted_iota(jnp.int32, (nb, nq), 0)
    cur = t_q // BLOCK
    forced = (n == 0) | (n == cur) | (n == cur - 1)
    future = n * BLOCK > t_q
    sel = _top_rows(jnp.where(forced, jnp.inf, jnp.where(future, -jnp.inf, imp)), min(N_SELECT, nb))
    sel_scr[...] = sel

    w0 = pl.multiple_of(jnp.maximum(s0 + nq - wlen, 0), nq)
    dpos = t_col - (w0 + lax.broadcasted_iota(jnp.int32, (wlen, 1), 0))
    sw = jnp.where((dpos >= 0) & (dpos < WINDOW), _dot(kw_ref[0, pl.ds(w0, wlen), :], qr), NEG)
    pw = jnp.exp2(sw - jnp.max(sw, axis=0, keepdims=True))
    lw = jnp.sum(pw, axis=0, keepdims=True)
    pwb = pw.astype(BF16)
    c0 = w0 // nq
    o_w = _dot(vw_ref[0, c0, 0:HEAD_DIM, :], pwb[0:nq])
    for c in range(1, wlen // nq):
        o_w = o_w + _dot(vw_ref[0, c0 + c, 0:HEAD_DIM, :], pwb[c * nq:(c + 1) * nq])
    o_w = o_w / lw

    sub = lax.broadcasted_iota(jnp.int32, (tk, 1), 0)

    def body(j, carry):
        m_i, l_i, acc = carry
        k0 = pl.multiple_of(j * tk, tk)
        rows = sel_scr[pl.ds(pl.multiple_of(j * (tk // BLOCK), tk // BLOCK), tk // BLOCK), :]
        okb = jnp.broadcast_to(rows[:, None, :], (tk // BLOCK, BLOCK, nq)).reshape(tk, nq)
        ok = (okb > 0.5) & (k0 + sub <= t_q)
        s = jnp.where(jnp.concatenate([ok] * GROUP, axis=1), _dot(ks_ref[0, pl.ds(k0, tk), :], qr), NEG)
        m_n = jnp.maximum(m_i, jnp.max(s, axis=0, keepdims=True))
        alpha = jnp.exp2(m_i - m_n)
        p = jnp.exp2(s - m_n)
        l_n = alpha * l_i + jnp.sum(p, axis=0, keepdims=True)
        return m_n, l_n, alpha * acc + _dot(vs_ref[0, j, 0:HEAD_DIM, :], p.astype(BF16))

    init = (jnp.full((1, ncol), NEG, F32), jnp.zeros((1, ncol), F32), jnp.zeros((HEAD_DIM, ncol), F32))
    _, l_s, acc_s = lax.fori_loop(0, (s0 + nq + tk - 1) // tk, body, init)

    gt = gt_ref[0, 0]
    out = gt[0:1] * o_c + gt[1:2] * (acc_s / l_s) + gt[2:3] * o_w
    o_ref[0, 0] = out.astype(BF16)


def _nsa_attention_t(qc, qr, cdup, dup, gates, b, s):
    nb = s // BLOCK
    tk = min(NSA_TK, s)
    nt = s // tk
    wlen = min(WINDOW + Q_BLOCK, s)
    nqb = s // Q_BLOCK
    g4 = N_GROUPS
    ncol = GROUP * Q_BLOCK

    def q_t(q):
        q6 = q.reshape(b, nqb, Q_BLOCK, g4, GROUP, HEAD_DIM).transpose(0, 3, 5, 1, 4, 2)
        q6 = q6.reshape(b, g4, HEAD_DIM, nqb * ncol)
        return jnp.pad(q6, ((0, 0), (0, 0), (0, LANES - HEAD_DIM), (0, 0)))

    def v_t(kind, rows):
        v = dup.reshape(b, s, -1)[:, :, kind * g4 * LANES:(kind + 1) * g4 * LANES]
        return v.reshape(b, s // rows, rows, g4 * LANES).transpose(0, 1, 3, 2)

    vc_t = cdup[:, 1].reshape(b, nb, g4, LANES).transpose(0, 2, 3, 1)
    vs_t, vw_t = v_t(1, tk), v_t(3, Q_BLOCK)
    blk = (jnp.arange(s) // BLOCK).reshape(nt, tk, 1)
    expand = (jnp.arange(nb)[None, None, :] == blk).astype(BF16)
    g_t = gates.reshape(b, nqb, Q_BLOCK, g4, LANES)[..., :GROUP * 3].reshape(b, nqb, Q_BLOCK, g4, GROUP, 3)
    g_t = g_t.transpose(0, 3, 5, 1, 4, 2).reshape(b, g4, 3, nqb * ncol)
    g_t = jnp.pad(g_t, ((0, 0), (0, 0), (0, 5), (0, 0)))

    qspec = pl.BlockSpec((1, 1, LANES, ncol), lambda i, g, q: (i, g, 0, q))
    dspec = lambda kind: pl.BlockSpec((1, s, LANES), lambda i, g, q: (i, 0, kind * g4 + g))
    o_t = pl.pallas_call(
        functools.partial(_nsa_attn_t_kernel, tk=tk, wlen=wlen),
        grid=(b, g4, nqb),
        in_specs=[qspec, qspec,
                  pl.BlockSpec((1, 1, nb, LANES), lambda i, g, q: (i, 0, 0, g)),
                  pl.BlockSpec((1, 1, LANES, nb), lambda i, g, q: (i, g, 0, 0)),
                  dspec(0), pl.BlockSpec((1, nt, LANES, tk), lambda i, g, q: (i, 0, g, 0)),
                  dspec(2), pl.BlockSpec((1, nqb, LANES, Q_BLOCK), lambda i, g, q: (i, 0, g, 0)),
                  pl.BlockSpec(expand.shape, lambda i, g, q: (0, 0, 0)),
                  pl.BlockSpec((1, 1, 8, ncol), lambda i, g, q: (i, g, 0, q))],
        out_specs=pl.BlockSpec((1, 1, HEAD_DIM, ncol), lambda i, g, q: (i, g, 0, q)),
        out_shape=jax.ShapeDtypeStruct((b, g4, HEAD_DIM, nqb * ncol), BF16),
        scratch_shapes=[pltpu.VMEM((nb, Q_BLOCK), F32)],
        compiler_params=_params("parallel", "parallel", "arbitrary"),
        name="nsa_attention",
    )(q_t(qc), q_t(qr), cdup, vc_t, dup.reshape(b, s, -1), vs_t, dup.reshape(b, s, -1), vw_t, expand, g_t)
    o6 = o_t.reshape(b, g4, HEAD_DIM, nqb, GROUP, Q_BLOCK).transpose(0, 3, 5, 1, 4, 2)
    return o6.reshape(b * s, g4 * GROUP * HEAD_DIM)


FFN_CHUNK = 256


def _ffn_kernel(*refs, u, tpb, chain, final, nf):
    (x_ref, a_ref, wo_ref, gm_ref, g_ref, sc_ref, sh_ref, gf_ref, wu_ref, cw_ref, cb_ref, wd_ref) = refs[:12]
    k = 12
    st_ref = gfin_ref = y_ref = carry_scr = None
    if not chain:
        st_ref = refs[k]
        k += 1
    if final:
        gfin_ref = refs[k]
        k += 1
    xo_ref, tail_ref = refs[k], refs[k + 1]
    k += 2
    if final:
        y_ref = refs[k]
        k += 1
    ext_scr = refs[k]
    if chain:
        carry_scr = refs[k + 1]

    fc = FFN_CHUNK
    tm = x_ref.shape[0]
    base = ext_scr.shape[0] - tm
    tail = tail_ref.shape[0]
    x1 = x_ref[...] + gm_ref[0] * _dot(a_ref[...], wo_ref[...])
    h = _norm_mod(x1, g_ref[...], sc_ref[0], sh_ref[0]).astype(BF16)
    if chain:
        first = (pl.program_id(0) % tpb) == 0
    acc = jnp.zeros((tm, x_ref.shape[1]), F32)
    f_dim = wd_ref.shape[0]

    def conv_half(f, half):
        cs = slice(half * f_dim + f * fc, half * f_dim + (f + 1) * fc)
        hs = slice(half * fc, (half + 1) * fc)
        up = _dot(h, wu_ref[:, cs])
        if chain:
            ext_scr[0:base, hs] = jnp.where(first, 0.0, carry_scr[f, :, hs])
            carry_scr[f, :, hs] = up[tm - base:, :]
        else:
            ext_scr[0:base, hs] = st_ref[:, cs]
        ext_scr[base:, hs] = up
        tail_ref[:, cs] = up[tm - tail:, :]
        cw = cw_ref[:, cs]
        return (cb_ref[:, cs] + cw[0:1] * ext_scr[base - 2 * u:base - 2 * u + tm, hs]
                + cw[1:2] * ext_scr[base - u:base - u + tm, hs] + cw[2:3] * up)

    for f in range(nf):
        a, g = conv_half(f, 0), conv_half(f, 1)
        act = (g * jax.nn.sigmoid(g) * a).astype(BF16)
        acc = acc + _dot(act, wd_ref[f * fc:(f + 1) * fc, :])
    xn = x1 + gf_ref[0] * acc
    xo_ref[...] = xn
    if final:
        ms = jnp.mean(xn * xn, axis=-1, keepdims=True)
        y_ref[...] = xn * lax.rsqrt(ms + EPS) * gfin_ref[...]


def _ffn(x, attn, w_out, mods, g, wu, cw, cb, wd, mode, tm, tpb, u, state=None, final_g=None):
    r, d = x.shape
    f2 = wu.shape[1]
    nf = f2 // (2 * FFN_CHUNK)
    chain = state is None
    final = final_g is not None
    base = 8 if chain else 2 * u
    tail = 8 if chain else 2 * u
    mod, _ = _row_specs(mode, tm, tpb, d)
    row = lambda n: pl.BlockSpec((tm, n), lambda i: (i, 0))
    full = lambda a: pl.BlockSpec(a.shape, lambda i: (0,) * a.ndim, pipeline_mode=pl.Buffered(1))
    args = [x, attn, w_out, mods[0], g, mods[1], mods[2], mods[3], wu, cw, cb, wd]
    specs = [row(d), row(d), full(w_out), mod, full(g), mod, mod, mod, full(wu), full(cw), full(cb), full(wd)]
    if not chain:
        args.append(state)
        specs.append(full(state))
    if final:
        args.append(final_g)
        specs.append(full(final_g))
    out_specs = [row(d), pl.BlockSpec((tail, f2), lambda i: (i, 0))]
    out_shape = [jax.ShapeDtypeStruct((r, d), F32), jax.ShapeDtypeStruct((r // tm * tail, f2), F32)]
    if final:
        out_specs.append(row(d))
        out_shape.append(jax.ShapeDtypeStruct((r, d), F32))
    scratch = [pltpu.VMEM((base + tm, 2 * FFN_CHUNK), F32)]
    if chain:
        scratch.append(pltpu.VMEM((nf, base, 2 * FFN_CHUNK), F32))
    return pl.pallas_call(
        functools.partial(_ffn_kernel, u=u, tpb=tpb, chain=chain, final=final, nf=nf),
        grid=(r // tm,),
        in_specs=specs, out_specs=out_specs, out_shape=out_shape, scratch_shapes=scratch,
        compiler_params=_params("arbitrary"),
        name="out_proj_ffn",
    )(*args)


FOX_W_COLS = 4 * 1024 + LANES


def _fox_proj_kernel(x_ref, g_ref, sc_ref, sh_ref, w_ref, ind_ref, indt_ref, gq_ref, gk_ref, bf_ref,
                     q_ref, kf_ref, kb_ref, vf_ref, vb_ref, og_ref, lf_ref, c_ref, carry_scr, *, tpb):
    h = _norm_mod(x_ref[...], g_ref[...], sc_ref[0], sh_ref[0]).astype(BF16)
    z = _dot(h, w_ref[...])
    tm = z.shape[0]
    ind, indt = ind_ref[...], indt_ref[...]

    def head_norm(zc, gain):
        sq = zc * zc
        hi = sq.astype(BF16)
        lo = (sq - hi.astype(F32)).astype(BF16)
        ms = (_dot(hi, ind) + _dot(lo, ind)) * (1.0 / HEAD_DIM)
        rinv = lax.rsqrt(ms + EPS)
        rh = rinv.astype(BF16)
        rl = (rinv - rh.astype(F32)).astype(BF16)
        return zc * (_dot(rh, indt) + _dot(rl, indt)) * gain

    q_ref[...] = (head_norm(z[:, 0:1024], gq_ref[...]) * QSCALE).astype(BF16)
    kn = head_norm(z[:, 1024:2048], gk_ref[...])
    kf_ref[...] = kn
    kb_ref[...] = kn.astype(BF16)
    v = z[:, 2048:3072]
    vf_ref[...] = v
    vb_ref[...] = v.astype(BF16)
    og_ref[...] = jax.nn.sigmoid(z[:, 3072:4096])
    zf = z[:, 4096:4096 + LANES] + bf_ref[...]
    lf = jnp.minimum(zf, 0.0) - jnp.log1p(jnp.exp(-jnp.abs(zf)))
    lf = jnp.where(lax.broadcasted_iota(jnp.int32, lf.shape, 1) < N_HEADS, lf, 0.0)
    lf_ref[...] = lf

    @pl.when(pl.program_id(0) % tpb == 0)
    def _():
        carry_scr[...] = jnp.zeros_like(carry_scr)

    tri = (lax.broadcasted_iota(jnp.int32, (tm, tm), 0) >= lax.broadcasted_iota(jnp.int32, (tm, tm), 1)).astype(BF16)
    hi, mid, lo = _split3(lf)
    c = _dot(tri, hi) + _dot(tri, mid) + _dot(tri, lo) + carry_scr[0:1, :]
    c_ref[...] = c
    carry_scr[0:1, :] = c[tm - 1:tm, :]


def _fox_weight(w_in):
    d = w_in.shape[0]
    return jnp.pad(w_in, ((0, 0), (0, FOX_W_COLS - w_in.shape[1]))).astype(BF16)


def _fox_proj(x, g, scale, shift, w, gq, gk, bf, mode, tm, tpb):
    r, d = x.shape
    mod, _ = _row_specs(mode, tm, tpb, d)
    row = lambda n: pl.BlockSpec((tm, n), lambda i: (i, 0))
    full = lambda a: pl.BlockSpec(a.shape, lambda i: (0,) * a.ndim)
    head_of = jnp.arange(1024) // HEAD_DIM
    ind = (head_of[:, None] == jnp.arange(LANES)[None, :]).astype(BF16)
    gq_t = jnp.tile(gq, N_HEADS)[None]
    gk_t = jnp.tile(gk, N_HEADS)[None]
    bf_p = jnp.pad(bf, (0, LANES - bf.shape[0]))[None]
    outs = [(1024, BF16), (1024, F32), (1024, BF16), (1024, F32), (1024, BF16), (1024, F32), (LANES, F32), (LANES, F32)]
    return pl.pallas_call(
        functools.partial(_fox_proj_kernel, tpb=tpb),
        grid=(r // tm,),
        in_specs=[row(d), full(g), mod, mod, full(w), full(ind), full(ind.T), full(gq_t), full(gk_t), full(bf_p)],
        out_specs=[row(n) for n, _ in outs],
        out_shape=[jax.ShapeDtypeStruct((r, n), t) for n, t in outs],
        scratch_shapes=[pltpu.VMEM((8, LANES), F32)],
        compiler_params=_params("arbitrary"),
        name="fox_proj",
    )(x, g, scale, shift, w, ind, ind.T, gq_t, gk_t, bf_p)


def _fox_attn_kernel(q_ref, k_ref, v_ref, nc_ref, og_ref, o_ref, *, t, tk, nt):
    qi = pl.program_id(2)
    q = q_ref[0]
    lo = lax.broadcasted_iota(jnp.int32, (t, LANES), 1) < HEAD_DIM
    t_q = qi * t + lax.broadcasted_iota(jnp.int32, (t, 1), 0)
    col = lax.broadcasted_iota(jnp.int32, (t, tk), 1)
    q_heads = (jnp.where(lo, q, jnp.zeros_like(q)), jnp.where(lo, jnp.zeros_like(q), q))

    def tile(j, carry, masked):
        k0 = pl.multiple_of(j * tk, tk)
        k = k_ref[0, pl.ds(k0, tk), :]
        v = v_ref[0, pl.ds(k0, tk), :]
        out = []
        for h2, (m_i, l_i, acc) in enumerate(carry):
            s = _dot_nt(q_heads[h2], k) + nc_ref[0, 0, h2 * nt + j] * LOG2E
            if masked:
                s = jnp.where(k0 + col <= t_q, s, NEG)
            m_n = jnp.maximum(m_i, jnp.max(s, axis=-1, keepdims=True))
            alpha = jnp.exp2(m_i - m_n)
            p = jnp.exp2(s - m_n)
            l_n = alpha * l_i + jnp.sum(p, axis=-1, keepdims=True)
            out.append((m_n, l_n, alpha * acc + _dot(p.astype(BF16), v)))
        return tuple(out)

    init = (jnp.full((t, 1), NEG, F32), jnp.zeros((t, 1), F32), jnp.zeros((t, LANES), F32))
    jd = (qi * t) // tk
    carry = lax.fori_loop(0, jd, lambda j, c: tile(j, c, False), (init, init))
    for dj in range(max(1, t // tk)):
        carry = tile(jd + dj, carry, True)
    (_, l_0, acc_0), (_, l_1, acc_1) = carry
    o_ref[0] = (jnp.where(lo, acc_0 / l_0, acc_1 / l_1) * og_ref[0]).astype(BF16)


FOX_TQ = 1024
FOX_TK = 1024


def _fox_attention(q, kb, vb, c, og, b, s):
    t, tk = min(FOX_TQ, s), min(FOX_TK, s)
    nt = s // tk
    hp = N_HEADS // 2
    negc = -c[:, :N_HEADS].reshape(b, nt, tk, hp, 2).transpose(0, 3, 4, 1, 2).reshape(b, hp, 2 * nt, 1, tk)
    qspec = pl.BlockSpec((1, t, LANES), lambda i, p, qi: (i, qi, p))
    kspec = pl.BlockSpec((1, s, LANES), lambda i, p, qi: (i, 0, p))
    return pl.pallas_call(
        functools.partial(_fox_attn_kernel, t=t, tk=tk, nt=nt),
        grid=(b, hp, s // t),
        in_specs=[qspec, kspec, kspec,
                  pl.BlockSpec((1, 1, 2 * nt, 1, tk), lambda i, p, qi: (i, p, 0, 0, 0)), qspec],
        out_specs=qspec,
        out_shape=jax.ShapeDtypeStruct((b, s, 1024), BF16),
        compiler_params=_params("parallel", "parallel", "arbitrary"),
        name="fox_attention",
    )(q.reshape(b, s, -1), kb.reshape(b, s, -1), vb.reshape(b, s, -1), negc, og.reshape(b, s, -1))


def _nsa_cmp_sample_kernel(*refs, n_pages):
    pt_ref = refs[0]
    pages = refs[1:1 + n_pages]
    pe_ref, w1_ref, w2_ref, o_ref, x_scr = refs[1 + n_pages:]
    del pt_ref
    rows = n_pages * N_GROUPS
    for j in range(n_pages):
        for slot in range(2):
            for g in range(N_GROUPS):
                r0 = (g * n_pages + j) * HEAD_DIM
                x_scr[slot, r0:r0 + HEAD_DIM, :] = pages[j][0, slot, g]
    for slot in range(2):
        acc = jnp.zeros((rows, LANES), F32)
        for dd in range(HEAD_DIM):
            xl = x_scr[slot, pl.ds(dd, rows, stride=HEAD_DIM), :] + pe_ref[slot, dd:dd + 1, :]
            acc = acc + _dot(xl.astype(BF16), w1_ref[slot, dd])
        hid = acc * jax.nn.sigmoid(acc)
        o_ref[0, slot] = _dot(hid.astype(BF16), w2_ref[slot])


def _nsa_cmp_sample(pool_t, pt, pe, w1, w2, db, n_pages):
    per = PAGE // BLOCK
    pe_t = jnp.tile(pe.transpose(0, 2, 1), (1, 1, per))
    w1_d = _block_diag(w1.transpose(0, 2, 1, 3), per).astype(BF16)
    w2_d = _block_diag(w2, per).astype(BF16)
    rows = n_pages * N_GROUPS
    page = lambda j: pl.BlockSpec((1, 2, N_GROUPS, HEAD_DIM, PAGE),
                                  lambda i, pt, j=j: (pt[i * n_pages + j], 0, 0, 0, 0))
    full = lambda a: pl.BlockSpec(a.shape, lambda i, pt: (0,) * a.ndim)
    return pl.pallas_call(
        functools.partial(_nsa_cmp_sample_kernel, n_pages=n_pages),
        grid_spec=pltpu.PrefetchScalarGridSpec(
            num_scalar_prefetch=1, grid=(db,),
            in_specs=[page(j) for j in range(n_pages)] + [full(pe_t), full(w1_d), full(w2_d)],
            out_specs=pl.BlockSpec((1, 2, rows, per * HEAD_DIM), lambda i, pt: (i, 0, 0, 0)),
            scratch_shapes=[pltpu.VMEM((2, rows * HEAD_DIM, PAGE), F32)]),
        out_shape=jax.ShapeDtypeStruct((db, 2, rows, per * HEAD_DIM), F32),
        compiler_params=_params("arbitrary"),
        name="nsa_compress_sample",
    )(pt, *([pool_t] * n_pages), pe_t, w1_d, w2_d)


def _page_specs(block, slot_block, n_pages):
    nd = len(block)
    return [pl.BlockSpec((1,) + block, lambda i, pt, j=j: (pt[i * n_pages + j], slot_block) + (0,) * (nd - 1))
            for j in range(n_pages)]


def _softmax_chunks(scores):
    m = scores[0]
    for s in scores[1:]:
        m = jnp.maximum(m, s)
    m = jnp.max(m, axis=-1, keepdims=True)
    ps = [jnp.exp2(s - m) for s in scores]
    tot = ps[0]
    for p in ps[1:]:
        tot = tot + p
    return ps, jnp.sum(tot, axis=-1, keepdims=True)


def _nsa_attn_sample_kernel(*refs, n_pages, past, dt, nbs):
    pages = refs[1:1 + n_pages]
    (qc_ref, qr_ref, kc_ref, vc_ref, win_ref, kwn_ref, vwn_ref, ksn_ref, vsn_ref, e_ref, en_ref, gt_ref,
     o_ref) = refs[1 + n_pages:]
    gd = N_GROUPS * HEAD_DIM
    nr = qc_ref.shape[1]
    per_r = dt * N_GROUPS
    row = lax.broadcasted_iota(jnp.int32, (nr, 1), 0)
    t_row = (row // N_GROUPS) % dt
    t_pos = past + t_row
    tn = lax.broadcasted_iota(jnp.int32, (nr, LANES), 1)

    kc = kc_ref[0]
    nb = kc.shape[0]
    n = lax.broadcasted_iota(jnp.int32, (nr, nb), 1)
    cmask = n * BLOCK + (BLOCK - 1) <= t_pos
    sc = jnp.where(cmask, _dot_nt(qc_ref[0], kc), NEG)
    e = jnp.where(cmask, jnp.exp2(sc - jnp.max(sc, axis=-1, keepdims=True)), 0.0)
    l = jnp.sum(e, axis=-1, keepdims=True)
    pc = e / jnp.where(l > 0.0, l, 1.0)
    o_c = _dot(pc.astype(BF16), vc_ref[0])

    imp = pc[0:per_r]
    for r in range(1, GROUP):
        imp = imp + pc[r * per_r:(r + 1) * per_r]
    n1 = lax.broadcasted_iota(jnp.int32, (per_r, nb), 1)
    t1 = past + lax.broadcasted_iota(jnp.int32, (per_r, 1), 0) // N_GROUPS
    cur = t1 // BLOCK
    forced = (n1 == 0) | (n1 == cur) | (n1 == cur - 1)
    dead = (n1 * BLOCK > t1) | (n1 >= nbs)
    sel = _top_blocks(jnp.where(forced, jnp.inf, jnp.where(dead, -jnp.inf, imp)), min(N_SELECT, nbs))
    selb = jnp.concatenate([sel] * GROUP, axis=0).astype(BF16)

    qr = qr_ref[0]
    wb = win_ref.shape[-1]
    kw = win_ref[0, 0].reshape(gd, wb).astype(BF16)
    vw = win_ref[0, 1].reshape(gd, wb).astype(BF16)
    d_old = t_pos - (past - wb + lax.broadcasted_iota(jnp.int32, (nr, wb), 1))
    s_old = jnp.where((d_old >= 0) & (d_old < WINDOW), _dot(qr, kw), NEG)
    s_new = jnp.where((tn <= t_row) & (tn < dt), _dot(qr, kwn_ref[0]), NEG)
    m = jnp.maximum(jnp.max(s_old, axis=-1, keepdims=True), jnp.max(s_new, axis=-1, keepdims=True))
    p_old = jnp.exp2(s_old - m)
    p_new = jnp.exp2(s_new - m)
    lw = jnp.sum(p_old, axis=-1, keepdims=True) + jnp.sum(p_new, axis=-1, keepdims=True)
    o_w = (_dot_nt(p_old.astype(BF16), vw) + _dot_nt(p_new.astype(BF16), vwn_ref[0])) / lw

    scores = []
    for j in range(n_pages):
        k_t = pages[j][0, 0].reshape(gd, PAGE).astype(BF16)
        scores.append(jnp.where(_dot(selb, e_ref[j]) > 0.5, _dot(qr, k_t), NEG))
    ok = (_dot(selb, en_ref[...]) > 0.5) & (tn <= t_row) & (tn < dt)
    scores.append(jnp.where(ok, _dot(qr, ksn_ref[0]), NEG))
    ps, ls = _softmax_chunks(scores)
    acc = _dot_nt(ps[n_pages].astype(BF16), vsn_ref[0])
    for j in range(n_pages):
        acc = acc + _dot_nt(ps[j].astype(BF16), pages[j][0, 1].reshape(gd, PAGE).astype(BF16))
    gt = gt_ref[0]
    o_ref[0] = gt[:, 0:1] * o_c + gt[:, 1:2] * (acc / ls) + gt[:, 2:3] * o_w


def _nsa_attn_sample(pool_t, pt, qbd_c, qbd_r, kc, vc, win_t, kwn, vwn, ksn, vsn, gates, past, dt, nbs, n_pages):
    db, nr, gd = qbd_r.shape
    key_blk = jnp.arange(n_pages * PAGE) // BLOCK
    e_tab = (jnp.arange(LANES)[None, :, None] == key_blk.reshape(n_pages, 1, PAGE)).astype(BF16)
    new_blk = jnp.where(jnp.arange(LANES) < dt, (past + jnp.arange(LANES)) // BLOCK, -1)
    e_new = (jnp.arange(LANES)[:, None] == new_blk[None, :]).astype(BF16)
    blk = lambda a: pl.BlockSpec((1,) + a.shape[1:], lambda i, pt: (i,) + (0,) * (a.ndim - 1))
    full = lambda a: pl.BlockSpec(a.shape, lambda i, pt: (0,) * a.ndim)
    per_batch = (qbd_c, qbd_r, kc, vc, win_t, kwn, vwn, ksn, vsn)
    return pl.pallas_call(
        functools.partial(_nsa_attn_sample_kernel, n_pages=n_pages, past=past, dt=dt, nbs=nbs),
        grid_spec=pltpu.PrefetchScalarGridSpec(
            num_scalar_prefetch=1, grid=(db,),
            in_specs=(_page_specs((2, N_GROUPS, HEAD_DIM, PAGE), 1, n_pages) + [blk(a) for a in per_batch]
                      + [full(e_tab), full(e_new), blk(gates)]),
            out_specs=pl.BlockSpec((1, nr, gd), lambda i, pt: (i, 0, 0))),
        out_shape=jax.ShapeDtypeStruct((db, nr, gd), F32),
        compiler_params=_params("parallel"),
        name="nsa_attention_sample",
    )(pt, *([pool_t] * n_pages), *per_batch, e_tab, e_new, gates)


def _fox_attn_sample_kernel(*refs, n_pages, dt):
    pages = refs[1:1 + n_pages]
    lfs = refs[1 + n_pages:1 + 2 * n_pages]
    q_ref, kn_ref, vn_ref, lfn_ref, og_ref, o_ref = refs[1 + 2 * n_pages:]
    hd = N_HEADS * HEAD_DIM
    tri = (lax.broadcasted_iota(jnp.int32, (PAGE, PAGE), 0) <= lax.broadcasted_iota(jnp.int32, (PAGE, PAGE), 1)).astype(BF16)

    def local_cum(lf):
        hi, mid, lo = _split3(lf)
        return _dot(hi, tri) + _dot(mid, tri) + _dot(lo, tri)

    q = q_ref[0]
    nr = q.shape[0]
    scores = []
    prefix = jnp.zeros((N_HEADS, 1), F32)
    for j in range(n_pages):
        loc = local_cum(lfs[j][0])
        c_page = (loc + prefix) * LOG2E
        prefix = prefix + loc[:, PAGE - 1:PAGE]
        k_t = pages[j][0, 0].reshape(hd, PAGE).astype(BF16)
        scores.append(_dot(q, k_t) - jnp.concatenate([c_page] * dt, axis=0))
    c_new = (local_cum(lfn_ref[0]) + prefix) * LOG2E
    t_row = lax.broadcasted_iota(jnp.int32, (nr, 1), 0) // N_HEADS
    tn = lax.broadcasted_iota(jnp.int32, (nr, LANES), 1)
    s_n = _dot(q, kn_ref[0]) - jnp.concatenate([c_new] * dt, axis=0)
    scores.append(jnp.where((tn <= t_row) & (tn < dt), s_n, NEG))
    ps, ls = _softmax_chunks(scores)
    acc = _dot_nt(ps[n_pages].astype(BF16), vn_ref[0])
    for j in range(n_pages):
        acc = acc + _dot_nt(ps[j].astype(BF16), pages[j][0, 1].reshape(hd, PAGE).astype(BF16))
    o_ref[0] = acc / ls * og_ref[0]


def _fox_attn_sample(pool_t, lf_pool_t, pt, qbd, kn_t, vn_t, lfn_t, og_t, dt, n_pages):
    db, nr, hd = qbd.shape
    blk = lambda a: pl.BlockSpec((1,) + a.shape[1:], lambda i, pt: (i,) + (0,) * (a.ndim - 1))
    lf_specs = [pl.BlockSpec((1, N_HEADS, PAGE), lambda i, pt, j=j: (pt[i * n_pages + j], 0, 0))
                for j in range(n_pages)]
    per_batch = (qbd, kn_t, vn_t, lfn_t, og_t)
    return pl.pallas_call(
        functools.partial(_fox_attn_sample_kernel, n_pages=n_pages, dt=dt),
        grid_spec=pltpu.PrefetchScalarGridSpec(
            num_scalar_prefetch=1, grid=(db,),
            in_specs=_page_specs((2, N_HEADS, HEAD_DIM, PAGE), 0, n_pages) + lf_specs + [blk(a) for a in per_batch],
            out_specs=pl.BlockSpec((1, nr, hd), lambda i, pt: (i, 0, 0))),
        out_shape=jax.ShapeDtypeStruct((db, nr, hd), F32),
        compiler_params=_params("parallel"),
        name="fox_attention_sample",
    )(pt, *([pool_t] * n_pages), *([lf_pool_t] * n_pages), *per_batch)


def _per_batch(a, dt, db):
    return a.reshape(dt, db, -1).transpose(1, 0, 2)


def _new_keys_t(a, dt, db):
    a = _per_batch(a, dt, db).transpose(0, 2, 1)
    return jnp.pad(a, ((0, 0), (0, 0), (0, LANES - dt))).astype(a.dtype)


def _nsa_qbd(q, dt, db):
    q5 = q.reshape(dt, db, N_GROUPS, GROUP, HEAD_DIM).transpose(1, 3, 0, 2, 4)
    eye = jnp.eye(N_GROUPS, dtype=q.dtype)
    out = q5[:, :, :, :, None, :] * eye[None, None, None, :, :, None]
    return out.reshape(db, GROUP * dt * N_GROUPS, N_GROUPS * HEAD_DIM)


def _nsa_undiag(o, dt, db):
    o6 = o.reshape(db, GROUP, dt, N_GROUPS, N_GROUPS, HEAD_DIM)
    dg = jnp.diagonal(o6, axis1=3, axis2=4)
    return dg.transpose(2, 0, 4, 1, 3).reshape(dt * db, N_HEADS * HEAD_DIM)


def _fox_qbd(q, dt, db):
    q4 = q.reshape(dt, db, N_HEADS, HEAD_DIM).transpose(1, 0, 2, 3)
    eye = jnp.eye(N_HEADS, dtype=q.dtype)
    out = q4[:, :, :, None, :] * eye[None, None, :, :, None]
    return out.reshape(db, dt * N_HEADS, N_HEADS * HEAD_DIM)


def _fox_undiag(o, dt, db):
    o5 = o.reshape(db, dt, N_HEADS, N_HEADS, HEAD_DIM)
    dg = jnp.diagonal(o5, axis1=2, axis2=3)
    return dg.transpose(1, 0, 3, 2).reshape(dt * db, N_HEADS * HEAD_DIM)


def _prompt_mods(mod, b):
    return [m.reshape(b, 1, -1) for m in jnp.split(mod[:b], 6, axis=-1)]


def _sample_mods(mod, b, t):
    return [jnp.tile(m, (t, 1))[None] for m in jnp.split(mod[b:], 6, axis=-1)]


def kernel(x_prompt, x_sample, cache_nsa_kv, cache_nsa_win, cache_fox_kv, cache_fox_logf, state_ffn_conv, page_table, c_prompt, c_sample, w_ada, b_ada, norm_mix_g, norm_ffn_g, w_nsa_in, pe_cmp, w_cmp1, w_cmp2, w_nsa_out, w_fox_in, b_fox_f, fox_q_norm_g, fox_k_norm_g, w_fox_out, w_ffn_up, ffn_conv_w, ffn_conv_b, w_ffn_down, final_norm_g):
    b, s, d = x_prompt.shape
    db, dt, _ = x_sample.shape
    f_dim = w_ffn_down.shape[1]
    depth = w_ada.shape[0]
    tm = 256
    tpb = s // tm
    tm_f = 512
    tpb_f = s // tm_f

    n_pages = page_table.shape[1]
    past = n_pages * PAGE
    nbs = -(-(past + dt) // BLOCK)
    per = PAGE // BLOCK
    r_s = dt * db
    pt = page_table.reshape(-1).astype(jnp.int32)
    key_last = (0, 2, 3, 4, 1)

    c_all = jnp.concatenate([c_prompt, c_sample], axis=0)
    xp = x_prompt.reshape(b * s, d)
    xs = x_sample.transpose(1, 0, 2).reshape(r_s, d)
    tabs_p = _rope_tables(jnp.arange(s, dtype=jnp.int32))
    tabs_s = _rope_tables(past + jnp.arange(r_s, dtype=jnp.int32) // db)

    nsa_kv_p, nsa_win_p, fox_kv_p, fox_lf_p, conv_p = [], [], [], [], []
    nsa_kv_s, nsa_win_s, fox_kv_s, fox_lf_s, conv_s = [], [], [], [], []
    y_prompt = y_sample = None
    for i in range(depth):
        j = i // 2
        mod = _adaln(c_all, w_ada[i].astype(BF16), b_ada[i][None])
        mp = _prompt_mods(mod, b)
        ms = _sample_mods(mod, b, dt)
        g_mix = norm_mix_g[i][None]
        if i % 2 == 0:
            w_in = _nsa_weight(w_nsa_in[j])
            qc, qr, rows, win, dup, gates = _nsa_proj(xp, g_mix, mp[1], mp[0], w_in, tabs_p, "prompt", tm, tpb)
            cdup = _compress_prompt(rows, *_cmp_weights(pe_cmp[j], w_cmp1[j], w_cmp2[j]), b, s)
            attn_p = _nsa_attention_t(qc, qr, cdup, dup, gates, b, s)
            w_out = w_nsa_out[j].astype(BF16)
            seq_last = lambda a, lead: a.transpose(0, 2, 1).reshape((b,) + lead + (a.shape[1],)).transpose(0, 4, 1, 2, 3)
            nsa_kv_p.append(seq_last(rows.reshape(b, s, -1), (4, N_GROUPS, HEAD_DIM)))
            wl = min(WINDOW, s)
            nsa_win_p.append(seq_last(win.reshape(b, s, -1)[:, s - wl:], (2, N_GROUPS, HEAD_DIM)))

            qc, qr, rows, win, _, gates = _nsa_proj(xs, g_mix, ms[1], ms[0], w_in, tabs_s, "sample", r_s, 1)
            nsa_kv_s.append(rows.reshape(dt, db, 4, N_GROUPS, HEAD_DIM).transpose(1, 0, 2, 3, 4))
            nsa_win_s.append(win.reshape(dt, db, 2, N_GROUPS, HEAD_DIM).transpose(1, 0, 2, 3, 4))
            pool_t = jnp.transpose(cache_nsa_kv[j], key_last)
            kc2 = _nsa_cmp_sample(pool_t, pt, pe_cmp[j], w_cmp1[j], w_cmp2[j], db, n_pages)
            kc = kc2.reshape(db, 2, N_GROUPS, n_pages, per, HEAD_DIM).transpose(0, 1, 3, 4, 2, 5)
            kc = kc.reshape(db, 2, n_pages * per, N_GROUPS * HEAD_DIM)
            kc = jnp.pad(kc, ((0, 0), (0, 0), (0, LANES - n_pages * per), (0, 0))).astype(BF16)
            qbd_c, qbd_r = _nsa_qbd(qc, dt, db), _nsa_qbd(qr, dt, db)
            gd = N_GROUPS * HEAD_DIM
            newt = lambda a: _new_keys_t(a.astype(BF16), dt, db)
            g_s = gates.reshape(dt, db, N_GROUPS, LANES)[..., :GROUP * 3].reshape(dt, db, N_GROUPS, GROUP, 3)
            g_s = g_s.transpose(1, 3, 0, 2, 4).reshape(db, GROUP * dt * N_GROUPS, 3)
            g_s = jnp.pad(g_s, ((0, 0), (0, 0), (0, LANES - 3)))
            o_s = _nsa_attn_sample(pool_t, pt, qbd_c, qbd_r, kc[:, 0], kc[:, 1],
                                   jnp.transpose(cache_nsa_win[j], key_last), newt(win[:, :gd]), newt(win[:, gd:]),
                                   newt(rows[:, 2 * gd:3 * gd]), newt(rows[:, 3 * gd:]), g_s, past, dt, nbs, n_pages)
            attn_s = _nsa_undiag(o_s, dt, db).astype(BF16)
        else:
            w_in = _fox_weight(w_fox_in[j])
            fox = lambda x, sc, sh, mode, t, n: _fox_proj(x, g_mix, sc, sh, w_in, fox_q_norm_g[j], fox_k_norm_g[j],
                                                         b_fox_f[j], mode, t, n)
            q, kf, kb, vf, vb, og, lf, c = fox(xp, mp[1], mp[0], "prompt", tm, tpb)
            attn_p = _fox_attention(q, kb, vb, c, og, b, s).reshape(b * s, d)
            w_out = w_fox_out[j].astype(BF16)
            hs = (N_HEADS, HEAD_DIM)
            kv_t = jnp.stack([kf.reshape(b, s, -1).transpose(0, 2, 1), vf.reshape(b, s, -1).transpose(0, 2, 1)], axis=1)
            fox_kv_p.append(kv_t.reshape((b, 2) + hs + (s,)).transpose(0, 4, 1, 2, 3))
            fox_lf_p.append(lf[:, :N_HEADS].reshape(b, s, N_HEADS))

            q, kf, kb, vf, vb, og, lf, _ = fox(xs, ms[1], ms[0], "sample", r_s, 1)
            fox_kv_s.append(jnp.stack([kf.reshape((dt, db) + hs), vf.reshape((dt, db) + hs)], axis=2).transpose(1, 0, 2, 3, 4))
            fox_lf_s.append(lf[:, :N_HEADS].reshape(dt, db, N_HEADS).transpose(1, 0, 2))
            og_t = jnp.repeat(_per_batch(og, dt, db)[:, :, None, :], N_HEADS, axis=2).reshape(db, dt * N_HEADS, -1)
            o_full = _fox_attn_sample(jnp.transpose(cache_fox_kv[j], key_last),
                                      jnp.transpose(cache_fox_logf[j], (0, 2, 1)), pt, _fox_qbd(q, dt, db),
                                      _new_keys_t(kb, dt, db), _new_keys_t(vb, dt, db),
                                      _new_keys_t(lf[:, :N_HEADS], dt, db), og_t, dt, n_pages)
            attn_s = _fox_undiag(o_full, dt, db).astype(BF16)

        final = final_norm_g[None] if i == depth - 1 else None
        ffn_w = (norm_ffn_g[i][None], w_ffn_up[i].astype(BF16), ffn_conv_w[i], ffn_conv_b[i][None],
                 w_ffn_down[i].astype(BF16))

        res = _ffn(xp, attn_p, w_out, (mp[2], mp[4], mp[3], mp[5]), *ffn_w, "prompt", tm_f, tpb_f, 1, final_g=final)
        xp = res[0]
        conv_p.append(res[1].reshape(b, tpb_f, 8, 2 * f_dim)[:, -1, 6:, :])
        if final is not None:
            y_prompt = res[2].reshape(b, s, d)

        state = state_ffn_conv[i].transpose(1, 0, 2).reshape(2 * db, 2 * f_dim)
        res = _ffn(xs, attn_s, w_out, (ms[2], ms[4], ms[3], ms[5]), *ffn_w, "sample", r_s, 1, db, state=state,
                   final_g=final)
        xs = res[0]
        conv_s.append(res[1].reshape(2, db, 2 * f_dim).transpose(1, 0, 2))
        if final is not None:
            y_sample = res[2].reshape(dt, db, d).transpose(1, 0, 2)

    return (y_prompt, y_sample, jnp.stack(nsa_kv_p), jnp.stack(nsa_kv_s), jnp.stack(nsa_win_p), jnp.stack(nsa_win_s),
            jnp.stack(fox_kv_p), jnp.stack(fox_kv_s), jnp.stack(fox_lf_p), jnp.stack(fox_lf_s),
            jnp.stack(conv_p), jnp.stack(conv_s))
```
